```python
import math
import jax, jax.numpy as jnp
from jax import lax
import numpy as np

D_MODEL = 2048
BATCH = 4
SEQ = 2048
DEPTH = 2
DEC_BATCH = 1
DEC_SEQ = 16384
PAST_LEN = 128

ATT_HEADS = 16
ATT_HEAD_DIM = 128
ATT_WIDTH = ATT_HEADS * ATT_HEAD_DIM
DILATED_PATTERNS = ((128, 1), (512, 4), (2048, 16))
POOL_WINDOWS = (2, 4, 8, 16)
POOL_WIDTH = D_MODEL // 2
POOL_GROUP = POOL_WIDTH // len(POOL_WINDOWS)
CONV_WIDTH = D_MODEL
CONV_TAPS = 3
REL_BUCKETS = 32
REL_MAX_DISTANCE = 1024
DN_ALPHA = (2 * DEPTH) ** 0.25
DN_BETA = (8 * DEPTH) ** -0.25
LN_EPS = 1e-5
NEG_INF = -1e30

N_EVEN = (DEPTH + 1) // 2
N_ODD = DEPTH // 2
IN_AB = 4 * ATT_WIDTH + 2 * POOL_WIDTH
IN_C = 4 * CONV_WIDTH

kernel_name = "hybrid_dilated_pool_shortconv_encoder"


def _t5_bucket(rel):
    nb = REL_BUCKETS // 2
    max_exact = nb // 2
    ret = np.where(rel > 0, nb, 0)
    n = np.abs(rel)
    n_safe = np.maximum(n, 1).astype(np.float64)
    large = max_exact + (np.log(n_safe / max_exact) / math.log(REL_MAX_DISTANCE / max_exact)
                         * (nb - max_exact)).astype(np.int64)
    large = np.minimum(large, nb - 1)
    return (ret + np.where(n < max_exact, n, large)).astype(np.int32)


def _layernorm(x, g, b):
    xf = x.astype(jnp.float32)
    mu = jnp.mean(xf, axis=-1, keepdims=True)
    var = jnp.mean(jnp.square(xf - mu), axis=-1, keepdims=True)
    return ((xf - mu) * lax.rsqrt(var + LN_EPS) * g + b).astype(x.dtype)


def _dilated_pattern(q, k, v, rel_bias, window, dilation):
    b, s, h, hd = q.shape
    r = window // (2 * dilation)
    L = s // dilation
    nb = -(-L // r)
    Lp = nb * r

    def to_strided(t):
        t = t.reshape(b, L, dilation, h, hd).transpose(0, 2, 1, 3, 4)
        return jnp.pad(t, ((0, 0), (0, 0), (0, Lp - L), (0, 0), (0, 0)))

    def band(t):
        tp = jnp.pad(t, ((0, 0), (0, 0), (r, r), (0, 0), (0, 0))).reshape(b, dilation, nb + 2, r, h, hd)
        return jnp.concatenate([tp[:, :, :-2], tp[:, :, 1:-1], tp[:, :, 2:]], axis=3)

    qb = to_strided(q).reshape(b, dilation, nb, r, h, hd)
    kb = band(to_strided(k))
    vb = band(to_strided(v))

    rel = np.arange(3 * r)[None, :] - r - np.arange(r)[:, None]
    in_band = np.abs(rel) <= r
    kpos = np.arange(nb)[:, None] * r - r + np.arange(3 * r)[None, :]
    valid = (kpos >= 0) & (kpos < L)
    mask = jnp.asarray(in_band[None, :, :] & valid[:, None, :])
    bias = rel_bias[_t5_bucket(rel * dilation)].astype(jnp.float32).transpose(2, 0, 1)

    scores = jnp.einsum('bgnqhd,bgnkhd->bgnhqk', qb, kb,
                        preferred_element_type=jnp.float32) * (ATT_HEAD_DIM ** -0.5)
    scores = jnp.where(mask[None, None, :, None], scores + bias[None, None, None], NEG_INF)
    m = jnp.max(scores, axis=-1, keepdims=True)
    p = jnp.exp(scores - m)
    den = jnp.sum(p, axis=-1)
    o = jnp.einsum('bgnhqk,bgnkhd->bgnqhd', p.astype(v.dtype), vb,
                   preferred_element_type=jnp.float32)
    o = o / den.transpose(0, 1, 2, 4, 3)[..., None]
    lse = m[..., 0] + jnp.log(den)

    o = o.reshape(b, dilation, Lp, h, hd)[:, :, :L].transpose(0, 2, 1, 3, 4).reshape(b, s, h, hd)
    lse = lse.transpose(0, 1, 2, 4, 3).reshape(b, dilation, Lp, h)[:, :, :L]
    lse = lse.transpose(0, 2, 1, 3).reshape(b, s, h)
    return o, lse


def _dilated_attention(q, k, v, rel_bias):
    results = [_dilated_pattern(q, k, v, rel_bias, w, d) for (w, d) in DILATED_PATTERNS]
    outs = jnp.stack([o for o, _ in results], axis=0)
    lses = jnp.stack([l for _, l in results], axis=0)
    wts = jax.nn.softmax(lses, axis=0)
    return jnp.einsum('pbsh,pbshd->bshd', wts, outs)


def _multiscale_pool_minus_self(u):
    b, s, g, c = u.shape
    uf = u.astype(jnp.float32)
    cs = jnp.pad(jnp.cumsum(uf, axis=1), ((0, 0), (1, 0), (0, 0), (0, 0)))
    pos = np.arange(s)
    outs = []
    for gi, w in enumerate(POOL_WINDOWS):
        lo = np.maximum(pos - w // 2, 0)
        hi = np.minimum(pos + w // 2 - 1, s - 1)
        cnt = jnp.asarray((hi - lo + 1).astype(np.float32))
        csg = cs[:, :, gi]
        mean = (csg[:, hi + 1] - csg[:, lo]) / cnt[None, :, None]
        outs.append(mean - uf[:, :, gi])
    return jnp.stack(outs, axis=2)


def _dwconv3(u, w):
    up = jnp.pad(u, ((0, 0), (1, 1), (0, 0)))
    return up[:, :-2] * w[0] + up[:, 1:-1] * w[1] + up[:, 2:] * w[2]


def _even_layer(x, rel_bias, w_in, pool_w, pool_scale, w_out):
    b, s, _ = x.shape
    hproj = x @ w_in
    q, k, v, g_a, u_b, g_b = jnp.split(
        hproj, [ATT_WIDTH, 2 * ATT_WIDTH, 3 * ATT_WIDTH, 4 * ATT_WIDTH, 4 * ATT_WIDTH + POOL_WIDTH], axis=-1)
    shp = (b, s, ATT_HEADS, ATT_HEAD_DIM)
    o_a = _dilated_attention(q.reshape(shp), k.reshape(shp), v.reshape(shp), rel_bias)
    o_a = o_a.reshape(b, s, ATT_WIDTH).astype(x.dtype) * jax.nn.silu(g_a)
    pooled = _multiscale_pool_minus_self(u_b.reshape(b, s, len(POOL_WINDOWS), POOL_GROUP))
    o_b = jnp.einsum('bsgc,gcd->bsgd', pooled.astype(x.dtype), pool_w).reshape(b, s, POOL_WIDTH)
    o_b = o_b * pool_scale * jax.nn.silu(g_b)
    return jnp.concatenate([o_a, o_b], axis=-1) @ w_out


def _odd_layer(x, w_in, conv_w, w_out):
    hproj = x @ w_in
    g_b, g_c, val, gate = jnp.split(hproj, [CONV_WIDTH, 2 * CONV_WIDTH, 3 * CONV_WIDTH], axis=-1)
    y = g_b * _dwconv3(g_c * val, conv_w) * jax.nn.silu(gate)
    return y @ w_out


def _trunk(x, rel_bias, w_in_ab, pool_w, pool_scale, w_out_ab, w_in_c, conv_w, w_out_c, ln_g, ln_b):
    for layer in range(DEPTH):
        i = layer // 2
        if layer % 2 == 0:
            f = _even_layer(x, rel_bias, w_in_ab[i], pool_w[i], pool_scale[i], w_out_ab[i])
        else:
            f = _odd_layer(x, w_in_c[i], conv_w[i], w_out_c[i])
        x = _layernorm(DN_ALPHA * x + f, ln_g[layer], ln_b[layer])
    return x


def setup_inputs(seed: int = 0) -> dict:
    key = jax.random.key(seed)
    ks = jax.random.split(key, 12)
    f32 = jnp.float32
    nrm = lambda k, shape, scale: jax.random.normal(k, shape, f32) * scale
    return {
        "x_prompt": nrm(ks[0], (BATCH, SEQ, D_MODEL), 1.0),
        "x_sample": nrm(ks[1], (DEC_BATCH, DEC_SEQ, D_MODEL), 1.0),
        "rel_bias": nrm(ks[2], (REL_BUCKETS, ATT_HEADS), 0.2),
        "w_in_ab": nrm(ks[3], (N_EVEN, D_MODEL, IN_AB), D_MODEL ** -0.5),
        "pool_w": nrm(ks[4], (N_EVEN, len(POOL_WINDOWS), POOL_GROUP, POOL_GROUP), POOL_GROUP ** -0.5),
        "pool_scale": 1.0 + nrm(ks[5], (N_EVEN, POOL_WIDTH), 0.1),
        "w_out_ab": nrm(ks[6], (N_EVEN, ATT_WIDTH + POOL_WIDTH, D_MODEL),
                         DN_BETA * (ATT_WIDTH + POOL_WIDTH) ** -0.5),
        "w_in_c": nrm(ks[7], (N_ODD, D_MODEL, IN_C), D_MODEL ** -0.5),
        "conv_w": nrm(ks[8], (N_ODD, CONV_TAPS, CONV_WIDTH), CONV_TAPS ** -0.5),
        "w_out_c": nrm(ks[9], (N_ODD, CONV_WIDTH, D_MODEL), DN_BETA * CONV_WIDTH ** -0.5),
        "ln_g": 1.0 + nrm(ks[10], (DEPTH, D_MODEL), 0.02),
        "ln_b": nrm(ks[11], (DEPTH, D_MODEL), 0.02),
    }


def reference(x_prompt, x_sample, rel_bias, w_in_ab, pool_w, pool_scale, w_out_ab,
              w_in_c, conv_w, w_out_c, ln_g, ln_b):
    y_prompt = _trunk(x_prompt, rel_bias, w_in_ab, pool_w, pool_scale, w_out_ab,
                      w_in_c, conv_w, w_out_c, ln_g, ln_b)
    y_sample = _trunk(x_sample, rel_bias, w_in_ab, pool_w, pool_scale, w_out_ab,
                      w_in_c, conv_w, w_out_c, ln_g, ln_b)
    return (y_prompt, y_sample)
```

```python
import functools
import math

import numpy as np
import jax
import jax.numpy as jnp
from jax import lax
from jax.experimental import pallas as pl
from jax.experimental.pallas import tpu as pltpu

D_MODEL = 2048
DEPTH = 2
ATT_HEADS = 16
HEAD_DIM = 128
ATT_WIDTH = ATT_HEADS * HEAD_DIM
DILATED_PATTERNS = ((128, 1), (512, 4), (2048, 16))
DILATIONS = tuple(d for _, d in DILATED_PATTERNS)
RADIUS = 64
POOL_WINDOWS = (2, 4, 8, 16)
POOL_WIDTH = D_MODEL // 2
POOL_GROUP = POOL_WIDTH // len(POOL_WINDOWS)
CONV_WIDTH = D_MODEL
REL_BUCKETS = 32
REL_MAX_DISTANCE = 1024
DN_ALPHA = (2 * DEPTH) ** 0.25
LN_EPS = 1e-5
NEG_INF = -1e30

LANES = 128
HALO = 16
ATT_Q_ROWS = 128
ATT_KEYS = ATT_Q_ROWS + 2 * RADIUS
VMEM_LIMIT = 56 * 1024 * 1024

BF16 = jnp.bfloat16
F32 = jnp.float32

assert all(w // (2 * d) == RADIUS for w, d in DILATED_PATTERNS)


def _t5_bucket(rel):
    nb = REL_BUCKETS // 2
    max_exact = nb // 2
    ret = np.where(rel > 0, nb, 0)
    n = np.abs(rel)
    n_safe = np.maximum(n, 1).astype(np.float64)
    large = max_exact + (np.log(n_safe / max_exact) / math.log(REL_MAX_DISTANCE / max_exact)
                         * (nb - max_exact)).astype(np.int64)
    large = np.minimum(large, nb - 1)
    return (ret + np.where(n < max_exact, n, large)).astype(np.int32)


def _band_bias(rel_bias, dilation):
    rel = np.arange(ATT_KEYS)[None, :] - RADIUS - np.arange(ATT_Q_ROWS)[:, None]
    in_band = jnp.asarray(np.abs(rel) <= RADIUS)
    bias = rel_bias[_t5_bucket(rel * dilation)].astype(F32).transpose(2, 0, 1)
    return jnp.where(in_band[None], bias, NEG_INF)


def _silu(x):
    return x / (1.0 + jnp.exp(-x))


def _layernorm(z, g, b):
    mu = jnp.mean(z, axis=-1, keepdims=True)
    zc = z - mu
    var = jnp.mean(zc * zc, axis=-1, keepdims=True)
    return zc * lax.rsqrt(var + LN_EPS) * g + b


def _qkv_kernel(x_ref, w_ref, o1_ref, o4_ref, o16_ref, xb_ref, slab_ref, *, tm, tn):
    j = pl.program_id(2)

    @pl.when(j == 0)
    def _():
        xb_ref[...] = x_ref[...].astype(BF16)

    acc = jnp.dot(xb_ref[...], w_ref[...], preferred_element_type=F32)
    acc = acc * jnp.where(j < ATT_WIDTH // tn, HEAD_DIM ** -0.5, 1.0)
    ns = tn // LANES
    for c in range(ns):
        blk = acc[:, c * LANES:(c + 1) * LANES]
        slab_ref[c] = blk
        o1_ref[0, c] = blk.astype(BF16)
    for d, o_ref in ((4, o4_ref), (16, o16_ref)):
        for g in range(d):
            for c in range(ns):
                o_ref[g, c] = slab_ref[c, pl.ds(g, tm // d, stride=d), :].astype(BF16)


def _qkv_proj(x, w, *, tm=1024, tn=512):
    b, s, dm = x.shape
    n = w.shape[1]
    nslab = n // LANES
    outs = [jax.ShapeDtypeStruct((b, d, nslab, s // d, LANES), BF16) for d in DILATIONS]
    out_specs = [pl.BlockSpec((None, d, tn // LANES, tm // d, LANES), lambda bi, i, j: (bi, 0, j, i, 0))
                 for d in DILATIONS]
    return pl.pallas_call(
        functools.partial(_qkv_kernel, tm=tm, tn=tn),
        grid=(b, s // tm, n // tn),
        in_specs=[pl.BlockSpec((None, tm, dm), lambda bi, i, j: (bi, i, 0)),
                  pl.BlockSpec((dm, tn), lambda bi, i, j: (0, j))],
        out_specs=out_specs,
        out_shape=outs,
        scratch_shapes=[pltpu.VMEM((tm, dm), BF16), pltpu.VMEM((tn // LANES, tm, LANES), F32)],
        compiler_params=pltpu.CompilerParams(
            dimension_semantics=("parallel", "parallel", "arbitrary"), vmem_limit_bytes=VMEM_LIMIT),
        name="qkv_proj",
    )(x, w)


def _gate_kernel(x_ref, w_ref, o_ref, xb_ref):
    @pl.when(pl.program_id(1) == 0)
    def _():
        xb_ref[...] = x_ref[...].astype(BF16)

    acc = jnp.dot(xb_ref[...], w_ref[...], preferred_element_type=F32)
    o_ref[...] = _silu(acc).astype(BF16)


def _gate_proj(x2, w, *, tm=1024, tn=512):
    m, dm = x2.shape
    n = w.shape[1]
    return pl.pallas_call(
        _gate_kernel,
        grid=(m // tm, n // tn),
        in_specs=[pl.BlockSpec((tm, dm), lambda i, j: (i, 0)),
                  pl.BlockSpec((dm, tn), lambda i, j: (0, j))],
        out_specs=pl.BlockSpec((tm, tn), lambda i, j: (i, j)),
        out_shape=jax.ShapeDtypeStruct((m, n), BF16),
        scratch_shapes=[pltpu.VMEM((tm, dm), BF16)],
        compiler_params=pltpu.CompilerParams(
            dimension_semantics=("parallel", "arbitrary"), vmem_limit_bytes=VMEM_LIMIT),
        name="gate_proj",
    )(x2, w)


def _pool_kernel(xp_ref, x_ref, xn_ref, wu_ref, wg_ref, pw_ref, ps_ref, o_ref, xb_ref, u_ref, *, tm, seq):
    i = pl.program_id(0)
    tiles_per_seq = seq // tm
    first = (i % tiles_per_seq) == 0
    last = (i % tiles_per_seq) == tiles_per_seq - 1

    xb_ref[0:HALO] = xp_ref[...].astype(BF16)
    xb_ref[HALO:HALO + tm] = x_ref[...].astype(BF16)
    xb_ref[HALO + tm:] = xn_ref[...].astype(BF16)
    u_ref[...] = jnp.dot(xb_ref[...], wu_ref[...], preferred_element_type=F32)

    @pl.when(first)
    def _():
        u_ref[0:HALO] = jnp.zeros((HALO, POOL_WIDTH), F32)

    @pl.when(last)
    def _():
        u_ref[HALO + tm:] = jnp.zeros((HALO, POOL_WIDTH), F32)

    gate = jnp.dot(xb_ref[HALO:HALO + tm], wg_ref[...], preferred_element_type=F32)
    pos = (i % tiles_per_seq) * tm + lax.broadcasted_iota(jnp.int32, (tm, POOL_GROUP), 0)
    for gi, w in enumerate(POOL_WINDOWS):
        cols = slice(gi * POOL_GROUP, (gi + 1) * POOL_GROUP)
        tot = u_ref[pl.ds(HALO - w // 2, tm), cols]
        for off in range(-w // 2 + 1, w // 2):
            tot = tot + u_ref[pl.ds(HALO + off, tm), cols]
        lo = jnp.maximum(pos - w // 2, 0)
        hi = jnp.minimum(pos + w // 2 - 1, seq - 1)
        cnt = (hi - lo + 1).astype(F32)
        pooled = tot / cnt - u_ref[pl.ds(HALO, tm), cols]
        ob = jnp.dot(pooled.astype(BF16), pw_ref[gi], preferred_element_type=F32)
        o_ref[:, cols] = (ob * ps_ref[:, cols] * _silu(gate[:, cols])).astype(BF16)


def _pool_branch(x2, wu, wg, pool_w, pool_scale, *, seq, tm=512):
    m, dm = x2.shape
    hb = tm // HALO
    nhb = m // HALO
    return pl.pallas_call(
        functools.partial(_pool_kernel, tm=tm, seq=seq),
        grid=(m // tm,),
        in_specs=[pl.BlockSpec((HALO, dm), lambda i: (jnp.maximum(i * hb - 1, 0), 0)),
                  pl.BlockSpec((tm, dm), lambda i: (i, 0)),
                  pl.BlockSpec((HALO, dm), lambda i: (jnp.minimum((i + 1) * hb, nhb - 1), 0)),
                  pl.BlockSpec((dm, POOL_WIDTH), lambda i: (0, 0)),
                  pl.BlockSpec((dm, POOL_WIDTH), lambda i: (0, 0)),
                  pl.BlockSpec((len(POOL_WINDOWS), POOL_GROUP, POOL_GROUP), lambda i: (0, 0, 0)),
                  pl.BlockSpec((1, POOL_WIDTH), lambda i: (0, 0))],
        out_specs=pl.BlockSpec((tm, POOL_WIDTH), lambda i: (i, 0)),
        out_shape=jax.ShapeDtypeStruct((m, POOL_WIDTH), BF16),
        scratch_shapes=[pltpu.VMEM((tm + 2 * HALO, dm), BF16), pltpu.VMEM((tm + 2 * HALO, POOL_WIDTH), F32)],
        compiler_params=pltpu.CompilerParams(dimension_semantics=("parallel",), vmem_limit_bytes=VMEM_LIMIT),
        name="pool_branch",
    )(x2, x2, x2, wu, wg, pool_w, pool_scale)


def _attn_kernel(q_ref, kp_ref, km_ref, kn_ref, vp_ref, vm_ref, vn_ref, bias_ref,
                 o_ref, m_ref, l_ref, *, qb, length):
    n = pl.program_id(1)
    na = qb // ATT_Q_ROWS
    m_ref[...] = jnp.zeros_like(m_ref)
    l_ref[...] = jnp.ones_like(l_ref)
    lane = lax.broadcasted_iota(jnp.int32, (ATT_Q_ROWS, LANES), 1)
    key_off = lax.broadcasted_iota(jnp.int32, (1, ATT_KEYS), 1) - RADIUS

    def window(prev_ref, main_ref, next_ref, h, a):
        lo = a * ATT_Q_ROWS - RADIUS
        hi = lo + ATT_KEYS
        parts = []
        if lo < 0:
            parts.append(prev_ref[h])
        parts.append(main_ref[h, max(lo, 0):min(hi, qb), :])
        if hi > qb:
            parts.append(next_ref[h])
        return parts[0] if len(parts) == 1 else jnp.concatenate(parts, axis=0)

    def head(h, carry):
        sel = lane == h
        for a in range(na):
            rows = slice(a * ATT_Q_ROWS, (a + 1) * ATT_Q_ROWS)
            kpos = n * qb + a * ATT_Q_ROWS + key_off
            valid = (kpos >= 0) & (kpos < length)
            q = q_ref[h, rows, :]
            k = window(kp_ref, km_ref, kn_ref, h, a)
            v = window(vp_ref, vm_ref, vn_ref, h, a)
            s = lax.dot_general(q, k, (((1,), (1,)), ((), ())), preferred_element_type=F32)
            s = jnp.where(valid, s + bias_ref[h], NEG_INF)
            mx = jnp.max(s, axis=-1, keepdims=True)
            p = jnp.exp(s - mx)
            den = jnp.sum(p, axis=-1, keepdims=True)
            acc = jnp.dot(p.astype(BF16), v, preferred_element_type=F32)
            o_ref[h, rows, :] = acc.astype(BF16)
            m_ref[rows, :] = jnp.where(sel, mx, m_ref[rows, :])
            l_ref[rows, :] = jnp.where(sel, den, l_ref[rows, :])
        return carry

    lax.fori_loop(0, ATT_HEADS, head, 0)


def _banded_attention(hd, bias):
    nseq, _, length, _ = hd.shape
    qb = min(512, length)
    nblk = length // qb
    hb = qb // RADIUS
    nh = length // RADIUS
    blk = (None, ATT_HEADS, qb, LANES)
    halo = (None, ATT_HEADS, RADIUS, LANES)

    def main_spec(slab):
        return pl.BlockSpec(blk, lambda s, n: (s, slab, n, 0))

    def prev_spec(slab):
        return pl.BlockSpec(halo, lambda s, n: (s, slab, jnp.maximum(n * hb - 1, 0), 0))

    def next_spec(slab):
        return pl.BlockSpec(halo, lambda s, n: (s, slab, jnp.minimum((n + 1) * hb, nh - 1), 0))

    return pl.pallas_call(
        functools.partial(_attn_kernel, qb=qb, length=length),
        grid=(nseq, nblk),
        in_specs=[main_spec(0), prev_spec(1), main_spec(1), next_spec(1),
                  prev_spec(2), main_spec(2), next_spec(2),
                  pl.BlockSpec((ATT_HEADS, ATT_Q_ROWS, ATT_KEYS), lambda s, n: (0, 0, 0))],
        out_specs=[pl.BlockSpec(blk, lambda s, n: (s, 0, n, 0)),
                   pl.BlockSpec((None, qb, LANES), lambda s, n: (s, n, 0)),
                   pl.BlockSpec((None, qb, LANES), lambda s, n: (s, n, 0))],
        out_shape=[jax.ShapeDtypeStruct((nseq, ATT_HEADS, length, LANES), BF16),
                   jax.ShapeDtypeStruct((nseq, length, LANES), F32),
                   jax.ShapeDtypeStruct((nseq, length, LANES), F32)],
        compiler_params=pltpu.CompilerParams(
            dimension_semantics=("parallel", "parallel"), vmem_limit_bytes=VMEM_LIMIT),
        name="banded_attention",
    )(hd, hd, hd, hd, hd, hd, hd, bias)


def _merge_kernel(o1_ref, o4_ref, o16_ref, m1_ref, l1_ref, m4_ref, l4_ref, m16_ref, l16_ref,
                  ga_ref, ob_ref, x_ref, w_ref, g_ref, b_ref, out_ref,
                  y_ref, st_ref, r_ref, *, tm):
    st_ref[0] = m1_ref[0]
    st_ref[1] = l1_ref[0]
    for p, (d, mr, lr) in enumerate(((4, m4_ref, l4_ref), (16, m16_ref, l16_ref)), start=1):
        for g in range(d):
            st_ref[2 * p, pl.ds(g, tm // d, stride=d), :] = mr[g]
            st_ref[2 * p + 1, pl.ds(g, tm // d, stride=d), :] = lr[g]
    ms = [st_ref[2 * p] for p in range(3)]
    mx = jnp.maximum(jnp.maximum(ms[0], ms[1]), ms[2])
    es = [jnp.exp(m - mx) for m in ms]
    den = es[0] * st_ref[1] + es[1] * st_ref[3] + es[2] * st_ref[5]
    for p in range(3):
        st_ref[2 * p] = es[p] / den

    for h in range(ATT_HEADS):
        for p, (d, o_ref) in enumerate(((4, o4_ref), (16, o16_ref))):
            for g in range(d):
                r_ref[p, pl.ds(g, tm // d, stride=d), :] = o_ref[g, h].astype(F32)
        oa = (st_ref[0, :, h:h + 1] * o1_ref[0, h].astype(F32)
              + st_ref[2, :, h:h + 1] * r_ref[0]
              + st_ref[4, :, h:h + 1] * r_ref[1])
        cols = slice(h * HEAD_DIM, (h + 1) * HEAD_DIM)
        y_ref[:, cols] = (oa * ga_ref[:, cols].astype(F32)).astype(BF16)
    y_ref[:, ATT_WIDTH:] = ob_ref[...]

    f = jnp.dot(y_ref[...], w_ref[...], preferred_element_type=F32)
    out_ref[...] = _layernorm(DN_ALPHA * x_ref[...] + f, g_ref[...], b_ref[...])


def _merge_outproj(o_pats, stats, ga, ob, x, w_out, ln_g, ln_b, *, tm=256):
    b, s, dm = x.shape
    nt = s // tm
    ga = ga.reshape(b, s, ATT_WIDTH)
    ob = ob.reshape(b, s, POOL_WIDTH)
    o_specs = [pl.BlockSpec((None, d, ATT_HEADS, tm // d, LANES), lambda bi, i: (bi, 0, 0, i, 0)) for d in DILATIONS]
    st_specs = []
    for d in DILATIONS:
        st_specs += [pl.BlockSpec((None, d, tm // d, LANES), lambda bi, i: (bi, 0, i, 0))] * 2
    const2 = lambda bi, i: (0, 0)
    args = list(o_pats)
    for mm, ll in stats:
        args += [mm, ll]
    return pl.pallas_call(
        functools.partial(_merge_kernel, tm=tm),
        grid=(b, nt),
        in_specs=o_specs + st_specs + [
            pl.BlockSpec((None, tm, ATT_WIDTH), lambda bi, i: (bi, i, 0)),
            pl.BlockSpec((None, tm, POOL_WIDTH), lambda bi, i: (bi, i, 0)),
            pl.BlockSpec((None, tm, dm), lambda bi, i: (bi, i, 0)),
            pl.BlockSpec(w_out.shape, const2, pipeline_mode=pl.Buffered(1)),
            pl.BlockSpec((1, dm), const2),
            pl.BlockSpec((1, dm), const2)],
        out_specs=pl.BlockSpec((None, tm, dm), lambda bi, i: (bi, i, 0)),
        out_shape=jax.ShapeDtypeStruct((b, s, dm), F32),
        scratch_shapes=[pltpu.VMEM((tm, ATT_WIDTH + POOL_WIDTH), BF16),
                        pltpu.VMEM((6, tm, LANES), F32),
                        pltpu.VMEM((2, tm, LANES), F32)],
        compiler_params=pltpu.CompilerParams(
            dimension_semantics=("parallel", "parallel"), vmem_limit_bytes=VMEM_LIMIT),
        name="merge_outproj",
    )(*args, ga, ob, x, w_out, ln_g, ln_b)


def _conv_in_kernel(xp_ref, x_ref, xn_ref, wb_ref, wc_ref, wv_ref, wg_ref, cw_ref, o_ref, xb_ref, cv_ref,
                    *, tm, seq):
    i = pl.program_id(0)
    tiles_per_seq = seq // tm

    @pl.when(pl.program_id(1) == 0)
    def _():
        xb_ref[0:HALO] = xp_ref[...].astype(BF16)
        xb_ref[HALO:HALO + tm] = x_ref[...].astype(BF16)
        xb_ref[HALO + tm:] = xn_ref[...].astype(BF16)

    xa = xb_ref[...]
    cv_ref[...] = (jnp.dot(xa, wc_ref[...], preferred_element_type=F32)
                   * jnp.dot(xa, wv_ref[...], preferred_element_type=F32))

    @pl.when((i % tiles_per_seq) == 0)
    def _():
        cv_ref[0:HALO] = jnp.zeros((HALO, cv_ref.shape[1]), F32)

    @pl.when((i % tiles_per_seq) == tiles_per_seq - 1)
    def _():
        cv_ref[HALO + tm:] = jnp.zeros((HALO, cv_ref.shape[1]), F32)

    xm = xb_ref[HALO:HALO + tm]
    gb = jnp.dot(xm, wb_ref[...], preferred_element_type=F32)
    gate = jnp.dot(xm, wg_ref[...], preferred_element_type=F32)
    conv = (cv_ref[pl.ds(HALO - 1, tm), :] * cw_ref[0:1, :]
            + cv_ref[pl.ds(HALO, tm), :] * cw_ref[1:2, :]
            + cv_ref[pl.ds(HALO + 1, tm), :] * cw_ref[2:3, :])
    o_ref[...] = (gb * conv * _silu(gate)).astype(BF16)


def _conv_inproj(x2, w_in, conv_w, *, seq, tm=1024, tc=512):
    m, dm = x2.shape
    hb = tm // HALO
    nhb = m // HALO
    nc = CONV_WIDTH // tc

    def wspec(part):
        return pl.BlockSpec((dm, tc), lambda i, c: (0, part * nc + c))

    return pl.pallas_call(
        functools.partial(_conv_in_kernel, tm=tm, seq=seq),
        grid=(m // tm, nc),
        in_specs=[pl.BlockSpec((HALO, dm), lambda i, c: (jnp.maximum(i * hb - 1, 0), 0)),
                  pl.BlockSpec((tm, dm), lambda i, c: (i, 0)),
                  pl.BlockSpec((HALO, dm), lambda i, c: (jnp.minimum((i + 1) * hb, nhb - 1), 0)),
                  wspec(0), wspec(1), wspec(2), wspec(3),
                  pl.BlockSpec((3, tc), lambda i, c: (0, c))],
        out_specs=pl.BlockSpec((tm, tc), lambda i, c: (i, c)),
        out_shape=jax.ShapeDtypeStruct((m, CONV_WIDTH), BF16),
        scratch_shapes=[pltpu.VMEM((tm + 2 * HALO, dm), BF16), pltpu.VMEM((tm + 2 * HALO, tc), F32)],
        compiler_params=pltpu.CompilerParams(
            dimension_semantics=("parallel", "arbitrary"), vmem_limit_bytes=VMEM_LIMIT),
        name="conv_inproj",
    )(x2, x2, x2, w_in, w_in, w_in, w_in, conv_w)


def _outproj_ln_kernel(y_ref, x_ref, w_ref, g_ref, b_ref, o_ref):
    f = jnp.dot(y_ref[...], w_ref[...], preferred_element_type=F32)
    o_ref[...] = _layernorm(DN_ALPHA * x_ref[...] + f, g_ref[...], b_ref[...])


def _outproj_ln(y, x2, w_out, ln_g, ln_b, *, tm=512):
    m, dm = x2.shape
    const2 = lambda i: (0, 0)
    return pl.pallas_call(
        _outproj_ln_kernel,
        grid=(m // tm,),
        in_specs=[pl.BlockSpec((tm, y.shape[1]), lambda i: (i, 0)),
                  pl.BlockSpec((tm, dm), lambda i: (i, 0)),
                  pl.BlockSpec(w_out.shape, const2, pipeline_mode=pl.Buffered(1)),
                  pl.BlockSpec((1, dm), const2),
                  pl.BlockSpec((1, dm), const2)],
        out_specs=pl.BlockSpec((tm, dm), lambda i: (i, 0)),
        out_shape=jax.ShapeDtypeStruct((m, dm), F32),
        compiler_params=pltpu.CompilerParams(dimension_semantics=("parallel",), vmem_limit_bytes=VMEM_LIMIT),
        name="outproj_ln",
    )(y, x2, w_out, ln_g, ln_b)


def _trunk(x, biases, w_qkv, w_ga, w_u, w_gb, pool_w, pool_scale, w_out_ab, w_in_c, conv_w, w_out_c, ln_g, ln_b):
    b, s, dm = x.shape
    x2 = x.reshape(b * s, dm)
    hds = _qkv_proj(x, w_qkv)
    ga = _gate_proj(x2, w_ga)
    ob = _pool_branch(x2, w_u, w_gb, pool_w, pool_scale, seq=s)
    o_pats, stats = [], []
    for d, hd, bias in zip(DILATIONS, hds, biases):
        ld = s // d
        o, mm, ll = _banded_attention(hd.reshape(b * d, 3 * ATT_HEADS, ld, LANES), bias)
        o_pats.append(o.reshape(b, d, ATT_HEADS, ld, LANES))
        stats.append((mm.reshape(b, d, ld, LANES), ll.reshape(b, d, ld, LANES)))
    x1 = _merge_outproj(o_pats, stats, ga, ob, x, w_out_ab, ln_g[0:1], ln_b[0:1])
    x1 = x1.reshape(b * s, dm)
    y = _conv_inproj(x1, w_in_c, conv_w, seq=s)
    out = _outproj_ln(y, x1, w_out_c, ln_g[1:2], ln_b[1:2])
    return out.reshape(b, s, dm)


def kernel(x_prompt, x_sample, rel_bias, w_in_ab, pool_w, pool_scale, w_out_ab, w_in_c, conv_w, w_out_c, ln_g, ln_b):
    assert DEPTH == 2 and w_in_ab.shape[0] == 1 and w_in_c.shape[0] == 1
    w_ab = w_in_ab[0].astype(BF16)
    a = ATT_WIDTH
    params = dict(
        biases=[_band_bias(rel_bias, d) for d in DILATIONS],
        w_qkv=w_ab[:, :3 * a],
        w_ga=w_ab[:, 3 * a:4 * a],
        w_u=w_ab[:, 4 * a:4 * a + POOL_WIDTH],
        w_gb=w_ab[:, 4 * a + POOL_WIDTH:],
        pool_w=pool_w[0].astype(BF16),
        pool_scale=pool_scale[0].reshape(1, POOL_WIDTH),
        w_out_ab=w_out_ab[0].astype(BF16),
        w_in_c=w_in_c[0].astype(BF16),
        conv_w=conv_w[0],
        w_out_c=w_out_c[0].astype(BF16),
        ln_g=ln_g,
        ln_b=ln_b,
    )
    return (_trunk(x_prompt, **params), _trunk(x_sample, **params))
```

```python
import functools
import math

import numpy as np
import jax
import jax.numpy as jnp
from jax import lax
from jax.experimental import pallas as pl
from jax.experimental.pallas import tpu as pltpu

D_MODEL = 2048
DEPTH = 2
ATT_HEADS = 16
HEAD_DIM = 128
ATT_WIDTH = ATT_HEADS * HEAD_DIM
DILATED_PATTERNS = ((128, 1), (512, 4), (2048, 16))
DILATIONS = tuple(d for _, d in DILATED_PATTERNS)
RADIUS = 64
POOL_WINDOWS = (2, 4, 8, 16)
POOL_WIDTH = D_MODEL // 2
POOL_GROUP = POOL_WIDTH // len(POOL_WINDOWS)
CONV_WIDTH = D_MODEL
REL_BUCKETS = 32
REL_MAX_DISTANCE = 1024
DN_ALPHA = (2 * DEPTH) ** 0.25
LN_EPS = 1e-5
NEG_INF = -1e30
LOG2E = math.log2(math.e)

LANES = 128
HALO = 16
ATT_Q_ROWS = 128
ATT_KEYS = ATT_Q_ROWS + 2 * RADIUS
ATT_UNITS = 4
VMEM_LIMIT = 56 * 1024 * 1024

BF16 = jnp.bfloat16
F32 = jnp.float32

assert all(w // (2 * d) == RADIUS for w, d in DILATED_PATTERNS)


def _t5_bucket(rel):
    nb = REL_BUCKETS // 2
    max_exact = nb // 2
    ret = np.where(rel > 0, nb, 0)
    n = np.abs(rel)
    n_safe = np.maximum(n, 1).astype(np.float64)
    large = max_exact + (np.log(n_safe / max_exact) / math.log(REL_MAX_DISTANCE / max_exact)
                         * (nb - max_exact)).astype(np.int64)
    large = np.minimum(large, nb - 1)
    return (ret + np.where(n < max_exact, n, large)).astype(np.int32)


def _band_bias(rel_bias, dilation):
    rel = np.arange(ATT_KEYS)[None, :] - RADIUS - np.arange(ATT_Q_ROWS)[:, None]
    bucket = jnp.asarray(_t5_bucket(rel * dilation).reshape(1, -1))
    onehot = (bucket == jnp.arange(REL_BUCKETS, dtype=jnp.int32)[:, None]).astype(F32)
    bias = jnp.dot(rel_bias.astype(F32).T, onehot, precision=lax.Precision.HIGHEST)
    bias = bias.reshape(1, ATT_HEADS, ATT_Q_ROWS, ATT_KEYS)
    key = np.arange(ATT_KEYS)[None, :]
    in_band = np.abs(rel) <= RADIUS
    after_start = key >= RADIUS
    before_end = key < ATT_Q_ROWS + RADIUS
    keep = np.stack([in_band, in_band & after_start, in_band & before_end, in_band & after_start & before_end])
    return jnp.where(jnp.asarray(keep[:, None]), bias * LOG2E, NEG_INF)


def _silu(x):
    return x / (1.0 + jnp.exp(-x))


def _layernorm(z, g, b):
    mu = jnp.mean(z, axis=-1, keepdims=True)
    zc = z - mu
    var = jnp.mean(zc * zc, axis=-1, keepdims=True)
    return zc * lax.rsqrt(var + LN_EPS) * g + b


def _qkv_kernel(x_ref, w_ref, o1_ref, o4_ref, o16_ref, xb_ref, slab_ref, *, tm, tn):
    j = pl.program_id(2)

    @pl.when(j == 0)
    def _():
        xb_ref[...] = x_ref[...].astype(BF16)

    acc = jnp.dot(xb_ref[...], w_ref[...], preferred_element_type=F32)
    acc = acc * jnp.where(j < ATT_WIDTH // tn, LOG2E * HEAD_DIM ** -0.5, 1.0)
    ns = tn // LANES
    for c in range(ns):
        blk = acc[:, c * LANES:(c + 1) * LANES]
        slab_ref[c] = blk
        o1_ref[0, c] = blk.astype(BF16)
    for d, o_ref in ((4, o4_ref), (16, o16_ref)):
        for g in range(d):
            for c in range(ns):
                o_ref[g, c] = slab_ref[c, pl.ds(g, tm // d, stride=d), :].astype(BF16)


def _qkv_proj(x, w, *, tm=1024, tn=512):
    b, s, dm = x.shape
    n = w.shape[1]
    nslab = n // LANES
    outs = [jax.ShapeDtypeStruct((b, d, nslab, s // d, LANES), BF16) for d in DILATIONS]
    out_specs = [pl.BlockSpec((None, d, tn // LANES, tm // d, LANES), lambda bi, i, j: (bi, 0, j, i, 0))
                 for d in DILATIONS]
    return pl.pallas_call(
        functools.partial(_qkv_kernel, tm=tm, tn=tn),
        grid=(b, s // tm, n // tn),
        in_specs=[pl.BlockSpec((None, tm, dm), lambda bi, i, j: (bi, i, 0)),
                  pl.BlockSpec((dm, tn), lambda bi, i, j: (0, j))],
        out_specs=out_specs,
        out_shape=outs,
        scratch_shapes=[pltpu.VMEM((tm, dm), BF16), pltpu.VMEM((tn // LANES, tm, LANES), F32)],
        compiler_params=pltpu.CompilerParams(
            dimension_semantics=("parallel", "parallel", "arbitrary"), vmem_limit_bytes=VMEM_LIMIT),
        name="qkv_proj",
    )(x, w)


def _gate_kernel(x_ref, w_ref, o_ref, xb_ref):
    @pl.when(pl.program_id(1) == 0)
    def _():
        xb_ref[...] = x_ref[...].astype(BF16)

    acc = jnp.dot(xb_ref[...], w_ref[...], preferred_element_type=F32)
    o_ref[...] = _silu(acc).astype(BF16)


def _gate_proj(x2, w, *, tm=1024, tn=512):
    m, dm = x2.shape
    n = w.shape[1]
    return pl.pallas_call(
        _gate_kernel,
        grid=(m // tm, n // tn),
        in_specs=[pl.BlockSpec((tm, dm), lambda i, j: (i, 0)),
                  pl.BlockSpec((dm, tn), lambda i, j: (0, j))],
        out_specs=pl.BlockSpec((tm, tn), lambda i, j: (i, j)),
        out_shape=jax.ShapeDtypeStruct((m, n), BF16),
        scratch_shapes=[pltpu.VMEM((tm, dm), BF16)],
        compiler_params=pltpu.CompilerParams(
            dimension_semantics=("parallel", "arbitrary"), vmem_limit_bytes=VMEM_LIMIT),
        name="gate_proj",
    )(x2, w)


def _pool_kernel(xp_ref, x_ref, xn_ref, wu_ref, wg_ref, pw_ref, ps_ref, o_ref, xb_ref, u_ref, *, tm, seq):
    i = pl.program_id(0)
    tiles_per_seq = seq // tm
    first = (i % tiles_per_seq) == 0
    last = (i % tiles_per_seq) == tiles_per_seq - 1

    xb_ref[0:HALO] = xp_ref[...].astype(BF16)
    xb_ref[HALO:HALO + tm] = x_ref[...].astype(BF16)
    xb_ref[HALO + tm:] = xn_ref[...].astype(BF16)
    u_ref[...] = jnp.dot(xb_ref[...], wu_ref[...], preferred_element_type=F32)

    @pl.when(first)
    def _():
        u_ref[0:HALO] = jnp.zeros((HALO, POOL_WIDTH), F32)

    @pl.when(last)
    def _():
        u_ref[HALO + tm:] = jnp.zeros((HALO, POOL_WIDTH), F32)

    gate = jnp.dot(xb_ref[HALO:HALO + tm], wg_ref[...], preferred_element_type=F32)
    pos = (i % tiles_per_seq) * tm + lax.broadcasted_iota(jnp.int32, (tm, POOL_GROUP), 0)
    for gi, w in enumerate(POOL_WINDOWS):
        cols = slice(gi * POOL_GROUP, (gi + 1) * POOL_GROUP)
        tot = u_ref[pl.ds(HALO - w // 2, tm), cols]
        for off in range(-w // 2 + 1, w // 2):
            tot = tot + u_ref[pl.ds(HALO + off, tm), cols]
        lo = jnp.maximum(pos - w // 2, 0)
        hi = jnp.minimum(pos + w // 2 - 1, seq - 1)
        cnt = (hi - lo + 1).astype(F32)
        pooled = tot / cnt - u_ref[pl.ds(HALO, tm), cols]
        ob = jnp.dot(pooled.astype(BF16), pw_ref[gi], preferred_element_type=F32)
        o_ref[:, cols] = (ob * ps_ref[:, cols] * _silu(gate[:, cols])).astype(BF16)


def _pool_branch(x2, wu, wg, pool_w, pool_scale, *, seq, tm=512):
    m, dm = x2.shape
    hb = tm // HALO
    nhb = m // HALO
    return pl.pallas_call(
        functools.partial(_pool_kernel, tm=tm, seq=seq),
        grid=(m // tm,),
        in_specs=[pl.BlockSpec((HALO, dm), lambda i: (jnp.maximum(i * hb - 1, 0), 0)),
                  pl.BlockSpec((tm, dm), lambda i: (i, 0)),
                  pl.BlockSpec((HALO, dm), lambda i: (jnp.minimum((i + 1) * hb, nhb - 1), 0)),
                  pl.BlockSpec((dm, POOL_WIDTH), lambda i: (0, 0)),
                  pl.BlockSpec((dm, POOL_WIDTH), lambda i: (0, 0)),
                  pl.BlockSpec((len(POOL_WINDOWS), POOL_GROUP, POOL_GROUP), lambda i: (0, 0, 0)),
                  pl.BlockSpec((1, POOL_WIDTH), lambda i: (0, 0))],
        out_specs=pl.BlockSpec((tm, POOL_WIDTH), lambda i: (i, 0)),
        out_shape=jax.ShapeDtypeStruct((m, POOL_WIDTH), BF16),
        scratch_shapes=[pltpu.VMEM((tm + 2 * HALO, dm), BF16), pltpu.VMEM((tm + 2 * HALO, POOL_WIDTH), F32)],
        compiler_params=pltpu.CompilerParams(dimension_semantics=("parallel",), vmem_limit_bytes=VMEM_LIMIT),
        name="pool_branch",
    )(x2, x2, x2, wu, wg, pool_w, pool_scale)


def _attn_kernel(q_ref, kp_ref, km_ref, kn_ref, vp_ref, vm_ref, vn_ref, bias_ref,
                 o_ref, m_ref, l_ref, s0_ref, s1_ref, *, qb, nblk):
    n = pl.program_id(1)
    na = qb // ATT_Q_ROWS
    hpi = max(1, ATT_UNITS // na)
    steps = ATT_HEADS // hpi
    m_ref[...] = jnp.zeros_like(m_ref)
    l_ref[...] = jnp.ones_like(l_ref)
    lane = lax.broadcasted_iota(jnp.int32, (ATT_Q_ROWS, LANES), 1)
    first = (n == 0).astype(jnp.int32)
    last = (n == nblk - 1).astype(jnp.int32)
    s_refs = (s0_ref, s1_ref)
    ones = jnp.ones((ATT_KEYS, LANES), BF16)

    def window(prev_ref, main_ref, next_ref, h, a):
        lo = a * ATT_Q_ROWS - RADIUS
        hi = lo + ATT_KEYS
        parts = []
        if lo < 0:
            parts.append(prev_ref[h])
        parts.append(main_ref[h, max(lo, 0):min(hi, qb), :])
        if hi > qb:
            parts.append(next_ref[h])
        return parts[0] if len(parts) == 1 else jnp.concatenate(parts, axis=0)

    def scores(step, slot):
        for j in range(hpi):
            h = step * hpi + j
            for a in range(na):
                q = q_ref[h, a * ATT_Q_ROWS:(a + 1) * ATT_Q_ROWS, :]
                k = window(kp_ref, km_ref, kn_ref, h, a)
                s_refs[slot][j * na + a] = lax.dot_general(
                    q, k, (((1,), (1,)), ((), ())), preferred_element_type=F32)

    def softmax_pv(step, slot):
        for j in range(hpi):
            h = step * hpi + j
            sel = lane == h
            for a in range(na):
                rows = slice(a * ATT_Q_ROWS, (a + 1) * ATT_Q_ROWS)
                variant = (first if a == 0 else 0) + (2 * last if a == na - 1 else 0)
                s = s_refs[slot][j * na + a] + bias_ref[variant, h]
                mx = jnp.max(s, axis=-1, keepdims=True)
                p = jnp.exp2(s - mx)
                v = window(vp_ref, vm_ref, vn_ref, h, a)
                pv = jnp.dot(p.astype(BF16), jnp.concatenate([v, ones], axis=1), preferred_element_type=F32)
                den = pv[:, HEAD_DIM:]
                o_ref[h, rows, :] = pv[:, :HEAD_DIM].astype(BF16)
                m_ref[rows, :] = jnp.where(sel, mx, m_ref[rows, :])
                l_ref[rows, :] = jnp.where(sel, den, l_ref[rows, :])

    scores(0, 0)

    def body(tt, carry):
        scores(2 * tt + 1, 1)
        softmax_pv(2 * tt, 0)
        scores(2 * tt + 2, 0)
        softmax_pv(2 * tt + 1, 1)
        return carry

    lax.fori_loop(0, steps // 2 - 1, body, 0)
    scores(steps - 1, 1)
    softmax_pv(steps - 2, 0)
    softmax_pv(steps - 1, 1)


def _banded_attention(hd, bias):
    nseq, _, length, _ = hd.shape
    qb = min(512, length)
    nblk = length // qb
    hb = qb // RADIUS
    nh = length // RADIUS
    blk = (None, ATT_HEADS, qb, LANES)
    halo = (None, ATT_HEADS, RADIUS, LANES)
    units = max(ATT_UNITS, qb // ATT_Q_ROWS)

    def main_spec(slab):
        return pl.BlockSpec(blk, lambda s, n: (s, slab, n, 0))

    def prev_spec(slab):
        return pl.BlockSpec(halo, lambda s, n: (s, slab, jnp.maximum(n * hb - 1, 0), 0))

    def next_spec(slab):
        return pl.BlockSpec(halo, lambda s, n: (s, slab, jnp.minimum((n + 1) * hb, nh - 1), 0))

    return pl.pallas_call(
        functools.partial(_attn_kernel, qb=qb, nblk=nblk),
        grid=(nseq, nblk),
        in_specs=[main_spec(0), prev_spec(1), main_spec(1), next_spec(1),
                  prev_spec(2), main_spec(2), next_spec(2),
                  pl.BlockSpec(bias.shape, lambda s, n: (0, 0, 0, 0), pipeline_mode=pl.Buffered(1))],
        out_specs=[pl.BlockSpec(blk, lambda s, n: (s, 0, n, 0)),
                   pl.BlockSpec((None, qb, LANES), lambda s, n: (s, n, 0)),
                   pl.BlockSpec((None, qb, LANES), lambda s, n: (s, n, 0))],
        out_shape=[jax.ShapeDtypeStruct((nseq, ATT_HEADS, length, LANES), BF16),
                   jax.ShapeDtypeStruct((nseq, length, LANES), F32),
                   jax.ShapeDtypeStruct((nseq, length, LANES), F32)],
        scratch_shapes=[pltpu.VMEM((units, ATT_Q_ROWS, ATT_KEYS), F32)] * 2,
        compiler_params=pltpu.CompilerParams(
            dimension_semantics=("parallel", "parallel"), vmem_limit_bytes=VMEM_LIMIT),
        name="banded_attention",
    )(hd, hd, hd, hd, hd, hd, hd, bias)


def _merge_kernel(o1_ref, o4_ref, o16_ref, m1_ref, l1_ref, m4_ref, l4_ref, m16_ref, l16_ref,
                  ga_ref, ob_ref, x_ref, w_ref, g_ref, b_ref, out_ref,
                  y_ref, st_ref, r_ref, *, tm):
    st_ref[0] = m1_ref[0]
    st_ref[1] = l1_ref[0]
    for p, (d, mr, lr) in enumerate(((4, m4_ref, l4_ref), (16, m16_ref, l16_ref)), start=1):
        for g in range(d):
            st_ref[2 * p, pl.ds(g, tm // d, stride=d), :] = mr[g]
            st_ref[2 * p + 1, pl.ds(g, tm // d, stride=d), :] = lr[g]
    ms = [st_ref[2 * p] for p in range(3)]
    mx = jnp.maximum(jnp.maximum(ms[0], ms[1]), ms[2])
    es = [jnp.exp2(m - mx) for m in ms]
    den = es[0] * st_ref[1] + es[1] * st_ref[3] + es[2] * st_ref[5]
    for p in range(3):
        st_ref[2 * p] = es[p] / den

    for h in range(ATT_HEADS):
        for p, (d, o_ref) in enumerate(((4, o4_ref), (16, o16_ref))):
            for g in range(d):
                r_ref[p, pl.ds(g, tm // d, stride=d), :] = o_ref[g, h].astype(F32)
        oa = (st_ref[0, :, h:h + 1] * o1_ref[0, h].astype(F32)
              + st_ref[2, :, h:h + 1] * r_ref[0]
              + st_ref[4, :, h:h + 1] * r_ref[1])
        cols = slice(h * HEAD_DIM, (h + 1) * HEAD_DIM)
        y_ref[:, cols] = (oa * ga_ref[:, cols].astype(F32)).astype(BF16)
    y_ref[:, ATT_WIDTH:] = ob_ref[...]

    f = jnp.dot(y_ref[...], w_ref[...], preferred_element_type=F32)
    out_ref[...] = _layernorm(DN_ALPHA * x_ref[...] + f, g_ref[...], b_ref[...])


def _merge_outproj(o_pats, stats, ga, ob, x, w_out, ln_g, ln_b, *, tm=256):
    b, s, dm = x.shape
    nt = s // tm
    ga = ga.reshape(b, s, ATT_WIDTH)
    ob = ob.reshape(b, s, POOL_WIDTH)
    o_specs = [pl.BlockSpec((None, d, ATT_HEADS, tm // d, LANES), lambda bi, i: (bi, 0, 0, i, 0)) for d in DILATIONS]
    st_specs = []
    for d in DILATIONS:
        st_specs += [pl.BlockSpec((None, d, tm // d, LANES), lambda bi, i: (bi, 0, i, 0))] * 2
    const2 = lambda bi, i: (0, 0)
    args = list(o_pats)
    for mm, ll in stats:
        args += [mm, ll]
    return pl.pallas_call(
        functools.partial(_merge_kernel, tm=tm),
        grid=(b, nt),
        in_specs=o_specs + st_specs + [
            pl.BlockSpec((None, tm, ATT_WIDTH), lambda bi, i: (bi, i, 0)),
            pl.BlockSpec((None, tm, POOL_WIDTH), lambda bi, i: (bi, i, 0)),
            pl.BlockSpec((None, tm, dm), lambda bi, i: (bi, i, 0)),
            pl.BlockSpec(w_out.shape, const2, pipeline_mode=pl.Buffered(1)),
            pl.BlockSpec((1, dm), const2),
            pl.BlockSpec((1, dm), const2)],
        out_specs=pl.BlockSpec((None, tm, dm), lambda bi, i: (bi, i, 0)),
        out_shape=jax.ShapeDtypeStruct((b, s, dm), F32),
        scratch_shapes=[pltpu.VMEM((tm, ATT_WIDTH + POOL_WIDTH), BF16),
                        pltpu.VMEM((6, tm, LANES), F32),
                        pltpu.VMEM((2, tm, LANES), F32)],
        compiler_params=pltpu.CompilerParams(
            dimension_semantics=("parallel", "parallel"), vmem_limit_bytes=VMEM_LIMIT),
        name="merge_outproj",
    )(*args, ga, ob, x, w_out, ln_g, ln_b)


def _conv_in_kernel(xp_ref, x_ref, xn_ref, wb_ref, wc_ref, wv_ref, wg_ref, cw_ref, o_ref, xb_ref, cv_ref,
                    *, tm, seq):
    i = pl.program_id(0)
    tiles_per_seq = seq // tm

    @pl.when(pl.program_id(1) == 0)
    def _():
        xb_ref[0:HALO] = xp_ref[...].astype(BF16)
        xb_ref[HALO:HALO + tm] = x_ref[...].astype(BF16)
        xb_ref[HALO + tm:] = xn_ref[...].astype(BF16)

    xa = xb_ref[...]
    cv_ref[...] = (jnp.dot(xa, wc_ref[...], preferred_element_type=F32)
                   * jnp.dot(xa, wv_ref[...], preferred_element_type=F32))

    @pl.when((i % tiles_per_seq) == 0)
    def _():
        cv_ref[0:HALO] = jnp.zeros((HALO, cv_ref.shape[1]), F32)

    @pl.when((i % tiles_per_seq) == tiles_per_seq - 1)
    def _():
        cv_ref[HALO + tm:] = jnp.zeros((HALO, cv_ref.shape[1]), F32)

    xm = xb_ref[HALO:HALO + tm]
    gb = jnp.dot(xm, wb_ref[...], preferred_element_type=F32)
    gate = jnp.dot(xm, wg_ref[...], preferred_element_type=F32)
    conv = (cv_ref[pl.ds(HALO - 1, tm), :] * cw_ref[0:1, :]
            + cv_ref[pl.ds(HALO, tm), :] * cw_ref[1:2, :]
            + cv_ref[pl.ds(HALO + 1, tm), :] * cw_ref[2:3, :])
    o_ref[...] = (gb * conv * _silu(gate)).astype(BF16)


def _conv_inproj(x2, w_in, conv_w, *, seq, tm=1024, tc=512):
    m, dm = x2.shape
    hb = tm // HALO
    nhb = m // HALO
    nc = CONV_WIDTH // tc

    def wspec(part):
        return pl.BlockSpec((dm, tc), lambda i, c: (0, part * nc + c))

    return pl.pallas_call(
        functools.partial(_conv_in_kernel, tm=tm, seq=seq),
        grid=(m // tm, nc),
        in_specs=[pl.BlockSpec((HALO, dm), lambda i, c: (jnp.maximum(i * hb - 1, 0), 0)),
                  pl.BlockSpec((tm, dm), lambda i, c: (i, 0)),
                  pl.BlockSpec((HALO, dm), lambda i, c: (jnp.minimum((i + 1) * hb, nhb - 1), 0)),
                  wspec(0), wspec(1), wspec(2), wspec(3),
                  pl.BlockSpec((3, tc), lambda i, c: (0, c))],
        out_specs=pl.BlockSpec((tm, tc), lambda i, c: (i, c)),
        out_shape=jax.ShapeDtypeStruct((m, CONV_WIDTH), BF16),
        scratch_shapes=[pltpu.VMEM((tm + 2 * HALO, dm), BF16), pltpu.VMEM((tm + 2 * HALO, tc), F32)],
        compiler_params=pltpu.CompilerParams(
            dimension_semantics=("parallel", "arbitrary"), vmem_limit_bytes=VMEM_LIMIT),
        name="conv_inproj",
    )(x2, x2, x2, w_in, w_in, w_in, w_in, conv_w)


def _outproj_ln_kernel(y_ref, x_ref, w_ref, g_ref, b_ref, o_ref):
    f = jnp.dot(y_ref[...], w_ref[...], preferred_element_type=F32)
    o_ref[...] = _layernorm(DN_ALPHA * x_ref[...] + f, g_ref[...], b_ref[...])


def _outproj_ln(y, x2, w_out, ln_g, ln_b, *, tm=512):
    m, dm = x2.shape
    const2 = lambda i: (0, 0)
    return pl.pallas_call(
        _outproj_ln_kernel,
        grid=(m // tm,),
        in_specs=[pl.BlockSpec((tm, y.shape[1]), lambda i: (i, 0)),
                  pl.BlockSpec((tm, dm), lambda i: (i, 0)),
                  pl.BlockSpec(w_out.shape, const2, pipeline_mode=pl.Buffered(1)),
                  pl.BlockSpec((1, dm), const2),
                  pl.BlockSpec((1, dm), const2)],
        out_specs=pl.BlockSpec((tm, dm), lambda i: (i, 0)),
        out_shape=jax.ShapeDtypeStruct((m, dm), F32),
        compiler_params=pltpu.CompilerParams(dimension_semantics=("parallel",), vmem_limit_bytes=VMEM_LIMIT),
        name="outproj_ln",
    )(y, x2, w_out, ln_g, ln_b)


def _trunk(x, biases, w_qkv, w_ga, w_u, w_gb, pool_w, pool_scale, w_out_ab, w_in_c, conv_w, w_out_c, ln_g, ln_b):
    b, s, dm = x.shape
    x2 = x.reshape(b * s, dm)
    hds = _qkv_proj(x, w_qkv)
    ga = _gate_proj(x2, w_ga)
    ob = _pool_branch(x2, w_u, w_gb, pool_w, pool_scale, seq=s)
    o_pats, stats = [], []
    for d, hd, bias in zip(DILATIONS, hds, biases):
        ld = s // d
        o, mm, ll = _banded_attention(hd.reshape(b * d, 3 * ATT_HEADS, ld, LANES), bias)
        o_pats.append(o.reshape(b, d, ATT_HEADS, ld, LANES))
        stats.append((mm.reshape(b, d, ld, LANES), ll.reshape(b, d, ld, LANES)))
    x1 = _merge_outproj(o_pats, stats, ga, ob, x, w_out_ab, ln_g[0:1], ln_b[0:1])
    x1 = x1.reshape(b * s, dm)
    y = _conv_inproj(x1, w_in_c, conv_w, seq=s)
    out = _outproj_ln(y, x1, w_out_c, ln_g[1:2], ln_b[1:2])
    return out.reshape(b, s, dm)


def kernel(x_prompt, x_sample, rel_bias, w_in_ab, pool_w, pool_scale, w_out_ab, w_in_c, conv_w, w_out_c, ln_g, ln_b):
    assert DEPTH == 2 and w_in_ab.shape[0] == 1 and w_in_c.shape[0] == 1
    w_ab = w_in_ab[0].astype(BF16)
    a = ATT_WIDTH
    params = dict(
        biases=[_band_bias(rel_bias, d) for d in DILATIONS],
        w_qkv=w_ab[:, :3 * a],
        w_ga=w_ab[:, 3 * a:4 * a],
        w_u=w_ab[:, 4 * a:4 * a + POOL_WIDTH],
        w_gb=w_ab[:, 4 * a + POOL_WIDTH:],
        pool_w=pool_w[0].astype(BF16),
        pool_scale=pool_scale[0].reshape(1, POOL_WIDTH),
        w_out_ab=w_out_ab[0].astype(BF16),
        w_in_c=w_in_c[0].astype(BF16),
        conv_w=conv_w[0],
        w_out_c=w_out_c[0].astype(BF16),
        ln_g=ln_g,
        ln_b=ln_b,
    )
    return (_trunk(x_prompt, **params), _trunk(x_sample, **params))
```

```python
import functools
import math

import numpy as np
import jax
import jax.numpy as jnp
from jax import lax
from jax.experimental import pallas as pl
from jax.experimental.pallas import tpu as pltpu

D_MODEL = 2048
DEPTH = 2
ATT_HEADS = 16
HEAD_DIM = 128
ATT_WIDTH = ATT_HEADS * HEAD_DIM
DILATED_PATTERNS = ((128, 1), (512, 4), (2048, 16))
DILATIONS = tuple(d for _, d in DILATED_PATTERNS)
RADIUS = 64
POOL_WINDOWS = (2, 4, 8, 16)
POOL_WIDTH = D_MODEL // 2
POOL_GROUP = POOL_WIDTH // len(POOL_WINDOWS)
CONV_WIDTH = D_MODEL
REL_BUCKETS = 32
REL_MAX_DISTANCE = 1024
DN_ALPHA = (2 * DEPTH) ** 0.25
LN_EPS = 1e-5
NEG_INF = -1e30
LOG2E = math.log2(math.e)

LANES = 128
HALO = 16
ATT_Q_ROWS = 128
ATT_KEYS = ATT_Q_ROWS + 2 * RADIUS
ATT_UNITS = 4
ATT_TRIP_STEPS = 4
PROJ_ROWS = 256
ROW_CHUNK = 128
VMEM_LIMIT = 56 * 1024 * 1024

BF16 = jnp.bfloat16
F32 = jnp.float32

assert all(w // (2 * d) == RADIUS for w, d in DILATED_PATTERNS)


def _t5_bucket(rel):
    nb = REL_BUCKETS // 2
    max_exact = nb // 2
    ret = np.where(rel > 0, nb, 0)
    n = np.abs(rel)
    n_safe = np.maximum(n, 1).astype(np.float64)
    large = max_exact + (np.log(n_safe / max_exact) / math.log(REL_MAX_DISTANCE / max_exact)
                         * (nb - max_exact)).astype(np.int64)
    large = np.minimum(large, nb - 1)
    return (ret + np.where(n < max_exact, n, large)).astype(np.int32)


def _band_bias(rel_bias, dilation):
    rel = np.arange(ATT_KEYS)[None, :] - RADIUS - np.arange(ATT_Q_ROWS)[:, None]
    bucket = jnp.asarray(_t5_bucket(rel * dilation).reshape(1, -1))
    onehot = (bucket == jnp.arange(REL_BUCKETS, dtype=jnp.int32)[:, None]).astype(F32)
    bias = jnp.dot(rel_bias.astype(F32).T, onehot, precision=lax.Precision.HIGHEST)
    bias = bias.reshape(1, ATT_HEADS, ATT_Q_ROWS, ATT_KEYS)
    key = np.arange(ATT_KEYS)[None, :]
    in_band = np.abs(rel) <= RADIUS
    after_start = key >= RADIUS
    before_end = key < ATT_Q_ROWS + RADIUS
    keep = np.stack([in_band, in_band & after_start, in_band & before_end, in_band & after_start & before_end])
    return jnp.where(jnp.asarray(keep[:, None]), bias * LOG2E, NEG_INF)


def _silu(x):
    return x / (1.0 + jnp.exp(-x))


def _layernorm(z, g, b):
    mu = jnp.mean(z, axis=-1, keepdims=True)
    zc = z - mu
    var = jnp.mean(zc * zc, axis=-1, keepdims=True)
    return zc * lax.rsqrt(var + LN_EPS) * g + b


def _qkvg_kernel(x_ref, w_ref, o1_ref, o4_ref, o16_ref, ga_ref, xb_ref, s0_ref, s1_ref, mid0_ref, mid1_ref,
                 *, tm, tn, nq):
    j = pl.program_id(2)
    chunk_bufs = ((s0_ref, mid0_ref), (s1_ref, mid1_ref))
    r4, r16 = PROJ_ROWS // 4, PROJ_ROWS // 16

    @pl.when(j == 0)
    def _():
        xb_ref[...] = x_ref[...].astype(BF16)

    def product(rc, cc):
        rows = slice(rc * PROJ_ROWS, (rc + 1) * PROJ_ROWS)
        cols = slice(cc * 2 * LANES, (cc + 1) * 2 * LANES)
        return jnp.dot(xb_ref[rows, :], w_ref[:, cols], preferred_element_type=F32)

    def emit_qkv(acc, rc, cc, s_ref, mid_ref):
        acc = acc * jnp.where(j < ATT_WIDTH // tn, LOG2E * HEAD_DIM ** -0.5, 1.0)
        for c in range(2):
            slab = 2 * cc + c
            blk = acc[:, c * LANES:(c + 1) * LANES]
            o1_ref[0, slab, rc * PROJ_ROWS:(rc + 1) * PROJ_ROWS, :] = blk.astype(BF16)
            s_ref[c] = blk
            for g4 in range(4):
                rows4 = s_ref[c, pl.ds(g4, r4, stride=4), :]
                o4_ref[g4, slab, rc * r4:(rc + 1) * r4, :] = rows4.astype(BF16)
                mid_ref[c, g4] = rows4
            for g4 in range(4):
                for k in range(4):
                    o16_ref[g4 + 4 * k, slab, rc * r16:(rc + 1) * r16, :] = (
                        mid_ref[c, g4, pl.ds(k, r16, stride=4), :].astype(BF16))

    def emit_gate(acc, rc, cc):
        ga_ref[rc * PROJ_ROWS:(rc + 1) * PROJ_ROWS, cc * 2 * LANES:(cc + 1) * 2 * LANES] = _silu(acc).astype(BF16)

    chunks = [(rc, cc) for cc in range(tn // (2 * LANES)) for rc in range(tm // PROJ_ROWS)]

    @pl.when(j < nq)
    def _():
        for n, (rc, cc) in enumerate(chunks):
            emit_qkv(product(rc, cc), rc, cc, *chunk_bufs[n % 2])

    @pl.when(j >= nq)
    def _():
        for rc, cc in chunks:
            emit_gate(product(rc, cc), rc, cc)


def _qkvg_proj(x, w, *, tm=1024, tn=1024):
    b, s, dm = x.shape
    nj = 4 * ATT_WIDTH // tn
    nq = 3 * ATT_WIDTH // tn
    nslab = 3 * ATT_WIDTH // LANES
    outs = [jax.ShapeDtypeStruct((b, d, nslab, s // d, LANES), BF16) for d in DILATIONS]
    outs.append(jax.ShapeDtypeStruct((b, s, ATT_WIDTH), BF16))
    out_specs = [pl.BlockSpec((None, d, tn // LANES, tm // d, LANES),
                              lambda bi, i, j: (bi, 0, jnp.minimum(j, nq - 1), i, 0)) for d in DILATIONS]
    out_specs.append(pl.BlockSpec((None, tm, tn), lambda bi, i, j: (bi, i, jnp.maximum(j - nq, 0))))
    return pl.pallas_call(
        functools.partial(_qkvg_kernel, tm=tm, tn=tn, nq=nq),
        grid=(b, s // tm, nj),
        in_specs=[pl.BlockSpec((None, tm, dm), lambda bi, i, j: (bi, i, 0)),
                  pl.BlockSpec((dm, tn), lambda bi, i, j: (0, j))],
        out_specs=out_specs,
        out_shape=outs,
        scratch_shapes=[pltpu.VMEM((tm, dm), BF16)]
        + [pltpu.VMEM((2, PROJ_ROWS, LANES), F32)] * 2
        + [pltpu.VMEM((2, 4, PROJ_ROWS // 4, LANES), F32)] * 2,
        compiler_params=pltpu.CompilerParams(
            dimension_semantics=("parallel", "parallel", "arbitrary"), vmem_limit_bytes=VMEM_LIMIT),
        name="qkvg_proj",
    )(x, w)


def _pool_kernel(xp_ref, x_ref, xn_ref, wu_ref, wg_ref, pw_ref, ps_ref, o_ref, xb_ref, u_ref, *, tm, seq):
    i = pl.program_id(0)
    tiles_per_seq = seq // tm
    keep_prev = jnp.where((i % tiles_per_seq) == 0, 0.0, 1.0)
    keep_next = jnp.where((i % tiles_per_seq) == tiles_per_seq - 1, 0.0, 1.0)
    nch = tm // ROW_CHUNK

    xb_ref[0:HALO] = xp_ref[...].astype(BF16)
    xb_ref[HALO:HALO + tm] = x_ref[...].astype(BF16)
    xb_ref[HALO + tm:] = xn_ref[...].astype(BF16)

    def pool_input(r):
        lo = HALO + r * ROW_CHUNK - (HALO if r == 0 else 0)
        hi = HALO + (r + 1) * ROW_CHUNK + (HALO if r == nch - 1 else 0)
        u_ref[lo:hi] = jnp.dot(xb_ref[lo:hi], wu_ref[...], preferred_element_type=F32)
        if r == 0:
            u_ref[0:HALO] = u_ref[0:HALO] * keep_prev
        if r == nch - 1:
            u_ref[HALO + tm:] = u_ref[HALO + tm:] * keep_next

    def pooled_out(r):
        base = HALO + r * ROW_CHUNK
        rows = slice(r * ROW_CHUNK, (r + 1) * ROW_CHUNK)
        gate = jnp.dot(xb_ref[base:base + ROW_CHUNK], wg_ref[...], preferred_element_type=F32)
        pos = ((i % tiles_per_seq) * tm + r * ROW_CHUNK
               + lax.broadcasted_iota(jnp.int32, (ROW_CHUNK, POOL_GROUP), 0))
        for gi, w in enumerate(POOL_WINDOWS):
            cols = slice(gi * POOL_GROUP, (gi + 1) * POOL_GROUP)
            tot = u_ref[pl.ds(base - w // 2, ROW_CHUNK), cols]
            for off in range(-w // 2 + 1, w // 2):
                tot = tot + u_ref[pl.ds(base + off, ROW_CHUNK), cols]
            lo = jnp.maximum(pos - w // 2, 0)
            hi = jnp.minimum(pos + w // 2 - 1, seq - 1)
            cnt = (hi - lo + 1).astype(F32)
            pooled = tot / cnt - u_ref[pl.ds(base, ROW_CHUNK), cols]
            ob = jnp.dot(pooled.astype(BF16), pw_ref[gi], preferred_element_type=F32)
            o_ref[rows, cols] = (ob * ps_ref[:, cols] * _silu(gate[:, cols])).astype(BF16)

    pool_input(0)
    for r in range(nch):
        if r + 1 < nch:
            pool_input(r + 1)
        pooled_out(r)


def _pool_branch(x2, w_ab, pool_w, pool_scale, *, seq, tm=1024):
    m, dm = x2.shape
    ub = 4 * ATT_WIDTH // POOL_WIDTH
    hb = tm // HALO
    nhb = m // HALO
    return pl.pallas_call(
        functools.partial(_pool_kernel, tm=tm, seq=seq),
        grid=(m // tm,),
        in_specs=[pl.BlockSpec((HALO, dm), lambda i: (jnp.maximum(i * hb - 1, 0), 0)),
                  pl.BlockSpec((tm, dm), lambda i: (i, 0)),
                  pl.BlockSpec((HALO, dm), lambda i: (jnp.minimum((i + 1) * hb, nhb - 1), 0)),
                  pl.BlockSpec((dm, POOL_WIDTH), lambda i: (0, ub)),
                  pl.BlockSpec((dm, POOL_WIDTH), lambda i: (0, ub + 1)),
                  pl.BlockSpec((len(POOL_WINDOWS), POOL_GROUP, POOL_GROUP), lambda i: (0, 0, 0)),
                  pl.BlockSpec((1, POOL_WIDTH), lambda i: (0, 0))],
        out_specs=pl.BlockSpec((tm, POOL_WIDTH), lambda i: (i, 0)),
        out_shape=jax.ShapeDtypeStruct((m, POOL_WIDTH), BF16),
        scratch_shapes=[pltpu.VMEM((tm + 2 * HALO, dm), BF16), pltpu.VMEM((tm + 2 * HALO, POOL_WIDTH), F32)],
        compiler_params=pltpu.CompilerParams(dimension_semantics=("parallel",), vmem_limit_bytes=VMEM_LIMIT),
        name="pool_branch",
    )(x2, x2, x2, w_ab, w_ab, pool_w, pool_scale)


def _attn_kernel(q_ref, kp_ref, km_ref, kn_ref, vp_ref, vm_ref, vn_ref, bias_ref,
                 o_ref, m_ref, l_ref, s0_ref, s1_ref, *, qb, nblk):
    n = pl.program_id(1)
    na = qb // ATT_Q_ROWS
    hpi = max(1, ATT_UNITS // na)
    steps = ATT_HEADS // hpi
    m_ref[...] = jnp.zeros_like(m_ref)
    l_ref[...] = jnp.ones_like(l_ref)
    lane = lax.broadcasted_iota(jnp.int32, (ATT_Q_ROWS, LANES), 1)
    first = (n == 0).astype(jnp.int32)
    last = (n == nblk - 1).astype(jnp.int32)
    s_refs = (s0_ref, s1_ref)
    ones = jnp.ones((ATT_KEYS, LANES), BF16)

    def window(prev_ref, main_ref, next_ref, h, a):
        lo = a * ATT_Q_ROWS - RADIUS
        hi = lo + ATT_KEYS
        parts = []
        if lo < 0:
            parts.append(prev_ref[h])
        parts.append(main_ref[h, max(lo, 0):min(hi, qb), :])
        if hi > qb:
            parts.append(next_ref[h])
        return parts[0] if len(parts) == 1 else jnp.concatenate(parts, axis=0)

    def scores(step, slot):
        for j in range(hpi):
            h = step * hpi + j
            for a in range(na):
                q = q_ref[h, a * ATT_Q_ROWS:(a + 1) * ATT_Q_ROWS, :]
                k = window(kp_ref, km_ref, kn_ref, h, a)
                s_refs[slot][j * na + a] = lax.dot_general(
                    q, k, (((1,), (1,)), ((), ())), preferred_element_type=F32)

    def softmax_pv(step, slot):
        for j in range(hpi):
            h = step * hpi + j
            sel = lane == h
            for a in range(na):
                rows = slice(a * ATT_Q_ROWS, (a + 1) * ATT_Q_ROWS)
                variant = (first if a == 0 else 0) + (2 * last if a == na - 1 else 0)
                s = s_refs[slot][j * na + a] + bias_ref[variant, h]
                mx = jnp.max(s, axis=-1, keepdims=True)
                p = jnp.exp2(s - mx)
                v = window(vp_ref, vm_ref, vn_ref, h, a)
                pv = jnp.dot(p.astype(BF16), jnp.concatenate([v, ones], axis=1), preferred_element_type=F32)
                den = pv[:, HEAD_DIM:]
                o_ref[h, rows, :] = pv[:, :HEAD_DIM].astype(BF16)
                m_ref[rows, :] = jnp.where(sel, mx, m_ref[rows, :])
                l_ref[rows, :] = jnp.where(sel, den, l_ref[rows, :])

    scores(0, 0)

    assert steps % ATT_TRIP_STEPS == 0 and ATT_TRIP_STEPS % 2 == 0

    def body(tt, carry):
        for u in range(ATT_TRIP_STEPS):
            step = ATT_TRIP_STEPS * tt + u
            scores((step + 1) % steps, (u + 1) % 2)
            softmax_pv(step, u % 2)
        return carry

    lax.fori_loop(0, steps // ATT_TRIP_STEPS, body, 0)


def _banded_attention(hd, bias):
    nseq, _, length, _ = hd.shape
    qb = min(512, length)
    nblk = length // qb
    hb = qb // RADIUS
    nh = length // RADIUS
    blk = (None, ATT_HEADS, qb, LANES)
    halo = (None, ATT_HEADS, RADIUS, LANES)
    units = max(ATT_UNITS, qb // ATT_Q_ROWS)

    def main_spec(slab):
        return pl.BlockSpec(blk, lambda s, n: (s, slab, n, 0))

    def prev_spec(slab):
        return pl.BlockSpec(halo, lambda s, n: (s, slab, jnp.maximum(n * hb - 1, 0), 0))

    def next_spec(slab):
        return pl.BlockSpec(halo, lambda s, n: (s, slab, jnp.minimum((n + 1) * hb, nh - 1), 0))

    return pl.pallas_call(
        functools.partial(_attn_kernel, qb=qb, nblk=nblk),
        grid=(nseq, nblk),
        in_specs=[main_spec(0), prev_spec(1), main_spec(1), next_spec(1),
                  prev_spec(2), main_spec(2), next_spec(2),
                  pl.BlockSpec(bias.shape, lambda s, n: (0, 0, 0, 0), pipeline_mode=pl.Buffered(1))],
        out_specs=[pl.BlockSpec(blk, lambda s, n: (s, 0, n, 0)),
                   pl.BlockSpec((None, qb, LANES), lambda s, n: (s, n, 0)),
                   pl.BlockSpec((None, qb, LANES), lambda s, n: (s, n, 0))],
        out_shape=[jax.ShapeDtypeStruct((nseq, ATT_HEADS, length, LANES), BF16),
                   jax.ShapeDtypeStruct((nseq, length, LANES), F32),
                   jax.ShapeDtypeStruct((nseq, length, LANES), F32)],
        scratch_shapes=[pltpu.VMEM((units, ATT_Q_ROWS, ATT_KEYS), F32)] * 2,
        compiler_params=pltpu.CompilerParams(
            dimension_semantics=("parallel", "parallel"), vmem_limit_bytes=VMEM_LIMIT),
        name="banded_attention",
    )(hd, hd, hd, hd, hd, hd, hd, bias)


def _merge_kernel(o1_ref, o4_ref, o16_ref, m1_ref, l1_ref, m4_ref, l4_ref, m16_ref, l16_ref,
                  ga_ref, ob_ref, x_ref, w_ref, g_ref, b_ref, out_ref,
                  y0_ref, y1_ref, st_ref, r_ref, f_ref, *, tm, total):
    t = pl.program_id(0)
    ys = (y0_ref, y1_ref)
    ncol = w_ref.shape[1] // (2 * LANES)
    nrow = tm // ROW_CHUNK
    assert nrow * ncol == ATT_HEADS

    def merge_weights():
        st_ref[0] = m1_ref[0]
        st_ref[1] = l1_ref[0]
        for p, (d, mr, lr) in enumerate(((4, m4_ref, l4_ref), (16, m16_ref, l16_ref)), start=1):
            for g in range(d):
                st_ref[2 * p, pl.ds(g, tm // d, stride=d), :] = mr[g]
                st_ref[2 * p + 1, pl.ds(g, tm // d, stride=d), :] = lr[g]
        ms = [st_ref[2 * p] for p in range(3)]
        mx = jnp.maximum(jnp.maximum(ms[0], ms[1]), ms[2])
        es = [jnp.exp2(m - mx) for m in ms]
        den = es[0] * st_ref[1] + es[1] * st_ref[3] + es[2] * st_ref[5]
        for p in range(3):
            st_ref[2 * p] = es[p] / den

    def merge_head(dst, h):
        for p, (d, o_ref) in enumerate(((4, o4_ref), (16, o16_ref))):
            for g in range(d):
                r_ref[p, pl.ds(g, tm // d, stride=d), :] = o_ref[g, h].astype(F32)
        oa = (st_ref[0, :, h:h + 1] * o1_ref[0, h].astype(F32)
              + st_ref[2, :, h:h + 1] * r_ref[0]
              + st_ref[4, :, h:h + 1] * r_ref[1])
        cols = slice(h * HEAD_DIM, (h + 1) * HEAD_DIM)
        dst[:, cols] = (oa * ga_ref[:, cols].astype(F32)).astype(BF16)

    def product(src, k):
        rows = slice((k // ncol) * ROW_CHUNK, (k // ncol + 1) * ROW_CHUNK)
        cols = slice((k % ncol) * 2 * LANES, (k % ncol + 1) * 2 * LANES)
        f_ref[rows, cols] = jnp.dot(src[rows, :], w_ref[:, cols], preferred_element_type=F32)
        if k % ncol == ncol - 1:
            out_ref[rows, :] = _layernorm(DN_ALPHA * x_ref[rows, :] + f_ref[rows, :], g_ref[...], b_ref[...])

    @pl.when(t == 0)
    def _():
        merge_weights()
        for h in range(ATT_HEADS):
            merge_head(ys[0], h)
        ys[0][:, ATT_WIDTH:] = ob_ref[...]

    @pl.when(t == total)
    def _():
        for k in range(ATT_HEADS):
            product(ys[(total - 1) % 2], k)

    for parity in (0, 1):
        @pl.when((t > 0) & (t < total) & (t % 2 == parity))
        def _(parity=parity):
            merge_weights()
            for k in range(ATT_HEADS):
                product(ys[1 - parity], k)
                merge_head(ys[parity], k)
            ys[parity][:, ATT_WIDTH:] = ob_ref[...]


def _merge_outproj(o_pats, stats, ga, ob, x, w_out, ln_g, ln_b, *, tm=256):
    b, s, dm = x.shape
    nt = s // tm
    total = b * nt
    ga = ga.reshape(b, s, ATT_WIDTH)
    ob = ob.reshape(b, s, POOL_WIDTH)

    def cur(t):
        t = jnp.minimum(t, total - 1)
        return t // nt, t % nt

    def prev(t):
        t = jnp.maximum(t - 1, 0)
        return t // nt, t % nt

    def o_map(t):
        bi, i = cur(t)
        return bi, 0, 0, i, 0

    def st_map(t):
        bi, i = cur(t)
        return bi, 0, i, 0

    def cur_map(t):
        bi, i = cur(t)
        return bi, i, 0

    def prev_map(t):
        bi, i = prev(t)
        return bi, i, 0

    o_specs = [pl.BlockSpec((None, d, ATT_HEADS, tm // d, LANES), o_map) for d in DILATIONS]
    st_specs = []
    for d in DILATIONS:
        st_specs += [pl.BlockSpec((None, d, tm // d, LANES), st_map)] * 2
    const2 = lambda t: (0, 0)
    args = list(o_pats)
    for mm, ll in stats:
        args += [mm, ll]
    return pl.pallas_call(
        functools.partial(_merge_kernel, tm=tm, total=total),
        grid=(total + 1,),
        in_specs=o_specs + st_specs + [
            pl.BlockSpec((None, tm, ATT_WIDTH), cur_map),
            pl.BlockSpec((None, tm, POOL_WIDTH), cur_map),
            pl.BlockSpec((None, tm, dm), prev_map),
            pl.BlockSpec(w_out.shape, const2, pipeline_mode=pl.Buffered(1)),
            pl.BlockSpec((1, dm), const2),
            pl.BlockSpec((1, dm), const2)],
        out_specs=pl.BlockSpec((None, tm, dm), prev_map),
        out_shape=jax.ShapeDtypeStruct((b, s, dm), F32),
        scratch_shapes=[pltpu.VMEM((tm, ATT_WIDTH + POOL_WIDTH), BF16),
                        pltpu.VMEM((tm, ATT_WIDTH + POOL_WIDTH), BF16),
                        pltpu.VMEM((6, tm, LANES), F32),
                        pltpu.VMEM((2, tm, LANES), F32),
                        pltpu.VMEM((tm, dm), F32)],
        compiler_params=pltpu.CompilerParams(dimension_semantics=("arbitrary",), vmem_limit_bytes=VMEM_LIMIT),
        name="merge_outproj",
    )(*args, ga, ob, x, w_out, ln_g, ln_b)


def _conv_in_kernel(xp_ref, x_ref, xn_ref, wb_ref, wc_ref, wv_ref, wg_ref, cw_ref, o_ref, xb_ref, cv_ref,
                    *, tm, seq):
    i = pl.program_id(0)
    tiles_per_seq = seq // tm

    @pl.when(pl.program_id(1) == 0)
    def _():
        xb_ref[0:HALO] = xp_ref[...].astype(BF16)
        xb_ref[HALO:HALO + tm] = x_ref[...].astype(BF16)
        xb_ref[HALO + tm:] = xn_ref[...].astype(BF16)

    keep_prev = jnp.where((i % tiles_per_seq) == 0, 0.0, 1.0)
    keep_next = jnp.where((i % tiles_per_seq) == tiles_per_seq - 1, 0.0, 1.0)
    nch = tm // ROW_CHUNK

    def conv_input(r):
        lo = HALO + r * ROW_CHUNK - (HALO if r == 0 else 0)
        hi = HALO + (r + 1) * ROW_CHUNK + (HALO if r == nch - 1 else 0)
        xa = xb_ref[lo:hi]
        cv_ref[lo:hi] = (jnp.dot(xa, wc_ref[...], preferred_element_type=F32)
                         * jnp.dot(xa, wv_ref[...], preferred_element_type=F32))
        if r == 0:
            cv_ref[0:HALO] = cv_ref[0:HALO] * keep_prev
        if r == nch - 1:
            cv_ref[HALO + tm:] = cv_ref[HALO + tm:] * keep_next

    def gated_conv(r):
        base = HALO + r * ROW_CHUNK
        xm = xb_ref[base:base + ROW_CHUNK]
        gb = jnp.dot(xm, wb_ref[...], preferred_element_type=F32)
        gate = jnp.dot(xm, wg_ref[...], preferred_element_type=F32)
        conv = (cv_ref[pl.ds(base - 1, ROW_CHUNK), :] * cw_ref[0:1, :]
                + cv_ref[pl.ds(base, ROW_CHUNK), :] * cw_ref[1:2, :]
                + cv_ref[pl.ds(base + 1, ROW_CHUNK), :] * cw_ref[2:3, :])
        o_ref[r * ROW_CHUNK:(r + 1) * ROW_CHUNK, :] = (gb * conv * _silu(gate)).astype(BF16)

    conv_input(0)
    for r in range(nch):
        if r + 1 < nch:
            conv_input(r + 1)
        gated_conv(r)


def _conv_inproj(x2, w_in, conv_w, *, seq, tm=1024, tc=512):
    m, dm = x2.shape
    hb = tm // HALO
    nhb = m // HALO
    nc = CONV_WIDTH // tc

    def wspec(part):
        return pl.BlockSpec((dm, tc), lambda i, c: (0, part * nc + c))

    return pl.pallas_call(
        functools.partial(_conv_in_kernel, tm=tm, seq=seq),
        grid=(m // tm, nc),
        in_specs=[pl.BlockSpec((HALO, dm), lambda i, c: (jnp.maximum(i * hb - 1, 0), 0)),
                  pl.BlockSpec((tm, dm), lambda i, c: (i, 0)),
                  pl.BlockSpec((HALO, dm), lambda i, c: (jnp.minimum((i + 1) * hb, nhb - 1), 0)),
                  wspec(0), wspec(1), wspec(2), wspec(3),
                  pl.BlockSpec((3, tc), lambda i, c: (0, c))],
        out_specs=pl.BlockSpec((tm, tc), lambda i, c: (i, c)),
        out_shape=jax.ShapeDtypeStruct((m, CONV_WIDTH), BF16),
        scratch_shapes=[pltpu.VMEM((tm + 2 * HALO, dm), BF16), pltpu.VMEM((tm + 2 * HALO, tc), F32)],
        compiler_params=pltpu.CompilerParams(
            dimension_semantics=("parallel", "arbitrary"), vmem_limit_bytes=VMEM_LIMIT),
        name="conv_inproj",
    )(x2, x2, x2, w_in, w_in, w_in, w_in, conv_w)


def _outproj_ln_kernel(y_ref, x_ref, w_ref, g_ref, b_ref, o_ref):
    for r in range(0, y_ref.shape[0], ROW_CHUNK):
        rows = slice(r, r + ROW_CHUNK)
        f = jnp.dot(y_ref[rows, :], w_ref[...], preferred_element_type=F32)
        o_ref[rows, :] = _layernorm(DN_ALPHA * x_ref[rows, :] + f, g_ref[...], b_ref[...])


def _outproj_ln(y, x2, w_out, ln_g, ln_b, *, tm=1024):
    m, dm = x2.shape
    const2 = lambda i: (0, 0)
    return pl.pallas_call(
        _outproj_ln_kernel,
        grid=(m // tm,),
        in_specs=[pl.BlockSpec((tm, y.shape[1]), lambda i: (i, 0)),
                  pl.BlockSpec((tm, dm), lambda i: (i, 0)),
                  pl.BlockSpec(w_out.shape, const2, pipeline_mode=pl.Buffered(1)),
                  pl.BlockSpec((1, dm), const2),
                  pl.BlockSpec((1, dm), const2)],
        out_specs=pl.BlockSpec((tm, dm), lambda i: (i, 0)),
        out_shape=jax.ShapeDtypeStruct((m, dm), F32),
        compiler_params=pltpu.CompilerParams(dimension_semantics=("parallel",), vmem_limit_bytes=VMEM_LIMIT),
        name="outproj_ln",
    )(y, x2, w_out, ln_g, ln_b)


def _trunk(x, biases, w_ab, pool_w, pool_scale, w_out_ab, w_in_c, conv_w, w_out_c, ln_g, ln_b):
    b, s, dm = x.shape
    x2 = x.reshape(b * s, dm)
    *hds, ga = _qkvg_proj(x, w_ab)
    ob = _pool_branch(x2, w_ab, pool_w, pool_scale, seq=s)
    o_pats, stats = [], []
    for d, hd, bias in zip(DILATIONS, hds, biases):
        ld = s // d
        o, mm, ll = _banded_attention(hd.reshape(b * d, 3 * ATT_HEADS, ld, LANES), bias)
        o_pats.append(o.reshape(b, d, ATT_HEADS, ld, LANES))
        stats.append((mm.reshape(b, d, ld, LANES), ll.reshape(b, d, ld, LANES)))
    x1 = _merge_outproj(o_pats, stats, ga, ob, x, w_out_ab, ln_g[0:1], ln_b[0:1])
    x1 = x1.reshape(b * s, dm)
    y = _conv_inproj(x1, w_in_c, conv_w, seq=s)
    out = _outproj_ln(y, x1, w_out_c, ln_g[1:2], ln_b[1:2])
    return out.reshape(b, s, dm)


def kernel(x_prompt, x_sample, rel_bias, w_in_ab, pool_w, pool_scale, w_out_ab, w_in_c, conv_w, w_out_c, ln_g, ln_b):
    assert DEPTH == 2 and w_in_ab.shape[0] == 1 and w_in_c.shape[0] == 1
    params = dict(
        biases=[_band_bias(rel_bias, d) for d in DILATIONS],
        w_ab=w_in_ab[0].astype(BF16),
        pool_w=pool_w[0].astype(BF16),
        pool_scale=pool_scale[0].reshape(1, POOL_WIDTH),
        w_out_ab=w_out_ab[0].astype(BF16),
        w_in_c=w_in_c[0].astype(BF16),
        conv_w=conv_w[0],
        w_out_c=w_out_c[0].astype(BF16),
        ln_g=ln_g,
        ln_b=ln_b,
    )
    return (_trunk(x_prompt, **params), _trunk(x_sample, **params))
```

```python
import functools
import math

import numpy as np
import jax
import jax.numpy as jnp
from jax import lax
from jax.experimental import pallas as pl
from jax.experimental.pallas import tpu as pltpu

D_MODEL = 2048
DEPTH = 2
ATT_HEADS = 16
HEAD_DIM = 128
ATT_WIDTH = ATT_HEADS * HEAD_DIM
DILATED_PATTERNS = ((128, 1), (512, 4), (2048, 16))
DILATIONS = tuple(d for _, d in DILATED_PATTERNS)
RADIUS = 64
POOL_WINDOWS = (2, 4, 8, 16)
POOL_WIDTH = D_MODEL // 2
POOL_GROUP = POOL_WIDTH // len(POOL_WINDOWS)
CONV_WIDTH = D_MODEL
REL_BUCKETS = 32
REL_MAX_DISTANCE = 1024
DN_ALPHA = (2 * DEPTH) ** 0.25
LN_EPS = 1e-5
NEG_INF = -1e30
LOG2E = math.log2(math.e)

LANES = 128
HALO = 16
ATT_Q_ROWS = 128
ATT_KEYS = ATT_Q_ROWS + 2 * RADIUS
ATT_UNITS = 4
ATT_TRIP_STEPS = 4
ATT_BLOCK_ROWS = 1024
ATT_MIN_STEP_ROWS = 512
PROJ_ROWS = 256
ROW_CHUNK = 128
VMEM_LIMIT = 56 * 1024 * 1024

BF16 = jnp.bfloat16
F32 = jnp.float32

assert all(w // (2 * d) == RADIUS for w, d in DILATED_PATTERNS)


def _t5_bucket(rel):
    nb = REL_BUCKETS // 2
    max_exact = nb // 2
    ret = np.where(rel > 0, nb, 0)
    n = np.abs(rel)
    n_safe = np.maximum(n, 1).astype(np.float64)
    large = max_exact + (np.log(n_safe / max_exact) / math.log(REL_MAX_DISTANCE / max_exact)
                         * (nb - max_exact)).astype(np.int64)
    large = np.minimum(large, nb - 1)
    return (ret + np.where(n < max_exact, n, large)).astype(np.int32)


def _band_bias(rel_bias, dilation):
    rel = np.arange(ATT_KEYS)[None, :] - RADIUS - np.arange(ATT_Q_ROWS)[:, None]
    bucket = jnp.asarray(_t5_bucket(rel * dilation).reshape(1, -1))
    onehot = (bucket == jnp.arange(REL_BUCKETS, dtype=jnp.int32)[:, None]).astype(F32)
    bias = jnp.dot(rel_bias.astype(F32).T, onehot, precision=lax.Precision.HIGHEST)
    bias = bias.reshape(1, ATT_HEADS, ATT_Q_ROWS, ATT_KEYS)
    key = np.arange(ATT_KEYS)[None, :]
    in_band = np.abs(rel) <= RADIUS
    after_start = key >= RADIUS
    before_end = key < ATT_Q_ROWS + RADIUS
    keep = np.stack([in_band, in_band & after_start, in_band & before_end, in_band & after_start & before_end])
    return jnp.where(jnp.asarray(keep[:, None]), bias * LOG2E, NEG_INF)


def _silu(x):
    return x / (1.0 + jnp.exp(-x))


def _layernorm(z, g, b):
    mu = jnp.mean(z, axis=-1, keepdims=True)
    zc = z - mu
    var = jnp.mean(zc * zc, axis=-1, keepdims=True)
    return zc * lax.rsqrt(var + LN_EPS) * g + b


def _qkvg_kernel(x_ref, w_ref, o1_ref, o4_ref, o16_ref, ga_ref, xb_ref, s0_ref, s1_ref, mid0_ref, mid1_ref,
                 *, tm, tn, nq):
    j = pl.program_id(2)
    chunk_bufs = ((s0_ref, mid0_ref), (s1_ref, mid1_ref))
    r4, r16 = PROJ_ROWS // 4, PROJ_ROWS // 16

    @pl.when(j == 0)
    def _():
        xb_ref[...] = x_ref[...].astype(BF16)

    def product(rc, cc):
        rows = slice(rc * PROJ_ROWS, (rc + 1) * PROJ_ROWS)
        cols = slice(cc * 2 * LANES, (cc + 1) * 2 * LANES)
        return jnp.dot(xb_ref[rows, :], w_ref[:, cols], preferred_element_type=F32)

    def emit_qkv(acc, rc, cc, s_ref, mid_ref):
        acc = acc * jnp.where(j < ATT_WIDTH // tn, LOG2E * HEAD_DIM ** -0.5, 1.0)
        for c in range(2):
            slab = 2 * cc + c
            blk = acc[:, c * LANES:(c + 1) * LANES]
            o1_ref[0, slab, rc * PROJ_ROWS:(rc + 1) * PROJ_ROWS, :] = blk.astype(BF16)
            s_ref[c] = blk
            for g4 in range(4):
                rows4 = s_ref[c, pl.ds(g4, r4, stride=4), :]
                o4_ref[g4, slab, rc * r4:(rc + 1) * r4, :] = rows4.astype(BF16)
                mid_ref[c, g4] = rows4
            for g4 in range(4):
                for k in range(4):
                    o16_ref[g4 + 4 * k, slab, rc * r16:(rc + 1) * r16, :] = (
                        mid_ref[c, g4, pl.ds(k, r16, stride=4), :].astype(BF16))

    def emit_gate(acc, rc, cc):
        ga_ref[rc * PROJ_ROWS:(rc + 1) * PROJ_ROWS, cc * 2 * LANES:(cc + 1) * 2 * LANES] = _silu(acc).astype(BF16)

    chunks = [(rc, cc) for cc in range(tn // (2 * LANES)) for rc in range(tm // PROJ_ROWS)]

    @pl.when(j < nq)
    def _():
        for n, (rc, cc) in enumerate(chunks):
            emit_qkv(product(rc, cc), rc, cc, *chunk_bufs[n % 2])

    @pl.when(j >= nq)
    def _():
        for rc, cc in chunks:
            emit_gate(product(rc, cc), rc, cc)


def _qkvg_proj(x, w, *, tm=1024, tn=1024):
    b, s, dm = x.shape
    nj = 4 * ATT_WIDTH // tn
    nq = 3 * ATT_WIDTH // tn
    qt = ATT_WIDTH // tn
    nslab = 3 * ATT_WIDTH // LANES
    outs = [jax.ShapeDtypeStruct((b, d, nslab, s // d, LANES), BF16) for d in DILATIONS]
    outs.append(jax.ShapeDtypeStruct((b, s, ATT_WIDTH), BF16))
    out_specs = [pl.BlockSpec((None, d, tn // LANES, tm // d, LANES),
                              lambda bi, i, j: (bi, 0, (jnp.minimum(j, nq - 1) + 2 * qt) % nq, i, 0)) for d in DILATIONS]
    out_specs.append(pl.BlockSpec((None, tm, tn), lambda bi, i, j: (bi, i, jnp.maximum(j - nq, 0))))
    return pl.pallas_call(
        functools.partial(_qkvg_kernel, tm=tm, tn=tn, nq=nq),
        grid=(b, s // tm, nj),
        in_specs=[pl.BlockSpec((None, tm, dm), lambda bi, i, j: (bi, i, 0)),
                  pl.BlockSpec((dm, tn), lambda bi, i, j: (0, j))],
        out_specs=out_specs,
        out_shape=outs,
        scratch_shapes=[pltpu.VMEM((tm, dm), BF16)]
        + [pltpu.VMEM((2, PROJ_ROWS, LANES), F32)] * 2
        + [pltpu.VMEM((2, 4, PROJ_ROWS // 4, LANES), F32)] * 2,
        compiler_params=pltpu.CompilerParams(
            dimension_semantics=("parallel", "parallel", "arbitrary"), vmem_limit_bytes=VMEM_LIMIT),
        name="qkvg_proj",
    )(x, w)


def _pool_kernel(xp_ref, x_ref, xn_ref, wu_ref, wg_ref, pw_ref, ps_ref, o_ref, xb_ref, u_ref, *, tm, seq):
    i = pl.program_id(0)
    tiles_per_seq = seq // tm
    keep_prev = jnp.where((i % tiles_per_seq) == 0, 0.0, 1.0)
    keep_next = jnp.where((i % tiles_per_seq) == tiles_per_seq - 1, 0.0, 1.0)
    nch = tm // ROW_CHUNK

    xb_ref[0:HALO] = xp_ref[...].astype(BF16)
    xb_ref[HALO:HALO + tm] = x_ref[...].astype(BF16)
    xb_ref[HALO + tm:] = xn_ref[...].astype(BF16)

    def pool_input(r):
        lo = HALO + r * ROW_CHUNK - (HALO if r == 0 else 0)
        hi = HALO + (r + 1) * ROW_CHUNK + (HALO if r == nch - 1 else 0)
        u_ref[lo:hi] = jnp.dot(xb_ref[lo:hi], wu_ref[...], preferred_element_type=F32)
        if r == 0:
            u_ref[0:HALO] = u_ref[0:HALO] * keep_prev
        if r == nch - 1:
            u_ref[HALO + tm:] = u_ref[HALO + tm:] * keep_next

    def pooled_out(r):
        base = HALO + r * ROW_CHUNK
        rows = slice(r * ROW_CHUNK, (r + 1) * ROW_CHUNK)
        gate = jnp.dot(xb_ref[base:base + ROW_CHUNK], wg_ref[...], preferred_element_type=F32)
        pos = ((i % tiles_per_seq) * tm + r * ROW_CHUNK
               + lax.broadcasted_iota(jnp.int32, (ROW_CHUNK, POOL_GROUP), 0))
        for gi, w in enumerate(POOL_WINDOWS):
            cols = slice(gi * POOL_GROUP, (gi + 1) * POOL_GROUP)
            tot = u_ref[pl.ds(base - w // 2, ROW_CHUNK), cols]
            for off in range(-w // 2 + 1, w // 2):
                tot = tot + u_ref[pl.ds(base + off, ROW_CHUNK), cols]
            lo = jnp.maximum(pos - w // 2, 0)
            hi = jnp.minimum(pos + w // 2 - 1, seq - 1)
            cnt = (hi - lo + 1).astype(F32)
            pooled = tot / cnt - u_ref[pl.ds(base, ROW_CHUNK), cols]
            ob = jnp.dot(pooled.astype(BF16), pw_ref[gi], preferred_element_type=F32)
            o_ref[rows, cols] = (ob * ps_ref[:, cols] * _silu(gate[:, cols])).astype(BF16)

    pool_input(0)
    for r in range(nch):
        if r + 1 < nch:
            pool_input(r + 1)
        pooled_out(r)


def _pool_branch(x2, w_ab, pool_w, pool_scale, *, seq, tm=512):
    m, dm = x2.shape
    ub = 4 * ATT_WIDTH // POOL_WIDTH
    hb = tm // HALO
    nhb = m // HALO
    return pl.pallas_call(
        functools.partial(_pool_kernel, tm=tm, seq=seq),
        grid=(m // tm,),
        in_specs=[pl.BlockSpec((HALO, dm), lambda i: (jnp.maximum(i * hb - 1, 0), 0)),
                  pl.BlockSpec((tm, dm), lambda i: (i, 0)),
                  pl.BlockSpec((HALO, dm), lambda i: (jnp.minimum((i + 1) * hb, nhb - 1), 0)),
                  pl.BlockSpec((dm, POOL_WIDTH), lambda i: (0, ub)),
                  pl.BlockSpec((dm, POOL_WIDTH), lambda i: (0, ub + 1)),
                  pl.BlockSpec((len(POOL_WINDOWS), POOL_GROUP, POOL_GROUP), lambda i: (0, 0, 0)),
                  pl.BlockSpec((1, POOL_WIDTH), lambda i: (0, 0))],
        out_specs=pl.BlockSpec((tm, POOL_WIDTH), lambda i: (i, 0)),
        out_shape=jax.ShapeDtypeStruct((m, POOL_WIDTH), BF16),
        scratch_shapes=[pltpu.VMEM((tm + 2 * HALO, dm), BF16), pltpu.VMEM((tm + 2 * HALO, POOL_WIDTH), F32)],
        compiler_params=pltpu.CompilerParams(dimension_semantics=("parallel",), vmem_limit_bytes=VMEM_LIMIT),
        name="pool_branch",
    )(x2, x2, x2, w_ab, w_ab, pool_w, pool_scale)


def _attn_kernel(q_ref, kvp_ref, kvm_ref, kvn_ref, bias_ref, o_ref, st_ref, s0_ref, s1_ref, *, qb, nblk, ns):
    n = pl.program_id(1)
    na = qb // ATT_Q_ROWS
    ng = max(1, na // ATT_UNITS)
    apg = na // ng
    hpi = max(1, ATT_UNITS // na)
    steps = (ATT_HEADS // hpi) * ng
    assert steps % ATT_TRIP_STEPS == 0 and ATT_TRIP_STEPS % (2 * ng) == 0
    lane = lax.broadcasted_iota(jnp.int32, (ATT_Q_ROWS, LANES), 1)
    st_ref[...] = jnp.broadcast_to(
        jnp.where((lane[:1] >= ATT_HEADS) & (lane[:1] < 2 * ATT_HEADS), 1.0, 0.0), st_ref.shape)
    first = (n == 0).astype(jnp.int32)
    last = (n == nblk - 1).astype(jnp.int32)
    s_refs = (s0_ref, s1_ref)
    ones = jnp.ones((ATT_KEYS, LANES), BF16)

    def window(si, slab, a):
        lo = a * ATT_Q_ROWS - RADIUS
        hi = lo + ATT_KEYS
        parts = []
        if lo < 0:
            parts.append(kvp_ref[si, slab])
        parts.append(kvm_ref[si, slab, max(lo, 0):min(hi, qb), :])
        if hi > qb:
            parts.append(kvn_ref[si, slab])
        return parts[0] if len(parts) == 1 else jnp.concatenate(parts, axis=0)

    def scores(si, step, grp, slot):
        for j in range(hpi):
            h = (step // ng) * hpi + j
            for ai in range(apg):
                a = grp * apg + ai
                q = q_ref[si, h, a * ATT_Q_ROWS:(a + 1) * ATT_Q_ROWS, :]
                s_refs[slot][j * apg + ai] = lax.dot_general(
                    q, window(si, h, a), (((1,), (1,)), ((), ())), preferred_element_type=F32)

    def softmax_pv(si, step, grp, slot):
        for j in range(hpi):
            h = (step // ng) * hpi + j
            for ai in range(apg):
                a = grp * apg + ai
                rows = slice(a * ATT_Q_ROWS, (a + 1) * ATT_Q_ROWS)
                variant = (first if a == 0 else 0) + (2 * last if a == na - 1 else 0)
                s = s_refs[slot][j * apg + ai] + bias_ref[variant, h]
                mx = jnp.max(s, axis=-1, keepdims=True)
                p = jnp.exp2(s - mx)
                v = window(si, ATT_HEADS + h, a)
                pv = jnp.dot(p.astype(BF16), jnp.concatenate([v, ones], axis=1), preferred_element_type=F32)
                o_ref[si, h, rows, :] = pv[:, :HEAD_DIM].astype(BF16)
                st_ref[si, rows, :] = jnp.where(
                    lane == h, mx, jnp.where(lane == h + ATT_HEADS, pv[:, HEAD_DIM:], st_ref[si, rows, :]))

    def sequence(si, carry):
        scores(si, 0, 0, 0)

        def body(tt, c):
            for u in range(ATT_TRIP_STEPS):
                step = ATT_TRIP_STEPS * tt + u
                scores(si, (step + 1) % steps, (u + 1) % ng, (u + 1) % 2)
                softmax_pv(si, step, u % ng, u % 2)
            return c

        lax.fori_loop(0, steps // ATT_TRIP_STEPS, body, 0)
        return carry

    if ns == 1:
        sequence(0, 0)
    else:
        lax.fori_loop(0, ns, sequence, 0)


def _banded_attention(hd, bias):
    nseq, _, length, _ = hd.shape
    qb = min(ATT_BLOCK_ROWS, length)
    nblk = length // qb
    ns = max(1, ATT_MIN_STEP_ROWS // length)
    hb = qb // RADIUS
    nh = length // RADIUS
    units = max(1, ATT_UNITS // (qb // ATT_Q_ROWS)) * min(ATT_UNITS, qb // ATT_Q_ROWS)
    return pl.pallas_call(
        functools.partial(_attn_kernel, qb=qb, nblk=nblk, ns=ns),
        grid=(nseq // ns, nblk),
        in_specs=[pl.BlockSpec((ns, ATT_HEADS, qb, LANES), lambda s, n: (s, 2, n, 0)),
                  pl.BlockSpec((ns, 2 * ATT_HEADS, RADIUS, LANES),
                               lambda s, n: (s, 0, jnp.maximum(n * hb - 1, 0), 0)),
                  pl.BlockSpec((ns, 2 * ATT_HEADS, qb, LANES), lambda s, n: (s, 0, n, 0)),
                  pl.BlockSpec((ns, 2 * ATT_HEADS, RADIUS, LANES),
                               lambda s, n: (s, 0, jnp.minimum((n + 1) * hb, nh - 1), 0)),
                  pl.BlockSpec(bias.shape, lambda s, n: (0, 0, 0, 0), pipeline_mode=pl.Buffered(1))],
        out_specs=[pl.BlockSpec((ns, ATT_HEADS, qb, LANES), lambda s, n: (s, 0, n, 0)),
                   pl.BlockSpec((ns, qb, LANES), lambda s, n: (s, n, 0))],
        out_shape=[jax.ShapeDtypeStruct((nseq, ATT_HEADS, length, LANES), BF16),
                   jax.ShapeDtypeStruct((nseq, length, LANES), F32)],
        scratch_shapes=[pltpu.VMEM((units, ATT_Q_ROWS, ATT_KEYS), F32)] * 2,
        compiler_params=pltpu.CompilerParams(
            dimension_semantics=("parallel", "parallel"), vmem_limit_bytes=VMEM_LIMIT),
        name="banded_attention",
    )(hd, hd, hd, hd, bias)


def _merge_kernel(o1_ref, o4_ref, o16_ref, s1_ref, s4_ref, s16_ref,
                  ga_ref, ob_ref, x_ref, w_ref, g_ref, b_ref, out_ref,
                  y0_ref, y1_ref, st_ref, r_ref, f_ref, *, tm, total):
    t = pl.program_id(0)
    ys = (y0_ref, y1_ref)
    ncol = w_ref.shape[1] // (2 * LANES)
    nrow = tm // ROW_CHUNK
    assert nrow * ncol == ATT_HEADS

    def merge_weights():
        st_ref[0] = s1_ref[0]
        for p, (d, sr) in enumerate(((4, s4_ref), (16, s16_ref)), start=1):
            for g in range(d):
                st_ref[p, pl.ds(g, tm // d, stride=d), :] = sr[g]
        ms = [st_ref[p] for p in range(3)]
        mx = jnp.maximum(jnp.maximum(ms[0], ms[1]), ms[2])
        es = [jnp.exp2(m - mx) for m in ms]
        den = sum(e * pltpu.roll(m, LANES - ATT_HEADS, axis=1) for e, m in zip(es, ms))
        head_lane = lax.broadcasted_iota(jnp.int32, den.shape, 1) < ATT_HEADS
        den = jnp.where(head_lane, den, 1.0)
        for p in range(3):
            st_ref[p] = es[p] / den

    def merge_head(dst, h):
        for p, (d, o_ref) in enumerate(((4, o4_ref), (16, o16_ref))):
            for g in range(d):
                r_ref[p, pl.ds(g, tm // d, stride=d), :] = o_ref[g, h].astype(F32)
        oa = (st_ref[0, :, h:h + 1] * o1_ref[0, h].astype(F32)
              + st_ref[1, :, h:h + 1] * r_ref[0]
              + st_ref[2, :, h:h + 1] * r_ref[1])
        cols = slice(h * HEAD_DIM, (h + 1) * HEAD_DIM)
        dst[:, cols] = (oa * ga_ref[:, cols].astype(F32)).astype(BF16)

    def product(src, k):
        rows = slice((k // ncol) * ROW_CHUNK, (k // ncol + 1) * ROW_CHUNK)
        cols = slice((k % ncol) * 2 * LANES, (k % ncol + 1) * 2 * LANES)
        f_ref[rows, cols] = jnp.dot(src[rows, :], w_ref[:, cols], preferred_element_type=F32)
        if k % ncol == ncol - 1:
            out_ref[rows, :] = _layernorm(DN_ALPHA * x_ref[rows, :] + f_ref[rows, :], g_ref[...], b_ref[...])

    @pl.when(t == 0)
    def _():
        merge_weights()
        for h in range(ATT_HEADS):
            merge_head(ys[0], h)
        ys[0][:, ATT_WIDTH:] = ob_ref[...]

    @pl.when(t == total)
    def _():
        for k in range(ATT_HEADS):
            product(ys[(total - 1) % 2], k)

    for parity in (0, 1):
        @pl.when((t > 0) & (t < total) & (t % 2 == parity))
        def _(parity=parity):
            merge_weights()
            for k in range(ATT_HEADS):
                product(ys[1 - parity], k)
                merge_head(ys[parity], k)
            ys[parity][:, ATT_WIDTH:] = ob_ref[...]


def _merge_outproj(o_pats, stats, ga, ob, x, w_out, ln_g, ln_b, *, tm=256):
    b, s, dm = x.shape
    nt = s // tm
    total = b * nt
    ga = ga.reshape(b, s, ATT_WIDTH)
    ob = ob.reshape(b, s, POOL_WIDTH)

    def cur(t):
        t = jnp.minimum(t, total - 1)
        return t // nt, t % nt

    def prev(t):
        t = jnp.maximum(t - 1, 0)
        return t // nt, t % nt

    def o_map(t):
        bi, i = cur(t)
        return bi, 0, 0, i, 0

    def st_map(t):
        bi, i = cur(t)
        return bi, 0, i, 0

    def cur_map(t):
        bi, i = cur(t)
        return bi, i, 0

    def prev_map(t):
        bi, i = prev(t)
        return bi, i, 0

    o_specs = [pl.BlockSpec((None, d, ATT_HEADS, tm // d, LANES), o_map) for d in DILATIONS]
    st_specs = [pl.BlockSpec((None, d, tm // d, LANES), st_map) for d in DILATIONS]
    const2 = lambda t: (0, 0)
    args = list(o_pats) + list(stats)
    return pl.pallas_call(
        functools.partial(_merge_kernel, tm=tm, total=total),
        grid=(total + 1,),
        in_specs=o_specs + st_specs + [
            pl.BlockSpec((None, tm, ATT_WIDTH), cur_map),
            pl.BlockSpec((None, tm, POOL_WIDTH), cur_map),
            pl.BlockSpec((None, tm, dm), prev_map),
            pl.BlockSpec(w_out.shape, const2, pipeline_mode=pl.Buffered(1)),
            pl.BlockSpec((1, dm), const2),
            pl.BlockSpec((1, dm), const2)],
        out_specs=pl.BlockSpec((None, tm, dm), prev_map),
        out_shape=jax.ShapeDtypeStruct((b, s, dm), F32),
        scratch_shapes=[pltpu.VMEM((tm, ATT_WIDTH + POOL_WIDTH), BF16),
                        pltpu.VMEM((tm, ATT_WIDTH + POOL_WIDTH), BF16),
                        pltpu.VMEM((3, tm, LANES), F32),
                        pltpu.VMEM((2, tm, LANES), F32),
                        pltpu.VMEM((tm, dm), F32)],
        compiler_params=pltpu.CompilerParams(dimension_semantics=("arbitrary",), vmem_limit_bytes=VMEM_LIMIT),
        name="merge_outproj",
    )(*args, ga, ob, x, w_out, ln_g, ln_b)


def _conv_in_kernel(xp_ref, x_ref, xn_ref, wb_ref, wc_ref, wv_ref, wg_ref, cw_ref, o_ref, xb_ref, cv_ref,
                    *, tm, seq):
    i = pl.program_id(0)
    tiles_per_seq = seq // tm

    @pl.when(pl.program_id(1) == 0)
    def _():
        xb_ref[0:HALO] = xp_ref[...].astype(BF16)
        xb_ref[HALO:HALO + tm] = x_ref[...].astype(BF16)
        xb_ref[HALO + tm:] = xn_ref[...].astype(BF16)

    keep_prev = jnp.where((i % tiles_per_seq) == 0, 0.0, 1.0)
    keep_next = jnp.where((i % tiles_per_seq) == tiles_per_seq - 1, 0.0, 1.0)
    nch = tm // ROW_CHUNK

    def conv_input(r):
        lo = HALO + r * ROW_CHUNK - (HALO if r == 0 else 0)
        hi = HALO + (r + 1) * ROW_CHUNK + (HALO if r == nch - 1 else 0)
        xa = xb_ref[lo:hi]
        cv_ref[lo:hi] = (jnp.dot(xa, wc_ref[...], preferred_element_type=F32)
                         * jnp.dot(xa, wv_ref[...], preferred_element_type=F32))
        if r == 0:
            cv_ref[0:HALO] = cv_ref[0:HALO] * keep_prev
        if r == nch - 1:
            cv_ref[HALO + tm:] = cv_ref[HALO + tm:] * keep_next

    def gated_conv(r):
        base = HALO + r * ROW_CHUNK
        xm = xb_ref[base:base + ROW_CHUNK]
        gb = jnp.dot(xm, wb_ref[...], preferred_element_type=F32)
        gate = jnp.dot(xm, wg_ref[...], preferred_element_type=F32)
        conv = (cv_ref[pl.ds(base - 1, ROW_CHUNK), :] * cw_ref[0:1, :]
                + cv_ref[pl.ds(base, ROW_CHUNK), :] * cw_ref[1:2, :]
                + cv_ref[pl.ds(base + 1, ROW_CHUNK), :] * cw_ref[2:3, :])
        o_ref[r * ROW_CHUNK:(r + 1) * ROW_CHUNK, :] = (gb * conv * _silu(gate)).astype(BF16)

    conv_input(0)
    for r in range(nch):
        if r + 1 < nch:
            conv_input(r + 1)
        gated_conv(r)


def _conv_inproj(x2, w_in, conv_w, *, seq, tm=1024, tc=512):
    m, dm = x2.shape
    hb = tm // HALO
    nhb = m // HALO
    nc = CONV_WIDTH // tc

    def wspec(part):
        return pl.BlockSpec((dm, tc), lambda i, c: (0, part * nc + c))

    return pl.pallas_call(
        functools.partial(_conv_in_kernel, tm=tm, seq=seq),
        grid=(m // tm, nc),
        in_specs=[pl.BlockSpec((HALO, dm), lambda i, c: (jnp.maximum(i * hb - 1, 0), 0)),
                  pl.BlockSpec((tm, dm), lambda i, c: (i, 0)),
                  pl.BlockSpec((HALO, dm), lambda i, c: (jnp.minimum((i + 1) * hb, nhb - 1), 0)),
                  wspec(0), wspec(1), wspec(2), wspec(3),
                  pl.BlockSpec((3, tc), lambda i, c: (0, c))],
        out_specs=pl.BlockSpec((tm, tc), lambda i, c: (i, c)),
        out_shape=jax.ShapeDtypeStruct((m, CONV_WIDTH), BF16),
        scratch_shapes=[pltpu.VMEM((tm + 2 * HALO, dm), BF16), pltpu.VMEM((tm + 2 * HALO, tc), F32)],
        compiler_params=pltpu.CompilerParams(
            dimension_semantics=("parallel", "arbitrary"), vmem_limit_bytes=VMEM_LIMIT),
        name="conv_inproj",
    )(x2, x2, x2, w_in, w_in, w_in, w_in, conv_w)


def _outproj_ln_kernel(y_ref, x_ref, w_ref, g_ref, b_ref, o_ref):
    for r in range(0, y_ref.shape[0], ROW_CHUNK):
        rows = slice(r, r + ROW_CHUNK)
        f = jnp.dot(y_ref[rows, :], w_ref[...], preferred_element_type=F32)
        o_ref[rows, :] = _layernorm(DN_ALPHA * x_ref[rows, :] + f, g_ref[...], b_ref[...])


def _outproj_ln(y, x2, w_out, ln_g, ln_b, *, tm=512):
    m, dm = x2.shape
    const2 = lambda i: (0, 0)
    return pl.pallas_call(
        _outproj_ln_kernel,
        grid=(m // tm,),
        in_specs=[pl.BlockSpec((tm, y.shape[1]), lambda i: (i, 0)),
                  pl.BlockSpec((tm, dm), lambda i: (i, 0)),
                  pl.BlockSpec(w_out.shape, const2, pipeline_mode=pl.Buffered(1)),
                  pl.BlockSpec((1, dm), const2),
                  pl.BlockSpec((1, dm), const2)],
        out_specs=pl.BlockSpec((tm, dm), lambda i: (i, 0)),
        out_shape=jax.ShapeDtypeStruct((m, dm), F32),
        compiler_params=pltpu.CompilerParams(dimension_semantics=("parallel",), vmem_limit_bytes=VMEM_LIMIT),
        name="outproj_ln",
    )(y, x2, w_out, ln_g, ln_b)


def _trunk(x, biases, w_ab, pool_w, pool_scale, w_out_ab, w_in_c, conv_w, w_out_c, ln_g, ln_b):
    b, s, dm = x.shape
    x2 = x.reshape(b * s, dm)
    *hds, ga = _qkvg_proj(x, w_ab)
    ob = _pool_branch(x2, w_ab, pool_w, pool_scale, seq=s)
    o_pats, stats = [], []
    for d, hd, bias in zip(DILATIONS, hds, biases):
        ld = s // d
        o, st = _banded_attention(hd.reshape(b * d, 3 * ATT_HEADS, ld, LANES), bias)
        o_pats.append(o.reshape(b, d, ATT_HEADS, ld, LANES))
        stats.append(st.reshape(b, d, ld, LANES))
    x1 = _merge_outproj(o_pats, stats, ga, ob, x, w_out_ab, ln_g[0:1], ln_b[0:1])
    x1 = x1.reshape(b * s, dm)
    y = _conv_inproj(x1, w_in_c, conv_w, seq=s)
    out = _outproj_ln(y, x1, w_out_c, ln_g[1:2], ln_b[1:2])
    return out.reshape(b, s, dm)


def kernel(x_prompt, x_sample, rel_bias, w_in_ab, pool_w, pool_scale, w_out_ab, w_in_c, conv_w, w_out_c, ln_g, ln_b):
    assert DEPTH == 2 and w_in_ab.shape[0] == 1 and w_in_c.shape[0] == 1
    params = dict(
        biases=[_band_bias(rel_bias, d) for d in DILATIONS],
        w_ab=w_in_ab[0].astype(BF16),
        pool_w=pool_w[0].astype(BF16),
        pool_scale=pool_scale[0].reshape(1, POOL_WIDTH),
        w_out_ab=w_out_ab[0].astype(BF16),
        w_in_c=w_in_c[0].astype(BF16),
        conv_w=conv_w[0],
        w_out_c=w_out_c[0].astype(BF16),
        ln_g=ln_g,
        ln_b=ln_b,
    )
    return (_trunk(x_prompt, **params), _trunk(x_sample, **params))
```

```python
import functools
import math

import numpy as np
import jax
import jax.numpy as jnp
from jax import lax
from jax.experimental import pallas as pl
from jax.experimental.pallas import tpu as pltpu

D_MODEL = 2048
DEPTH = 2
ATT_HEADS = 16
HEAD_DIM = 128
ATT_WIDTH = ATT_HEADS * HEAD_DIM
DILATED_PATTERNS = ((128, 1), (512, 4), (2048, 16))
DILATIONS = tuple(d for _, d in DILATED_PATTERNS)
RADIUS = 64
POOL_WINDOWS = (2, 4, 8, 16)
POOL_WIDTH = D_MODEL // 2
POOL_GROUP = POOL_WIDTH // len(POOL_WINDOWS)
CONV_WIDTH = D_MODEL
REL_BUCKETS = 32
REL_MAX_DISTANCE = 1024
DN_ALPHA = (2 * DEPTH) ** 0.25
LN_EPS = 1e-5
NEG_INF = -1e30
LOG2E = math.log2(math.e)

LANES = 128
HALO = 16
ATT_Q_ROWS = 128
ATT_KEYS = ATT_Q_ROWS + 2 * RADIUS
ATT_UNITS = 4
ATT_TRIP_STEPS = 4
ATT_BLOCK_ROWS = 1024
ATT_MIN_STEP_ROWS = 512
PROJ_ROWS = 256
ROW_CHUNK = 128
VMEM_LIMIT = 56 * 1024 * 1024

BF16 = jnp.bfloat16
F32 = jnp.float32

assert all(w // (2 * d) == RADIUS for w, d in DILATED_PATTERNS)


def _t5_bucket(rel):
    nb = REL_BUCKETS // 2
    max_exact = nb // 2
    ret = np.where(rel > 0, nb, 0)
    n = np.abs(rel)
    n_safe = np.maximum(n, 1).astype(np.float64)
    large = max_exact + (np.log(n_safe / max_exact) / math.log(REL_MAX_DISTANCE / max_exact)
                         * (nb - max_exact)).astype(np.int64)
    large = np.minimum(large, nb - 1)
    return (ret + np.where(n < max_exact, n, large)).astype(np.int32)


def _band_bias(rel_bias, dilation):
    rel = np.arange(ATT_KEYS)[None, :] - RADIUS - np.arange(ATT_Q_ROWS)[:, None]
    bucket = jnp.asarray(_t5_bucket(rel * dilation).reshape(1, -1))
    onehot = (bucket == jnp.arange(REL_BUCKETS, dtype=jnp.int32)[:, None]).astype(F32)
    bias = jnp.dot(rel_bias.astype(F32).T, onehot, precision=lax.Precision.HIGHEST)
    bias = bias.reshape(1, ATT_HEADS, ATT_Q_ROWS, ATT_KEYS)
    key = np.arange(ATT_KEYS)[None, :]
    in_band = np.abs(rel) <= RADIUS
    after_start = key >= RADIUS
    before_end = key < ATT_Q_ROWS + RADIUS
    keep = np.stack([in_band, in_band & after_start, in_band & before_end, in_band & after_start & before_end])
    return jnp.where(jnp.asarray(keep[:, None]), bias * LOG2E, NEG_INF)


def _silu(x):
    return x / (1.0 + jnp.exp(-x))


def _layernorm(z, g, b):
    mu = jnp.mean(z, axis=-1, keepdims=True)
    zc = z - mu
    var = jnp.mean(zc * zc, axis=-1, keepdims=True)
    return zc * lax.rsqrt(var + LN_EPS) * g + b


def _qkvg_kernel(x_ref, w_ref, o1_ref, o4_ref, o16_ref, ga_ref, xb_ref, s0_ref, s1_ref, mid0_ref, mid1_ref,
                 *, tm, tn, nq):
    j = pl.program_id(2)
    chunk_bufs = ((s0_ref, mid0_ref), (s1_ref, mid1_ref))
    r4, r16 = PROJ_ROWS // 4, PROJ_ROWS // 16

    @pl.when(j == 0)
    def _():
        xb_ref[...] = x_ref[...].astype(BF16)

    def product(rc, cc):
        rows = slice(rc * PROJ_ROWS, (rc + 1) * PROJ_ROWS)
        cols = slice(cc * 2 * LANES, (cc + 1) * 2 * LANES)
        return jnp.dot(xb_ref[rows, :], w_ref[:, cols], preferred_element_type=F32)

    def emit_qkv(acc, rc, cc, s_ref, mid_ref):
        acc = acc * jnp.where(j < ATT_WIDTH // tn, LOG2E * HEAD_DIM ** -0.5, 1.0)
        for c in range(2):
            slab = 2 * cc + c
            blk = acc[:, c * LANES:(c + 1) * LANES]
            o1_ref[0, slab, rc * PROJ_ROWS:(rc + 1) * PROJ_ROWS, :] = blk.astype(BF16)
            s_ref[c] = blk
            for g4 in range(4):
                rows4 = s_ref[c, pl.ds(g4, r4, stride=4), :]
                o4_ref[g4, slab, rc * r4:(rc + 1) * r4, :] = rows4.astype(BF16)
                mid_ref[c, g4] = rows4
            for g4 in range(4):
                for k in range(4):
                    o16_ref[g4 + 4 * k, slab, rc * r16:(rc + 1) * r16, :] = (
                        mid_ref[c, g4, pl.ds(k, r16, stride=4), :].astype(BF16))

    def emit_gate(acc, rc, cc):
        ga_ref[rc * PROJ_ROWS:(rc + 1) * PROJ_ROWS, cc * 2 * LANES:(cc + 1) * 2 * LANES] = _silu(acc).astype(BF16)

    chunks = [(rc, cc) for cc in range(tn // (2 * LANES)) for rc in range(tm // PROJ_ROWS)]

    @pl.when(j < nq)
    def _():
        for n, (rc, cc) in enumerate(chunks):
            emit_qkv(product(rc, cc), rc, cc, *chunk_bufs[n % 2])

    @pl.when(j >= nq)
    def _():
        for rc, cc in chunks:
            emit_gate(product(rc, cc), rc, cc)


def _qkvg_proj(x, w, *, tm=1024, tn=1024):
    b, s, dm = x.shape
    nj = 4 * ATT_WIDTH // tn
    nq = 3 * ATT_WIDTH // tn
    qt = ATT_WIDTH // tn
    nslab = 3 * ATT_WIDTH // LANES
    outs = [jax.ShapeDtypeStruct((b, d, nslab, s // d, LANES), BF16) for d in DILATIONS]
    outs.append(jax.ShapeDtypeStruct((b, s, ATT_WIDTH), BF16))
    out_specs = [pl.BlockSpec((None, d, tn // LANES, tm // d, LANES),
                              lambda bi, i, j: (bi, 0, (jnp.minimum(j, nq - 1) + 2 * qt) % nq, i, 0)) for d in DILATIONS]
    out_specs.append(pl.BlockSpec((None, tm, tn), lambda bi, i, j: (bi, i, jnp.maximum(j - nq, 0))))
    return pl.pallas_call(
        functools.partial(_qkvg_kernel, tm=tm, tn=tn, nq=nq),
        grid=(b, s // tm, nj),
        in_specs=[pl.BlockSpec((None, tm, dm), lambda bi, i, j: (bi, i, 0)),
                  pl.BlockSpec((dm, tn), lambda bi, i, j: (0, j))],
        out_specs=out_specs,
        out_shape=outs,
        scratch_shapes=[pltpu.VMEM((tm, dm), BF16)]
        + [pltpu.VMEM((2, PROJ_ROWS, LANES), F32)] * 2
        + [pltpu.VMEM((2, 4, PROJ_ROWS // 4, LANES), F32)] * 2,
        compiler_params=pltpu.CompilerParams(
            dimension_semantics=("parallel", "parallel", "arbitrary"), vmem_limit_bytes=VMEM_LIMIT),
        name="qkvg_proj",
    )(x, w)


def _pool_kernel(xp_ref, x_ref, xn_ref, wu_ref, wg_ref, pw_ref, ps_ref, o_ref, xb_ref, u_ref, *, tm, seq):
    i = pl.program_id(0)
    tiles_per_seq = seq // tm
    keep_prev = jnp.where((i % tiles_per_seq) == 0, 0.0, 1.0)
    keep_next = jnp.where((i % tiles_per_seq) == tiles_per_seq - 1, 0.0, 1.0)
    nch = tm // ROW_CHUNK

    xb_ref[0:HALO] = xp_ref[...].astype(BF16)
    xb_ref[HALO:HALO + tm] = x_ref[...].astype(BF16)
    xb_ref[HALO + tm:] = xn_ref[...].astype(BF16)

    def pool_input(r):
        lo = HALO + r * ROW_CHUNK - (HALO if r == 0 else 0)
        hi = HALO + (r + 1) * ROW_CHUNK + (HALO if r == nch - 1 else 0)
        u_ref[lo:hi] = jnp.dot(xb_ref[lo:hi], wu_ref[...], preferred_element_type=F32)
        if r == 0:
            u_ref[0:HALO] = u_ref[0:HALO] * keep_prev
        if r == nch - 1:
            u_ref[HALO + tm:] = u_ref[HALO + tm:] * keep_next

    def pooled_out(r):
        base = HALO + r * ROW_CHUNK
        rows = slice(r * ROW_CHUNK, (r + 1) * ROW_CHUNK)
        gate = jnp.dot(xb_ref[base:base + ROW_CHUNK], wg_ref[...], preferred_element_type=F32)
        pos = ((i % tiles_per_seq) * tm + r * ROW_CHUNK
               + lax.broadcasted_iota(jnp.int32, (ROW_CHUNK, POOL_GROUP), 0))
        for gi, w in enumerate(POOL_WINDOWS):
            cols = slice(gi * POOL_GROUP, (gi + 1) * POOL_GROUP)
            tot = u_ref[pl.ds(base - w // 2, ROW_CHUNK), cols]
            for off in range(-w // 2 + 1, w // 2):
                tot = tot + u_ref[pl.ds(base + off, ROW_CHUNK), cols]
            lo = jnp.maximum(pos - w // 2, 0)
            hi = jnp.minimum(pos + w // 2 - 1, seq - 1)
            cnt = (hi - lo + 1).astype(F32)
            pooled = tot / cnt - u_ref[pl.ds(base, ROW_CHUNK), cols]
            ob = jnp.dot(pooled.astype(BF16), pw_ref[gi], preferred_element_type=F32)
            o_ref[rows, cols] = (ob * ps_ref[:, cols] * _silu(gate[:, cols])).astype(BF16)

    pool_input(0)
    for r in range(nch):
        if r + 1 < nch:
            pool_input(r + 1)
        pooled_out(r)


def _pool_branch(x2, w_ab, pool_w, pool_scale, *, seq, tm=512):
    m, dm = x2.shape
    ub = 4 * ATT_WIDTH // POOL_WIDTH
    hb = tm // HALO
    nhb = m // HALO
    return pl.pallas_call(
        functools.partial(_pool_kernel, tm=tm, seq=seq),
        grid=(m // tm,),
        in_specs=[pl.BlockSpec((HALO, dm), lambda i: (jnp.maximum(i * hb - 1, 0), 0)),
                  pl.BlockSpec((tm, dm), lambda i: (i, 0)),
                  pl.BlockSpec((HALO, dm), lambda i: (jnp.minimum((i + 1) * hb, nhb - 1), 0)),
                  pl.BlockSpec((dm, POOL_WIDTH), lambda i: (0, ub)),
                  pl.BlockSpec((dm, POOL_WIDTH), lambda i: (0, ub + 1)),
                  pl.BlockSpec((len(POOL_WINDOWS), POOL_GROUP, POOL_GROUP), lambda i: (0, 0, 0)),
                  pl.BlockSpec((1, POOL_WIDTH), lambda i: (0, 0))],
        out_specs=pl.BlockSpec((tm, POOL_WIDTH), lambda i: (i, 0)),
        out_shape=jax.ShapeDtypeStruct((m, POOL_WIDTH), BF16),
        scratch_shapes=[pltpu.VMEM((tm + 2 * HALO, dm), BF16), pltpu.VMEM((tm + 2 * HALO, POOL_WIDTH), F32)],
        compiler_params=pltpu.CompilerParams(dimension_semantics=("parallel",), vmem_limit_bytes=VMEM_LIMIT),
        name="pool_branch",
    )(x2, x2, x2, w_ab, w_ab, pool_w, pool_scale)


def _attn_kernel(q_ref, kvp_ref, kvm_ref, kvn_ref, bias_ref, o_ref, st_ref, s0_ref, s1_ref, *, qb, nblk, ns):
    n = pl.program_id(1)
    na = qb // ATT_Q_ROWS
    ng = max(1, na // ATT_UNITS)
    apg = na // ng
    hpi = max(1, ATT_UNITS // na)
    steps = (ATT_HEADS // hpi) * ng
    assert steps % ATT_TRIP_STEPS == 0 and ATT_TRIP_STEPS % (2 * ng) == 0
    lane = lax.broadcasted_iota(jnp.int32, (ATT_Q_ROWS, LANES), 1)
    st_ref[...] = jnp.broadcast_to(
        jnp.where((lane[:1] >= ATT_HEADS) & (lane[:1] < 2 * ATT_HEADS), 1.0, 0.0), st_ref.shape)
    first = (n == 0).astype(jnp.int32)
    last = (n == nblk - 1).astype(jnp.int32)
    s_refs = (s0_ref, s1_ref)
    ones = jnp.ones((ATT_KEYS, LANES), BF16)

    def window(si, slab, a):
        lo = a * ATT_Q_ROWS - RADIUS
        hi = lo + ATT_KEYS
        parts = []
        if lo < 0:
            parts.append(kvp_ref[si, slab])
        parts.append(kvm_ref[si, slab, max(lo, 0):min(hi, qb), :])
        if hi > qb:
            parts.append(kvn_ref[si, slab])
        return parts[0] if len(parts) == 1 else jnp.concatenate(parts, axis=0)

    def scores(si, step, grp, slot):
        for j in range(hpi):
            h = (step // ng) * hpi + j
            for ai in range(apg):
                a = grp * apg + ai
                q = q_ref[si, h, a * ATT_Q_ROWS:(a + 1) * ATT_Q_ROWS, :]
                s_refs[slot][j * apg + ai] = lax.dot_general(
                    q, window(si, h, a), (((1,), (1,)), ((), ())), preferred_element_type=F32)

    def softmax_pv(si, step, grp, slot):
        for j in range(hpi):
            h = (step // ng) * hpi + j
            for ai in range(apg):
                a = grp * apg + ai
                rows = slice(a * ATT_Q_ROWS, (a + 1) * ATT_Q_ROWS)
                variant = (first if a == 0 else 0) + (2 * last if a == na - 1 else 0)
                s = s_refs[slot][j * apg + ai] + bias_ref[variant, h]
                mx = jnp.max(s, axis=-1, keepdims=True)
                p = jnp.exp2(s - mx)
                v = window(si, ATT_HEADS + h, a)
                pv = jnp.dot(p.astype(BF16), jnp.concatenate([v, ones], axis=1), preferred_element_type=F32)
                o_ref[si, h, rows, :] = pv[:, :HEAD_DIM].astype(BF16)
                st_ref[si, rows, :] = jnp.where(
                    lane == h, mx, jnp.where(lane == h + ATT_HEADS, pv[:, HEAD_DIM:], st_ref[si, rows, :]))

    def sequence(si, carry):
        scores(si, 0, 0, 0)

        def body(tt, c):
            for u in range(ATT_TRIP_STEPS):
                step = ATT_TRIP_STEPS * tt + u
                scores(si, (step + 1) % steps, (u + 1) % ng, (u + 1) % 2)
                softmax_pv(si, step, u % ng, u % 2)
            return c

        lax.fori_loop(0, steps // ATT_TRIP_STEPS, body, 0)
        return carry

    if ns == 1:
        sequence(0, 0)
    else:
        lax.fori_loop(0, ns, sequence, 0)


def _banded_attention(hd, bias):
    nseq, _, length, _ = hd.shape
    qb = min(ATT_BLOCK_ROWS, length)
    nblk = length // qb
    ns = max(1, ATT_MIN_STEP_ROWS // length)
    hb = qb // RADIUS
    nh = length // RADIUS
    units = max(1, ATT_UNITS // (qb // ATT_Q_ROWS)) * min(ATT_UNITS, qb // ATT_Q_ROWS)
    return pl.pallas_call(
        functools.partial(_attn_kernel, qb=qb, nblk=nblk, ns=ns),
        grid=(nseq // ns, nblk),
        in_specs=[pl.BlockSpec((ns, ATT_HEADS, qb, LANES), lambda s, n: (s, 2, n, 0)),
                  pl.BlockSpec((ns, 2 * ATT_HEADS, RADIUS, LANES),
                               lambda s, n: (s, 0, jnp.maximum(n * hb - 1, 0), 0)),
                  pl.BlockSpec((ns, 2 * ATT_HEADS, qb, LANES), lambda s, n: (s, 0, n, 0)),
                  pl.BlockSpec((ns, 2 * ATT_HEADS, RADIUS, LANES),
                               lambda s, n: (s, 0, jnp.minimum((n + 1) * hb, nh - 1), 0)),
                  pl.BlockSpec(bias.shape, lambda s, n: (0, 0, 0, 0), pipeline_mode=pl.Buffered(1))],
        out_specs=[pl.BlockSpec((ns, ATT_HEADS, qb, LANES), lambda s, n: (s, 0, n, 0)),
                   pl.BlockSpec((ns, qb, LANES), lambda s, n: (s, n, 0))],
        out_shape=[jax.ShapeDtypeStruct((nseq, ATT_HEADS, length, LANES), BF16),
                   jax.ShapeDtypeStruct((nseq, length, LANES), F32)],
        scratch_shapes=[pltpu.VMEM((units, ATT_Q_ROWS, ATT_KEYS), F32)] * 2,
        compiler_params=pltpu.CompilerParams(
            dimension_semantics=("parallel", "parallel"), vmem_limit_bytes=VMEM_LIMIT),
        name="banded_attention",
    )(hd, hd, hd, hd, bias)


def _merge_kernel(o1_ref, o4_ref, o16_ref, s1_ref, s4_ref, s16_ref,
                  ga_ref, ob_ref, x_ref, w_ref, g_ref, b_ref, out_ref, outb_ref,
                  y0_ref, y1_ref, st_ref, r_ref, f_ref, *, tm, total):
    t = pl.program_id(0)
    ys = (y0_ref, y1_ref)
    ncol = w_ref.shape[1] // (2 * LANES)
    hpp = ATT_HEADS // ncol

    def merge_weights():
        st_ref[0] = s1_ref[0]
        for p, (d, sr) in enumerate(((4, s4_ref), (16, s16_ref)), start=1):
            for g in range(d):
                st_ref[p, pl.ds(g, tm // d, stride=d), :] = sr[g]
        ms = [st_ref[p] for p in range(3)]
        mx = jnp.maximum(jnp.maximum(ms[0], ms[1]), ms[2])
        es = [jnp.exp2(m - mx) for m in ms]
        den = sum(e * pltpu.roll(m, LANES - ATT_HEADS, axis=1) for e, m in zip(es, ms))
        head_lane = lax.broadcasted_iota(jnp.int32, den.shape, 1) < ATT_HEADS
        den = jnp.where(head_lane, den, 1.0)
        for p in range(3):
            st_ref[p] = es[p] / den

    def merge_head(dst, h):
        for p, (d, o_ref) in enumerate(((4, o4_ref), (16, o16_ref))):
            for g in range(d):
                r_ref[p, pl.ds(g, tm // d, stride=d), :] = o_ref[g, h].astype(F32)
        oa = (st_ref[0, :, h:h + 1] * o1_ref[0, h].astype(F32)
              + st_ref[1, :, h:h + 1] * r_ref[0]
              + st_ref[2, :, h:h + 1] * r_ref[1])
        cols = slice(h * HEAD_DIM, (h + 1) * HEAD_DIM)
        dst[:, cols] = (oa * ga_ref[:, cols].astype(F32)).astype(BF16)

    def product(src, k):
        cols = slice(k * 2 * LANES, (k + 1) * 2 * LANES)
        f_ref[:, cols] = jnp.dot(src[...], w_ref[:, cols], preferred_element_type=F32)
        if k == ncol - 1:
            for r in range(0, tm, ROW_CHUNK):
                rows = slice(r, r + ROW_CHUNK)
                y = _layernorm(DN_ALPHA * x_ref[rows, :] + f_ref[rows, :], g_ref[...], b_ref[...])
                out_ref[rows, :] = y
                outb_ref[rows, :] = y.astype(BF16)

    @pl.when(t == 0)
    def _():
        merge_weights()
        for h in range(ATT_HEADS):
            merge_head(ys[0], h)
        ys[0][:, ATT_WIDTH:] = ob_ref[...]

    @pl.when(t == total)
    def _():
        for k in range(ncol):
            product(ys[(total - 1) % 2], k)

    for parity in (0, 1):
        @pl.when((t > 0) & (t < total) & (t % 2 == parity))
        def _(parity=parity):
            merge_weights()
            for k in range(ncol):
                product(ys[1 - parity], k)
                for h in range(k * hpp, (k + 1) * hpp):
                    merge_head(ys[parity], h)
            ys[parity][:, ATT_WIDTH:] = ob_ref[...]


def _merge_outproj(o_pats, stats, ga, ob, x, w_out, ln_g, ln_b, *, tm=256):
    b, s, dm = x.shape
    nt = s // tm
    total = b * nt
    ga = ga.reshape(b, s, ATT_WIDTH)
    ob = ob.reshape(b, s, POOL_WIDTH)

    def cur(t):
        t = jnp.minimum(t, total - 1)
        return t // nt, t % nt

    def prev(t):
        t = jnp.maximum(t - 1, 0)
        return t // nt, t % nt

    def o_map(t):
        bi, i = cur(t)
        return bi, 0, 0, i, 0

    def st_map(t):
        bi, i = cur(t)
        return bi, 0, i, 0

    def cur_map(t):
        bi, i = cur(t)
        return bi, i, 0

    def prev_map(t):
        bi, i = prev(t)
        return bi, i, 0

    o_specs = [pl.BlockSpec((None, d, ATT_HEADS, tm // d, LANES), o_map) for d in DILATIONS]
    st_specs = [pl.BlockSpec((None, d, tm // d, LANES), st_map) for d in DILATIONS]
    const2 = lambda t: (0, 0)
    args = list(o_pats) + list(stats)
    return pl.pallas_call(
        functools.partial(_merge_kernel, tm=tm, total=total),
        grid=(total + 1,),
        in_specs=o_specs + st_specs + [
            pl.BlockSpec((None, tm, ATT_WIDTH), cur_map),
            pl.BlockSpec((None, tm, POOL_WIDTH), cur_map),
            pl.BlockSpec((None, tm, dm), prev_map),
            pl.BlockSpec(w_out.shape, const2, pipeline_mode=pl.Buffered(1)),
            pl.BlockSpec((1, dm), const2),
            pl.BlockSpec((1, dm), const2)],
        out_specs=[pl.BlockSpec((None, tm, dm), prev_map), pl.BlockSpec((None, tm, dm), prev_map)],
        out_shape=[jax.ShapeDtypeStruct((b, s, dm), F32), jax.ShapeDtypeStruct((b, s, dm), BF16)],
        scratch_shapes=[pltpu.VMEM((tm, ATT_WIDTH + POOL_WIDTH), BF16),
                        pltpu.VMEM((tm, ATT_WIDTH + POOL_WIDTH), BF16),
                        pltpu.VMEM((3, tm, LANES), F32),
                        pltpu.VMEM((2, tm, LANES), F32),
                        pltpu.VMEM((tm, dm), F32)],
        compiler_params=pltpu.CompilerParams(dimension_semantics=("arbitrary",), vmem_limit_bytes=VMEM_LIMIT),
        name="merge_outproj",
    )(*args, ga, ob, x, w_out, ln_g, ln_b)


def _conv_in_kernel(xp_ref, x_ref, xn_ref, wb_ref, wc_ref, wv_ref, wg_ref, cw_ref, o_ref, cv_ref, *, tm, seq):
    i = pl.program_id(0)
    tiles_per_seq = seq // tm
    keep_prev = jnp.where((i % tiles_per_seq) == 0, 0.0, 1.0)
    keep_next = jnp.where((i % tiles_per_seq) == tiles_per_seq - 1, 0.0, 1.0)
    nch = tm // ROW_CHUNK

    def conv_input(r):
        parts = [x_ref[r * ROW_CHUNK:(r + 1) * ROW_CHUNK, :]]
        if r == 0:
            parts.insert(0, xp_ref[...])
        if r == nch - 1:
            parts.append(xn_ref[...])
        xa = parts[0] if len(parts) == 1 else jnp.concatenate(parts, axis=0)
        lo = HALO + r * ROW_CHUNK - (HALO if r == 0 else 0)
        cv_ref[lo:lo + xa.shape[0]] = (jnp.dot(xa, wc_ref[...], preferred_element_type=F32)
                                       * jnp.dot(xa, wv_ref[...], preferred_element_type=F32))
        if r == 0:
            cv_ref[0:HALO] = cv_ref[0:HALO] * keep_prev
        if r == nch - 1:
            cv_ref[HALO + tm:] = cv_ref[HALO + tm:] * keep_next

    def gated_conv(r):
        base = HALO + r * ROW_CHUNK
        xm = x_ref[r * ROW_CHUNK:(r + 1) * ROW_CHUNK, :]
        gb = jnp.dot(xm, wb_ref[...], preferred_element_type=F32)
        gate = jnp.dot(xm, wg_ref[...], preferred_element_type=F32)
        conv = (cv_ref[pl.ds(base - 1, ROW_CHUNK), :] * cw_ref[0:1, :]
                + cv_ref[pl.ds(base, ROW_CHUNK), :] * cw_ref[1:2, :]
                + cv_ref[pl.ds(base + 1, ROW_CHUNK), :] * cw_ref[2:3, :])
        o_ref[r * ROW_CHUNK:(r + 1) * ROW_CHUNK, :] = (gb * conv * _silu(gate)).astype(BF16)

    conv_input(0)
    for r in range(nch):
        if r + 1 < nch:
            conv_input(r + 1)
        gated_conv(r)


def _conv_inproj(xb, w_in, conv_w, *, seq, tm=1024, tc=1024):
    m, dm = xb.shape
    hb = tm // HALO
    nhb = m // HALO
    nc = CONV_WIDTH // tc

    def wspec(part):
        return pl.BlockSpec((dm, tc), lambda i, c: (0, part * nc + c))

    return pl.pallas_call(
        functools.partial(_conv_in_kernel, tm=tm, seq=seq),
        grid=(m // tm, nc),
        in_specs=[pl.BlockSpec((HALO, dm), lambda i, c: (jnp.maximum(i * hb - 1, 0), 0)),
                  pl.BlockSpec((tm, dm), lambda i, c: (i, 0)),
                  pl.BlockSpec((HALO, dm), lambda i, c: (jnp.minimum((i + 1) * hb, nhb - 1), 0)),
                  wspec(0), wspec(1), wspec(2), wspec(3),
                  pl.BlockSpec((3, tc), lambda i, c: (0, c))],
        out_specs=pl.BlockSpec((tm, tc), lambda i, c: (i, c)),
        out_shape=jax.ShapeDtypeStruct((m, CONV_WIDTH), BF16),
        scratch_shapes=[pltpu.VMEM((tm + 2 * HALO, tc), F32)],
        compiler_params=pltpu.CompilerParams(
            dimension_semantics=("parallel", "parallel"), vmem_limit_bytes=VMEM_LIMIT),
        name="conv_inproj",
    )(xb, xb, xb, w_in, w_in, w_in, w_in, conv_w)


def _outproj_ln_kernel(y_ref, x_ref, w_ref, g_ref, b_ref, o_ref):
    for r in range(0, y_ref.shape[0], ROW_CHUNK):
        rows = slice(r, r + ROW_CHUNK)
        f = jnp.dot(y_ref[rows, :], w_ref[...], preferred_element_type=F32)
        o_ref[rows, :] = _layernorm(DN_ALPHA * x_ref[rows, :] + f, g_ref[...], b_ref[...])


def _outproj_ln(y, x2, w_out, ln_g, ln_b, *, tm=512):
    m, dm = x2.shape
    const2 = lambda i: (0, 0)
    return pl.pallas_call(
        _outproj_ln_kernel,
        grid=(m // tm,),
        in_specs=[pl.BlockSpec((tm, y.shape[1]), lambda i: (i, 0)),
                  pl.BlockSpec((tm, dm), lambda i: (i, 0)),
                  pl.BlockSpec(w_out.shape, const2, pipeline_mode=pl.Buffered(1)),
                  pl.BlockSpec((1, dm), const2),
                  pl.BlockSpec((1, dm), const2)],
        out_specs=pl.BlockSpec((tm, dm), lambda i: (i, 0)),
        out_shape=jax.ShapeDtypeStruct((m, dm), F32),
        compiler_params=pltpu.CompilerParams(dimension_semantics=("parallel",), vmem_limit_bytes=VMEM_LIMIT),
        name="outproj_ln",
    )(y, x2, w_out, ln_g, ln_b)


def _trunk(x, biases, w_ab, pool_w, pool_scale, w_out_ab, w_in_c, conv_w, w_out_c, ln_g, ln_b):
    b, s, dm = x.shape
    x2 = x.reshape(b * s, dm)
    *hds, ga = _qkvg_proj(x, w_ab)
    ob = _pool_branch(x2, w_ab, pool_w, pool_scale, seq=s)
    o_pats, stats = [], []
    for d, hd, bias in zip(DILATIONS, hds, biases):
        ld = s // d
        o, st = _banded_attention(hd.reshape(b * d, 3 * ATT_HEADS, ld, LANES), bias)
        o_pats.append(o.reshape(b, d, ATT_HEADS, ld, LANES))
        stats.append(st.reshape(b, d, ld, LANES))
    x1, x1b = _merge_outproj(o_pats, stats, ga, ob, x, w_out_ab, ln_g[0:1], ln_b[0:1])
    y = _conv_inproj(x1b.reshape(b * s, dm), w_in_c, conv_w, seq=s)
    out = _outproj_ln(y, x1.reshape(b * s, dm), w_out_c, ln_g[1:2], ln_b[1:2])
    return out.reshape(b, s, dm)


def kernel(x_prompt, x_sample, rel_bias, w_in_ab, pool_w, pool_scale, w_out_ab, w_in_c, conv_w, w_out_c, ln_g, ln_b):
    assert DEPTH == 2 and w_in_ab.shape[0] == 1 and w_in_c.shape[0] == 1
    params = dict(
        biases=[_band_bias(rel_bias, d) for d in DILATIONS],
        w_ab=w_in_ab[0].astype(BF16),
        pool_w=pool_w[0].astype(BF16),
        pool_scale=pool_scale[0].reshape(1, POOL_WIDTH),
        w_out_ab=w_out_ab[0].astype(BF16),
        w_in_c=w_in_c[0].astype(BF16),
        conv_w=conv_w[0],
        w_out_c=w_out_c[0].astype(BF16),
        ln_g=ln_g,
        ln_b=ln_b,
    )
    return (_trunk(x_prompt, **params), _trunk(x_sample, **params))
```

```python
import functools
import math

import numpy as np
import jax
import jax.numpy as jnp
from jax import lax
from jax.experimental import pallas as pl
from jax.experimental.pallas import tpu as pltpu

D_MODEL = 2048
DEPTH = 2
ATT_HEADS = 16
HEAD_DIM = 128
ATT_WIDTH = ATT_HEADS * HEAD_DIM
DILATED_PATTERNS = ((128, 1), (512, 4), (2048, 16))
DILATIONS = tuple(d for _, d in DILATED_PATTERNS)
RADIUS = 64
POOL_WINDOWS = (2, 4, 8, 16)
POOL_WIDTH = D_MODEL // 2
POOL_GROUP = POOL_WIDTH // len(POOL_WINDOWS)
CONV_WIDTH = D_MODEL
REL_BUCKETS = 32
REL_MAX_DISTANCE = 1024
DN_ALPHA = (2 * DEPTH) ** 0.25
LN_EPS = 1e-5
NEG_INF = -1e30
LOG2E = math.log2(math.e)

LANES = 128
HALO = 16
ATT_Q_ROWS = 128
ATT_KEYS = ATT_Q_ROWS + 2 * RADIUS
ATT_UNITS = 4
ATT_TRIP_STEPS = 8
ATT_BLOCK_ROWS = 1024
ATT_MIN_STEP_ROWS = 512
PROJ_ROWS = 256
ROW_CHUNK = 128
VMEM_LIMIT = 56 * 1024 * 1024

BF16 = jnp.bfloat16
F32 = jnp.float32

assert all(w // (2 * d) == RADIUS for w, d in DILATED_PATTERNS)


def _t5_bucket(rel):
    nb = REL_BUCKETS // 2
    max_exact = nb // 2
    ret = np.where(rel > 0, nb, 0)
    n = np.abs(rel)
    n_safe = np.maximum(n, 1).astype(np.float64)
    large = max_exact + (np.log(n_safe / max_exact) / math.log(REL_MAX_DISTANCE / max_exact)
                         * (nb - max_exact)).astype(np.int64)
    large = np.minimum(large, nb - 1)
    return (ret + np.where(n < max_exact, n, large)).astype(np.int32)


def _band_bias(rel_bias, dilation):
    rel = np.arange(ATT_KEYS)[None, :] - RADIUS - np.arange(ATT_Q_ROWS)[:, None]
    bucket = jnp.asarray(_t5_bucket(rel * dilation).reshape(1, -1))
    onehot = (bucket == jnp.arange(REL_BUCKETS, dtype=jnp.int32)[:, None]).astype(F32)
    bias = jnp.dot(rel_bias.astype(F32).T, onehot, precision=lax.Precision.HIGHEST)
    bias = bias.reshape(1, ATT_HEADS, ATT_Q_ROWS, ATT_KEYS)
    key = np.arange(ATT_KEYS)[None, :]
    in_band = np.abs(rel) <= RADIUS
    after_start = key >= RADIUS
    before_end = key < ATT_Q_ROWS + RADIUS
    keep = np.stack([in_band, in_band & after_start, in_band & before_end, in_band & after_start & before_end])
    return jnp.where(jnp.asarray(keep[:, None]), bias * LOG2E, NEG_INF)


def _silu(x):
    return x / (1.0 + jnp.exp(-x))


def _layernorm(z, g, b):
    mu = jnp.mean(z, axis=-1, keepdims=True)
    zc = z - mu
    var = jnp.mean(zc * zc, axis=-1, keepdims=True)
    return zc * lax.rsqrt(var + LN_EPS) * g + b


def _qkvg_kernel(x_ref, w_ref, o1_ref, o4_ref, o16_ref, ga_ref, xb_ref, s0_ref, s1_ref, mid0_ref, mid1_ref,
                 *, tm, tn, nq):
    j = pl.program_id(2)
    chunk_bufs = ((s0_ref, mid0_ref), (s1_ref, mid1_ref))
    r4, r16 = PROJ_ROWS // 4, PROJ_ROWS // 16

    @pl.when(j == 0)
    def _():
        xb_ref[...] = x_ref[...].astype(BF16)

    def product(rc, cc):
        rows = slice(rc * PROJ_ROWS, (rc + 1) * PROJ_ROWS)
        cols = slice(cc * 2 * LANES, (cc + 1) * 2 * LANES)
        return jnp.dot(xb_ref[rows, :], w_ref[:, cols], preferred_element_type=F32)

    def emit_qkv(acc, rc, cc, s_ref, mid_ref):
        acc = acc * jnp.where(j < ATT_WIDTH // tn, LOG2E * HEAD_DIM ** -0.5, 1.0)
        for c in range(2):
            slab = 2 * cc + c
            blk = acc[:, c * LANES:(c + 1) * LANES]
            o1_ref[0, slab, rc * PROJ_ROWS:(rc + 1) * PROJ_ROWS, :] = blk.astype(BF16)
            s_ref[c] = blk
            for g4 in range(4):
                rows4 = s_ref[c, pl.ds(g4, r4, stride=4), :]
                o4_ref[g4, slab, rc * r4:(rc + 1) * r4, :] = rows4.astype(BF16)
                mid_ref[c, g4] = rows4
            for g4 in range(4):
                for k in range(4):
                    o16_ref[g4 + 4 * k, slab, rc * r16:(rc + 1) * r16, :] = (
                        mid_ref[c, g4, pl.ds(k, r16, stride=4), :].astype(BF16))

    def emit_gate(acc, rc, cc):
        ga_ref[rc * PROJ_ROWS:(rc + 1) * PROJ_ROWS, cc * 2 * LANES:(cc + 1) * 2 * LANES] = _silu(acc).astype(BF16)

    chunks = [(rc, cc) for cc in range(tn // (2 * LANES)) for rc in range(tm // PROJ_ROWS)]

    @pl.when(j < nq)
    def _():
        for n, (rc, cc) in enumerate(chunks):
            emit_qkv(product(rc, cc), rc, cc, *chunk_bufs[n % 2])

    @pl.when(j >= nq)
    def _():
        for rc, cc in chunks:
            emit_gate(product(rc, cc), rc, cc)


def _qkvg_proj(x, w, *, tm=1024, tn=1024):
    b, s, dm = x.shape
    nj = 4 * ATT_WIDTH // tn
    nq = 3 * ATT_WIDTH // tn
    qt = ATT_WIDTH // tn
    nslab = 3 * ATT_WIDTH // LANES
    outs = [jax.ShapeDtypeStruct((b, d, nslab, s // d, LANES), BF16) for d in DILATIONS]
    outs.append(jax.ShapeDtypeStruct((b, s, ATT_WIDTH), BF16))
    out_specs = [pl.BlockSpec((None, d, tn // LANES, tm // d, LANES),
                              lambda bi, i, j: (bi, 0, (jnp.minimum(j, nq - 1) + 2 * qt) % nq, i, 0)) for d in DILATIONS]
    out_specs.append(pl.BlockSpec((None, tm, tn), lambda bi, i, j: (bi, i, jnp.maximum(j - nq, 0))))
    return pl.pallas_call(
        functools.partial(_qkvg_kernel, tm=tm, tn=tn, nq=nq),
        grid=(b, s // tm, nj),
        in_specs=[pl.BlockSpec((None, tm, dm), lambda bi, i, j: (bi, i, 0)),
                  pl.BlockSpec((dm, tn), lambda bi, i, j: (0, j))],
        out_specs=out_specs,
        out_shape=outs,
        scratch_shapes=[pltpu.VMEM((tm, dm), BF16)]
        + [pltpu.VMEM((2, PROJ_ROWS, LANES), F32)] * 2
        + [pltpu.VMEM((2, 4, PROJ_ROWS // 4, LANES), F32)] * 2,
        compiler_params=pltpu.CompilerParams(
            dimension_semantics=("parallel", "parallel", "arbitrary"), vmem_limit_bytes=VMEM_LIMIT),
        name="qkvg_proj",
    )(x, w)


def _pool_kernel(xp_ref, x_ref, xn_ref, wu_ref, wg_ref, pw_ref, ps_ref, o_ref, xb_ref, u_ref, *, tm, seq):
    i = pl.program_id(0)
    tiles_per_seq = seq // tm
    keep_prev = jnp.where((i % tiles_per_seq) == 0, 0.0, 1.0)
    keep_next = jnp.where((i % tiles_per_seq) == tiles_per_seq - 1, 0.0, 1.0)
    nch = tm // ROW_CHUNK

    xb_ref[0:HALO] = xp_ref[...].astype(BF16)
    xb_ref[HALO:HALO + tm] = x_ref[...].astype(BF16)
    xb_ref[HALO + tm:] = xn_ref[...].astype(BF16)

    def pool_input(r):
        lo = HALO + r * ROW_CHUNK - (HALO if r == 0 else 0)
        hi = HALO + (r + 1) * ROW_CHUNK + (HALO if r == nch - 1 else 0)
        u_ref[lo:hi] = jnp.dot(xb_ref[lo:hi], wu_ref[...], preferred_element_type=F32)
        if r == 0:
            u_ref[0:HALO] = u_ref[0:HALO] * keep_prev
        if r == nch - 1:
            u_ref[HALO + tm:] = u_ref[HALO + tm:] * keep_next

    def pooled_out(r):
        base = HALO + r * ROW_CHUNK
        rows = slice(r * ROW_CHUNK, (r + 1) * ROW_CHUNK)
        gate = jnp.dot(xb_ref[base:base + ROW_CHUNK], wg_ref[...], preferred_element_type=F32)
        pos = ((i % tiles_per_seq) * tm + r * ROW_CHUNK
               + lax.broadcasted_iota(jnp.int32, (ROW_CHUNK, POOL_GROUP), 0))
        for gi, w in enumerate(POOL_WINDOWS):
            cols = slice(gi * POOL_GROUP, (gi + 1) * POOL_GROUP)
            tot = u_ref[pl.ds(base - w // 2, ROW_CHUNK), cols]
            for off in range(-w // 2 + 1, w // 2):
                tot = tot + u_ref[pl.ds(base + off, ROW_CHUNK), cols]
            lo = jnp.maximum(pos - w // 2, 0)
            hi = jnp.minimum(pos + w // 2 - 1, seq - 1)
            cnt = (hi - lo + 1).astype(F32)
            pooled = tot / cnt - u_ref[pl.ds(base, ROW_CHUNK), cols]
            ob = jnp.dot(pooled.astype(BF16), pw_ref[gi], preferred_element_type=F32)
            o_ref[rows, cols] = (ob * ps_ref[:, cols] * _silu(gate[:, cols])).astype(BF16)

    pool_input(0)
    for r in range(nch):
        if r + 1 < nch:
            pool_input(r + 1)
        pooled_out(r)


def _pool_branch(x2, w_ab, pool_w, pool_scale, *, seq, tm=512):
    m, dm = x2.shape
    ub = 4 * ATT_WIDTH // POOL_WIDTH
    hb = tm // HALO
    nhb = m // HALO
    return pl.pallas_call(
        functools.partial(_pool_kernel, tm=tm, seq=seq),
        grid=(m // tm,),
        in_specs=[pl.BlockSpec((HALO, dm), lambda i: (jnp.maximum(i * hb - 1, 0), 0)),
                  pl.BlockSpec((tm, dm), lambda i: (i, 0)),
                  pl.BlockSpec((HALO, dm), lambda i: (jnp.minimum((i + 1) * hb, nhb - 1), 0)),
                  pl.BlockSpec((dm, POOL_WIDTH), lambda i: (0, ub)),
                  pl.BlockSpec((dm, POOL_WIDTH), lambda i: (0, ub + 1)),
                  pl.BlockSpec((len(POOL_WINDOWS), POOL_GROUP, POOL_GROUP), lambda i: (0, 0, 0)),
                  pl.BlockSpec((1, POOL_WIDTH), lambda i: (0, 0))],
        out_specs=pl.BlockSpec((tm, POOL_WIDTH), lambda i: (i, 0)),
        out_shape=jax.ShapeDtypeStruct((m, POOL_WIDTH), BF16),
        scratch_shapes=[pltpu.VMEM((tm + 2 * HALO, dm), BF16), pltpu.VMEM((tm + 2 * HALO, POOL_WIDTH), F32)],
        compiler_params=pltpu.CompilerParams(dimension_semantics=("parallel",), vmem_limit_bytes=VMEM_LIMIT),
        name="pool_branch",
    )(x2, x2, x2, w_ab, w_ab, pool_w, pool_scale)


def _attn_kernel(q_ref, kvp_ref, kvm_ref, kvn_ref, bias_ref, o_ref, st_ref, s0_ref, s1_ref, *, qb, nblk, ns):
    n = pl.program_id(1)
    na = qb // ATT_Q_ROWS
    ng = max(1, na // ATT_UNITS)
    apg = na // ng
    hpi = max(1, ATT_UNITS // na)
    steps = (ATT_HEADS // hpi) * ng
    trip = min(ATT_TRIP_STEPS, steps)
    assert steps % trip == 0 and trip % (2 * ng) == 0
    lane = lax.broadcasted_iota(jnp.int32, (ATT_Q_ROWS, LANES), 1)
    st_ref[...] = jnp.broadcast_to(
        jnp.where((lane[:1] >= ATT_HEADS) & (lane[:1] < 2 * ATT_HEADS), 1.0, 0.0), st_ref.shape)
    first = (n == 0).astype(jnp.int32)
    last = (n == nblk - 1).astype(jnp.int32)
    s_refs = (s0_ref, s1_ref)
    ones = jnp.ones((ATT_KEYS, LANES), BF16)

    def window(si, slab, a):
        lo = a * ATT_Q_ROWS - RADIUS
        hi = lo + ATT_KEYS
        parts = []
        if lo < 0:
            parts.append(kvp_ref[si, slab])
        parts.append(kvm_ref[si, slab, max(lo, 0):min(hi, qb), :])
        if hi > qb:
            parts.append(kvn_ref[si, slab])
        return parts[0] if len(parts) == 1 else jnp.concatenate(parts, axis=0)

    def scores(si, step, grp, slot):
        for j in range(hpi):
            h = (step // ng) * hpi + j
            for ai in range(apg):
                a = grp * apg + ai
                q = q_ref[si, h, a * ATT_Q_ROWS:(a + 1) * ATT_Q_ROWS, :]
                s_refs[slot][j * apg + ai] = lax.dot_general(
                    q, window(si, h, a), (((1,), (1,)), ((), ())), preferred_element_type=F32)

    def softmax_pv(si, step, grp, slot):
        for j in range(hpi):
            h = (step // ng) * hpi + j
            for ai in range(apg):
                a = grp * apg + ai
                rows = slice(a * ATT_Q_ROWS, (a + 1) * ATT_Q_ROWS)
                variant = (first if a == 0 else 0) + (2 * last if a == na - 1 else 0)
                s = s_refs[slot][j * apg + ai] + bias_ref[variant, h]
                mx = jnp.max(s, axis=-1, keepdims=True)
                p = jnp.exp2(s - mx)
                v = window(si, ATT_HEADS + h, a)
                pv = jnp.dot(p.astype(BF16), jnp.concatenate([v, ones], axis=1), preferred_element_type=F32)
                o_ref[si, h, rows, :] = pv[:, :HEAD_DIM].astype(BF16)
                st_ref[si, rows, :] = jnp.where(
                    lane == h, mx, jnp.where(lane == h + ATT_HEADS, pv[:, HEAD_DIM:], st_ref[si, rows, :]))

    def sequence(si, carry):
        scores(si, 0, 0, 0)

        def body(tt, c):
            for u in range(trip):
                step = trip * tt + u
                scores(si, (step + 1) % steps, (u + 1) % ng, (u + 1) % 2)
                softmax_pv(si, step, u % ng, u % 2)
            return c

        lax.fori_loop(0, steps // trip, body, 0)
        return carry

    if ns == 1:
        sequence(0, 0)
    else:
        lax.fori_loop(0, ns, sequence, 0)


def _banded_attention(hd, bias):
    nseq, _, length, _ = hd.shape
    qb = min(ATT_BLOCK_ROWS, length)
    nblk = length // qb
    ns = max(1, ATT_MIN_STEP_ROWS // length)
    hb = qb // RADIUS
    nh = length // RADIUS
    units = max(1, ATT_UNITS // (qb // ATT_Q_ROWS)) * min(ATT_UNITS, qb // ATT_Q_ROWS)
    return pl.pallas_call(
        functools.partial(_attn_kernel, qb=qb, nblk=nblk, ns=ns),
        grid=(nseq // ns, nblk),
        in_specs=[pl.BlockSpec((ns, ATT_HEADS, qb, LANES), lambda s, n: (s, 2, n, 0)),
                  pl.BlockSpec((ns, 2 * ATT_HEADS, RADIUS, LANES),
                               lambda s, n: (s, 0, jnp.maximum(n * hb - 1, 0), 0)),
                  pl.BlockSpec((ns, 2 * ATT_HEADS, qb, LANES), lambda s, n: (s, 0, n, 0)),
                  pl.BlockSpec((ns, 2 * ATT_HEADS, RADIUS, LANES),
                               lambda s, n: (s, 0, jnp.minimum((n + 1) * hb, nh - 1), 0)),
                  pl.BlockSpec(bias.shape, lambda s, n: (0, 0, 0, 0), pipeline_mode=pl.Buffered(1))],
        out_specs=[pl.BlockSpec((ns, ATT_HEADS, qb, LANES), lambda s, n: (s, 0, n, 0)),
                   pl.BlockSpec((ns, qb, LANES), lambda s, n: (s, n, 0))],
        out_shape=[jax.ShapeDtypeStruct((nseq, ATT_HEADS, length, LANES), BF16),
                   jax.ShapeDtypeStruct((nseq, length, LANES), F32)],
        scratch_shapes=[pltpu.VMEM((units, ATT_Q_ROWS, ATT_KEYS), F32)] * 2,
        compiler_params=pltpu.CompilerParams(
            dimension_semantics=("parallel", "parallel"), vmem_limit_bytes=VMEM_LIMIT),
        name="banded_attention",
    )(hd, hd, hd, hd, bias)


def _merge_kernel(o1_ref, o4_ref, o16_ref, s1_ref, s4_ref, s16_ref,
                  ga_ref, ob_ref, x_ref, w_ref, g_ref, b_ref, out_ref, outb_ref,
                  y0_ref, y1_ref, st_ref, r_ref, f_ref, *, tm, total):
    t = pl.program_id(0)
    ys = (y0_ref, y1_ref)
    ncol = w_ref.shape[1] // (2 * LANES)
    hpp = ATT_HEADS // ncol

    def merge_weights():
        st_ref[0] = s1_ref[0]
        for p, (d, sr) in enumerate(((4, s4_ref), (16, s16_ref)), start=1):
            for g in range(d):
                st_ref[p, pl.ds(g, tm // d, stride=d), :] = sr[g]
        ms = [st_ref[p] for p in range(3)]
        mx = jnp.maximum(jnp.maximum(ms[0], ms[1]), ms[2])
        es = [jnp.exp2(m - mx) for m in ms]
        den = sum(e * pltpu.roll(m, LANES - ATT_HEADS, axis=1) for e, m in zip(es, ms))
        head_lane = lax.broadcasted_iota(jnp.int32, den.shape, 1) < ATT_HEADS
        den = jnp.where(head_lane, den, 1.0)
        for p in range(3):
            st_ref[p] = es[p] / den

    def merge_head(dst, h):
        for p, (d, o_ref) in enumerate(((4, o4_ref), (16, o16_ref))):
            for g in range(d):
                r_ref[p, pl.ds(g, tm // d, stride=d), :] = o_ref[g, h].astype(F32)
        oa = (st_ref[0, :, h:h + 1] * o1_ref[0, h].astype(F32)
              + st_ref[1, :, h:h + 1] * r_ref[0]
              + st_ref[2, :, h:h + 1] * r_ref[1])
        cols = slice(h * HEAD_DIM, (h + 1) * HEAD_DIM)
        dst[:, cols] = (oa * ga_ref[:, cols].astype(F32)).astype(BF16)

    def product(src, k):
        cols = slice(k * 2 * LANES, (k + 1) * 2 * LANES)
        f_ref[:, cols] = jnp.dot(src[...], w_ref[:, cols], preferred_element_type=F32)
        if k == ncol - 1:
            for r in range(0, tm, ROW_CHUNK):
                rows = slice(r, r + ROW_CHUNK)
                y = _layernorm(DN_ALPHA * x_ref[rows, :] + f_ref[rows, :], g_ref[...], b_ref[...])
                out_ref[rows, :] = y
                outb_ref[rows, :] = y.astype(BF16)

    @pl.when(t == 0)
    def _():
        merge_weights()
        for h in range(ATT_HEADS):
            merge_head(ys[0], h)
        ys[0][:, ATT_WIDTH:] = ob_ref[...]

    @pl.when(t == total)
    def _():
        for k in range(ncol):
            product(ys[(total - 1) % 2], k)

    for parity in (0, 1):
        @pl.when((t > 0) & (t < total) & (t % 2 == parity))
        def _(parity=parity):
            merge_weights()
            for k in range(ncol):
                product(ys[1 - parity], k)
                for h in range(k * hpp, (k + 1) * hpp):
                    merge_head(ys[parity], h)
            ys[parity][:, ATT_WIDTH:] = ob_ref[...]


def _merge_outproj(o_pats, stats, ga, ob, x, w_out, ln_g, ln_b, *, tm=256):
    b, s, dm = x.shape
    nt = s // tm
    total = b * nt
    ga = ga.reshape(b, s, ATT_WIDTH)
    ob = ob.reshape(b, s, POOL_WIDTH)

    def cur(t):
        t = jnp.minimum(t, total - 1)
        return t // nt, t % nt

    def prev(t):
        t = jnp.maximum(t - 1, 0)
        return t // nt, t % nt

    def o_map(t):
        bi, i = cur(t)
        return bi, 0, 0, i, 0

    def st_map(t):
        bi, i = cur(t)
        return bi, 0, i, 0

    def cur_map(t):
        bi, i = cur(t)
        return bi, i, 0

    def prev_map(t):
        bi, i = prev(t)
        return bi, i, 0

    o_specs = [pl.BlockSpec((None, d, ATT_HEADS, tm // d, LANES), o_map) for d in DILATIONS]
    st_specs = [pl.BlockSpec((None, d, tm // d, LANES), st_map) for d in DILATIONS]
    const2 = lambda t: (0, 0)
    args = list(o_pats) + list(stats)
    return pl.pallas_call(
        functools.partial(_merge_kernel, tm=tm, total=total),
        grid=(total + 1,),
        in_specs=o_specs + st_specs + [
            pl.BlockSpec((None, tm, ATT_WIDTH), cur_map),
            pl.BlockSpec((None, tm, POOL_WIDTH), cur_map),
            pl.BlockSpec((None, tm, dm), prev_map),
            pl.BlockSpec(w_out.shape, const2, pipeline_mode=pl.Buffered(1)),
            pl.BlockSpec((1, dm), const2),
            pl.BlockSpec((1, dm), const2)],
        out_specs=[pl.BlockSpec((None, tm, dm), prev_map), pl.BlockSpec((None, tm, dm), prev_map)],
        out_shape=[jax.ShapeDtypeStruct((b, s, dm), F32), jax.ShapeDtypeStruct((b, s, dm), BF16)],
        scratch_shapes=[pltpu.VMEM((tm, ATT_WIDTH + POOL_WIDTH), BF16),
                        pltpu.VMEM((tm, ATT_WIDTH + POOL_WIDTH), BF16),
                        pltpu.VMEM((3, tm, LANES), F32),
                        pltpu.VMEM((2, tm, LANES), F32),
                        pltpu.VMEM((tm, dm), F32)],
        compiler_params=pltpu.CompilerParams(dimension_semantics=("arbitrary",), vmem_limit_bytes=VMEM_LIMIT),
        name="merge_outproj",
    )(*args, ga, ob, x, w_out, ln_g, ln_b)


def _conv_in_kernel(xp_ref, x_ref, xn_ref, wb_ref, wc_ref, wv_ref, wg_ref, cw_ref, o_ref, cv_ref, *, tm, seq):
    i = pl.program_id(0)
    tiles_per_seq = seq // tm
    keep_prev = jnp.where((i % tiles_per_seq) == 0, 0.0, 1.0)
    keep_next = jnp.where((i % tiles_per_seq) == tiles_per_seq - 1, 0.0, 1.0)
    nch = tm // ROW_CHUNK

    def conv_input(r):
        parts = [x_ref[r * ROW_CHUNK:(r + 1) * ROW_CHUNK, :]]
        if r == 0:
            parts.insert(0, xp_ref[...])
        if r == nch - 1:
            parts.append(xn_ref[...])
        xa = parts[0] if len(parts) == 1 else jnp.concatenate(parts, axis=0)
        lo = HALO + r * ROW_CHUNK - (HALO if r == 0 else 0)
        cv_ref[lo:lo + xa.shape[0]] = (jnp.dot(xa, wc_ref[...], preferred_element_type=F32)
                                       * jnp.dot(xa, wv_ref[...], preferred_element_type=F32))
        if r == 0:
            cv_ref[0:HALO] = cv_ref[0:HALO] * keep_prev
        if r == nch - 1:
            cv_ref[HALO + tm:] = cv_ref[HALO + tm:] * keep_next

    def gated_conv(r):
        base = HALO + r * ROW_CHUNK
        xm = x_ref[r * ROW_CHUNK:(r + 1) * ROW_CHUNK, :]
        gb = jnp.dot(xm, wb_ref[...], preferred_element_type=F32)
        gate = jnp.dot(xm, wg_ref[...], preferred_element_type=F32)
        conv = (cv_ref[pl.ds(base - 1, ROW_CHUNK), :] * cw_ref[0:1, :]
                + cv_ref[pl.ds(base, ROW_CHUNK), :] * cw_ref[1:2, :]
                + cv_ref[pl.ds(base + 1, ROW_CHUNK), :] * cw_ref[2:3, :])
        o_ref[r * ROW_CHUNK:(r + 1) * ROW_CHUNK, :] = (gb * conv * _silu(gate)).astype(BF16)

    conv_input(0)
    for r in range(nch):
        if r + 1 < nch:
            conv_input(r + 1)
        gated_conv(r)


def _conv_inproj(xb, w_in, conv_w, *, seq, tm=1024, tc=1024):
    m, dm = xb.shape
    hb = tm // HALO
    nhb = m // HALO
    nc = CONV_WIDTH // tc

    def wspec(part):
        return pl.BlockSpec((dm, tc), lambda i, c: (0, part * nc + c))

    return pl.pallas_call(
        functools.partial(_conv_in_kernel, tm=tm, seq=seq),
        grid=(m // tm, nc),
        in_specs=[pl.BlockSpec((HALO, dm), lambda i, c: (jnp.maximum(i * hb - 1, 0), 0)),
                  pl.BlockSpec((tm, dm), lambda i, c: (i, 0)),
                  pl.BlockSpec((HALO, dm), lambda i, c: (jnp.minimum((i + 1) * hb, nhb - 1), 0)),
                  wspec(0), wspec(1), wspec(2), wspec(3),
                  pl.BlockSpec((3, tc), lambda i, c: (0, c))],
        out_specs=pl.BlockSpec((tm, tc), lambda i, c: (i, c)),
        out_shape=jax.ShapeDtypeStruct((m, CONV_WIDTH), BF16),
        scratch_shapes=[pltpu.VMEM((tm + 2 * HALO, tc), F32)],
        compiler_params=pltpu.CompilerParams(
            dimension_semantics=("parallel", "parallel"), vmem_limit_bytes=VMEM_LIMIT),
        name="conv_inproj",
    )(xb, xb, xb, w_in, w_in, w_in, w_in, conv_w)


def _outproj_ln_kernel(y_ref, x_ref, w_ref, g_ref, b_ref, o_ref):
    for r in range(0, y_ref.shape[0], ROW_CHUNK):
        rows = slice(r, r + ROW_CHUNK)
        f = jnp.dot(y_ref[rows, :], w_ref[...], preferred_element_type=F32)
        o_ref[rows, :] = _layernorm(DN_ALPHA * x_ref[rows, :] + f, g_ref[...], b_ref[...])


def _outproj_ln(y, x2, w_out, ln_g, ln_b, *, tm=512):
    m, dm = x2.shape
    const2 = lambda i: (0, 0)
    return pl.pallas_call(
        _outproj_ln_kernel,
        grid=(m // tm,),
        in_specs=[pl.BlockSpec((tm, y.shape[1]), lambda i: (i, 0)),
                  pl.BlockSpec((tm, dm), lambda i: (i, 0)),
                  pl.BlockSpec(w_out.shape, const2, pipeline_mode=pl.Buffered(1)),
                  pl.BlockSpec((1, dm), const2),
                  pl.BlockSpec((1, dm), const2)],
        out_specs=pl.BlockSpec((tm, dm), lambda i: (i, 0)),
        out_shape=jax.ShapeDtypeStruct((m, dm), F32),
        compiler_params=pltpu.CompilerParams(dimension_semantics=("parallel",), vmem_limit_bytes=VMEM_LIMIT),
        name="outproj_ln",
    )(y, x2, w_out, ln_g, ln_b)


def _trunk(x, biases, w_ab, pool_w, pool_scale, w_out_ab, w_in_c, conv_w, w_out_c, ln_g, ln_b):
    b, s, dm = x.shape
    x2 = x.reshape(b * s, dm)
    *hds, ga = _qkvg_proj(x, w_ab)
    ob = _pool_branch(x2, w_ab, pool_w, pool_scale, seq=s)
    o_pats, stats = [], []
    for d, hd, bias in zip(DILATIONS, hds, biases):
        ld = s // d
        o, st = _banded_attention(hd.reshape(b * d, 3 * ATT_HEADS, ld, LANES), bias)
        o_pats.append(o.reshape(b, d, ATT_HEADS, ld, LANES))
        stats.append(st.reshape(b, d, ld, LANES))
    x1, x1b = _merge_outproj(o_pats, stats, ga, ob, x, w_out_ab, ln_g[0:1], ln_b[0:1])
    y = _conv_inproj(x1b.reshape(b * s, dm), w_in_c, conv_w, seq=s)
    out = _outproj_ln(y, x1.reshape(b * s, dm), w_out_c, ln_g[1:2], ln_b[1:2])
    return out.reshape(b, s, dm)


def kernel(x_prompt, x_sample, rel_bias, w_in_ab, pool_w, pool_scale, w_out_ab, w_in_c, conv_w, w_out_c, ln_g, ln_b):
    assert DEPTH == 2 and w_in_ab.shape[0] == 1 and w_in_c.shape[0] == 1
    params = dict(
        biases=[_band_bias(rel_bias, d) for d in DILATIONS],
        w_ab=w_in_ab[0].astype(BF16),
        pool_w=pool_w[0].astype(BF16),
        pool_scale=pool_scale[0].reshape(1, POOL_WIDTH),
        w_out_ab=w_out_ab[0].astype(BF16),
        w_in_c=w_in_c[0].astype(BF16),
        conv_w=conv_w[0],
        w_out_c=w_out_c[0].astype(BF16),
        ln_g=ln_g,
        ln_b=ln_b,
    )
    return (_trunk(x_prompt, **params), _trunk(x_sample, **params))
```

```python
import functools
import math

import numpy as np
import jax
import jax.numpy as jnp
from jax import lax
from jax.experimental import pallas as pl
from jax.experimental.pallas import tpu as pltpu

D_MODEL = 2048
DEPTH = 2
ATT_HEADS = 16
HEAD_DIM = 128
ATT_WIDTH = ATT_HEADS * HEAD_DIM
DILATED_PATTERNS = ((128, 1), (512, 4), (2048, 16))
DILATIONS = tuple(d for _, d in DILATED_PATTERNS)
RADIUS = 64
POOL_WINDOWS = (2, 4, 8, 16)
POOL_WIDTH = D_MODEL // 2
POOL_GROUP = POOL_WIDTH // len(POOL_WINDOWS)
CONV_WIDTH = D_MODEL
REL_BUCKETS = 32
REL_MAX_DISTANCE = 1024
DN_ALPHA = (2 * DEPTH) ** 0.25
LN_EPS = 1e-5
NEG_INF = -1e30
LOG2E = math.log2(math.e)

LANES = 128
HALO = 16
ATT_Q_ROWS = 128
ATT_KEYS = ATT_Q_ROWS + 2 * RADIUS
ATT_UNITS = 4
ATT_TRIP_STEPS = 16
ATT_BLOCK_ROWS = 1024
ATT_MIN_STEP_ROWS = 512
PROJ_ROWS = 256
ROW_CHUNK = 128
VMEM_LIMIT = 56 * 1024 * 1024

BF16 = jnp.bfloat16
F32 = jnp.float32

assert all(w // (2 * d) == RADIUS for w, d in DILATED_PATTERNS)


def _t5_bucket(rel):
    nb = REL_BUCKETS // 2
    max_exact = nb // 2
    ret = np.where(rel > 0, nb, 0)
    n = np.abs(rel)
    n_safe = np.maximum(n, 1).astype(np.float64)
    large = max_exact + (np.log(n_safe / max_exact) / math.log(REL_MAX_DISTANCE / max_exact)
                         * (nb - max_exact)).astype(np.int64)
    large = np.minimum(large, nb - 1)
    return (ret + np.where(n < max_exact, n, large)).astype(np.int32)


def _band_bias(rel_bias, dilation):
    rel = np.arange(ATT_KEYS)[None, :] - RADIUS - np.arange(ATT_Q_ROWS)[:, None]
    bucket = jnp.asarray(_t5_bucket(rel * dilation).reshape(1, -1))
    onehot = (bucket == jnp.arange(REL_BUCKETS, dtype=jnp.int32)[:, None]).astype(F32)
    bias = jnp.dot(rel_bias.astype(F32).T, onehot, precision=lax.Precision.HIGHEST)
    bias = bias.reshape(1, ATT_HEADS, ATT_Q_ROWS, ATT_KEYS)
    key = np.arange(ATT_KEYS)[None, :]
    in_band = np.abs(rel) <= RADIUS
    after_start = key >= RADIUS
    before_end = key < ATT_Q_ROWS + RADIUS
    keep = np.stack([in_band, in_band & after_start, in_band & before_end, in_band & after_start & before_end])
    return jnp.where(jnp.asarray(keep[:, None]), bias * LOG2E, NEG_INF)


def _silu(x):
    return x / (1.0 + jnp.exp(-x))


def _layernorm(z, g, b):
    mu = jnp.mean(z, axis=-1, keepdims=True)
    zc = z - mu
    var = jnp.mean(zc * zc, axis=-1, keepdims=True)
    return zc * lax.rsqrt(var + LN_EPS) * g + b


def _qkvg_kernel(x_ref, w_ref, o1_ref, o4_ref, o16_ref, ga_ref, xb_ref, s0_ref, s1_ref, mid0_ref, mid1_ref,
                 *, tm, tn, nq):
    j = pl.program_id(2)
    chunk_bufs = ((s0_ref, mid0_ref), (s1_ref, mid1_ref))
    r4, r16 = PROJ_ROWS // 4, PROJ_ROWS // 16

    @pl.when(j == 0)
    def _():
        xb_ref[...] = x_ref[...].astype(BF16)

    def product(rc, cc):
        rows = slice(rc * PROJ_ROWS, (rc + 1) * PROJ_ROWS)
        cols = slice(cc * 2 * LANES, (cc + 1) * 2 * LANES)
        return jnp.dot(xb_ref[rows, :], w_ref[:, cols], preferred_element_type=F32)

    def emit_qkv(acc, rc, cc, s_ref, mid_ref):
        acc = acc * jnp.where(j < ATT_WIDTH // tn, LOG2E * HEAD_DIM ** -0.5, 1.0)
        for c in range(2):
            slab = 2 * cc + c
            blk = acc[:, c * LANES:(c + 1) * LANES]
            o1_ref[0, slab, rc * PROJ_ROWS:(rc + 1) * PROJ_ROWS, :] = blk.astype(BF16)
            s_ref[c] = blk
            for g4 in range(4):
                rows4 = s_ref[c, pl.ds(g4, r4, stride=4), :]
                o4_ref[g4, slab, rc * r4:(rc + 1) * r4, :] = rows4.astype(BF16)
                mid_ref[c, g4] = rows4
            for g4 in range(4):
                for k in range(4):
                    o16_ref[g4 + 4 * k, slab, rc * r16:(rc + 1) * r16, :] = (
                        mid_ref[c, g4, pl.ds(k, r16, stride=4), :].astype(BF16))

    def emit_gate(acc, rc, cc):
        ga_ref[rc * PROJ_ROWS:(rc + 1) * PROJ_ROWS, cc * 2 * LANES:(cc + 1) * 2 * LANES] = _silu(acc).astype(BF16)

    chunks = [(rc, cc) for cc in range(tn // (2 * LANES)) for rc in range(tm // PROJ_ROWS)]

    @pl.when(j < nq)
    def _():
        for n, (rc, cc) in enumerate(chunks):
            emit_qkv(product(rc, cc), rc, cc, *chunk_bufs[n % 2])

    @pl.when(j >= nq)
    def _():
        for rc, cc in chunks:
            emit_gate(product(rc, cc), rc, cc)


def _qkvg_proj(x, w, *, tm=1024, tn=1024):
    b, s, dm = x.shape
    nj = 4 * ATT_WIDTH // tn
    nq = 3 * ATT_WIDTH // tn
    qt = ATT_WIDTH // tn
    nslab = 3 * ATT_WIDTH // LANES
    outs = [jax.ShapeDtypeStruct((b, d, nslab, s // d, LANES), BF16) for d in DILATIONS]
    outs.append(jax.ShapeDtypeStruct((b, s, ATT_WIDTH), BF16))
    out_specs = [pl.BlockSpec((None, d, tn // LANES, tm // d, LANES),
                              lambda bi, i, j: (bi, 0, (jnp.minimum(j, nq - 1) + 2 * qt) % nq, i, 0)) for d in DILATIONS]
    out_specs.append(pl.BlockSpec((None, tm, tn), lambda bi, i, j: (bi, i, jnp.maximum(j - nq, 0))))
    return pl.pallas_call(
        functools.partial(_qkvg_kernel, tm=tm, tn=tn, nq=nq),
        grid=(b, s // tm, nj),
        in_specs=[pl.BlockSpec((None, tm, dm), lambda bi, i, j: (bi, i, 0)),
                  pl.BlockSpec((dm, tn), lambda bi, i, j: (0, j))],
        out_specs=out_specs,
        out_shape=outs,
        scratch_shapes=[pltpu.VMEM((tm, dm), BF16)]
        + [pltpu.VMEM((2, PROJ_ROWS, LANES), F32)] * 2
        + [pltpu.VMEM((2, 4, PROJ_ROWS // 4, LANES), F32)] * 2,
        compiler_params=pltpu.CompilerParams(
            dimension_semantics=("parallel", "parallel", "arbitrary"), vmem_limit_bytes=VMEM_LIMIT),
        name="qkvg_proj",
    )(x, w)


def _pool_kernel(xp_ref, x_ref, xn_ref, wu_ref, wg_ref, pw_ref, ps_ref, o_ref, xb_ref, u_ref, *, tm, seq):
    i = pl.program_id(0)
    tiles_per_seq = seq // tm
    keep_prev = jnp.where((i % tiles_per_seq) == 0, 0.0, 1.0)
    keep_next = jnp.where((i % tiles_per_seq) == tiles_per_seq - 1, 0.0, 1.0)
    nch = tm // ROW_CHUNK

    xb_ref[0:HALO] = xp_ref[...].astype(BF16)
    xb_ref[HALO:HALO + tm] = x_ref[...].astype(BF16)
    xb_ref[HALO + tm:] = xn_ref[...].astype(BF16)

    def pool_input(r):
        lo = HALO + r * ROW_CHUNK - (HALO if r == 0 else 0)
        hi = HALO + (r + 1) * ROW_CHUNK + (HALO if r == nch - 1 else 0)
        u_ref[lo:hi] = jnp.dot(xb_ref[lo:hi], wu_ref[...], preferred_element_type=F32)
        if r == 0:
            u_ref[0:HALO] = u_ref[0:HALO] * keep_prev
        if r == nch - 1:
            u_ref[HALO + tm:] = u_ref[HALO + tm:] * keep_next

    def pooled_out(r):
        base = HALO + r * ROW_CHUNK
        rows = slice(r * ROW_CHUNK, (r + 1) * ROW_CHUNK)
        gate = jnp.dot(xb_ref[base:base + ROW_CHUNK], wg_ref[...], preferred_element_type=F32)
        pos = ((i % tiles_per_seq) * tm + r * ROW_CHUNK
               + lax.broadcasted_iota(jnp.int32, (ROW_CHUNK, POOL_GROUP), 0))
        for gi, w in enumerate(POOL_WINDOWS):
            cols = slice(gi * POOL_GROUP, (gi + 1) * POOL_GROUP)
            tot = u_ref[pl.ds(base - w // 2, ROW_CHUNK), cols]
            for off in range(-w // 2 + 1, w // 2):
                tot = tot + u_ref[pl.ds(base + off, ROW_CHUNK), cols]
            lo = jnp.maximum(pos - w // 2, 0)
            hi = jnp.minimum(pos + w // 2 - 1, seq - 1)
            cnt = (hi - lo + 1).astype(F32)
            pooled = tot / cnt - u_ref[pl.ds(base, ROW_CHUNK), cols]
            ob = jnp.dot(pooled.astype(BF16), pw_ref[gi], preferred_element_type=F32)
            o_ref[rows, cols] = (ob * ps_ref[:, cols] * _silu(gate[:, cols])).astype(BF16)

    pool_input(0)
    for r in range(nch):
        if r + 1 < nch:
            pool_input(r + 1)
        pooled_out(r)


def _pool_branch(x2, w_ab, pool_w, pool_scale, *, seq, tm=512):
    m, dm = x2.shape
    ub = 4 * ATT_WIDTH // POOL_WIDTH
    hb = tm // HALO
    nhb = m // HALO
    return pl.pallas_call(
        functools.partial(_pool_kernel, tm=tm, seq=seq),
        grid=(m // tm,),
        in_specs=[pl.BlockSpec((HALO, dm), lambda i: (jnp.maximum(i * hb - 1, 0), 0)),
                  pl.BlockSpec((tm, dm), lambda i: (i, 0)),
                  pl.BlockSpec((HALO, dm), lambda i: (jnp.minimum((i + 1) * hb, nhb - 1), 0)),
                  pl.BlockSpec((dm, POOL_WIDTH), lambda i: (0, ub)),
                  pl.BlockSpec((dm, POOL_WIDTH), lambda i: (0, ub + 1)),
                  pl.BlockSpec((len(POOL_WINDOWS), POOL_GROUP, POOL_GROUP), lambda i: (0, 0, 0)),
                  pl.BlockSpec((1, POOL_WIDTH), lambda i: (0, 0))],
        out_specs=pl.BlockSpec((tm, POOL_WIDTH), lambda i: (i, 0)),
        out_shape=jax.ShapeDtypeStruct((m, POOL_WIDTH), BF16),
        scratch_shapes=[pltpu.VMEM((tm + 2 * HALO, dm), BF16), pltpu.VMEM((tm + 2 * HALO, POOL_WIDTH), F32)],
        compiler_params=pltpu.CompilerParams(dimension_semantics=("parallel",), vmem_limit_bytes=VMEM_LIMIT),
        name="pool_branch",
    )(x2, x2, x2, w_ab, w_ab, pool_w, pool_scale)


def _attn_kernel(q_ref, kvp_ref, kvm_ref, kvn_ref, bias_ref, o_ref, st_ref, s0_ref, s1_ref, *, qb, nblk, ns):
    n = pl.program_id(1)
    na = qb // ATT_Q_ROWS
    ng = max(1, na // ATT_UNITS)
    apg = na // ng
    hpi = max(1, ATT_UNITS // na)
    steps = (ATT_HEADS // hpi) * ng
    trip = min(ATT_TRIP_STEPS, steps)
    assert steps % trip == 0 and trip % (2 * ng) == 0
    lane = lax.broadcasted_iota(jnp.int32, (ATT_Q_ROWS, LANES), 1)
    st_ref[...] = jnp.broadcast_to(
        jnp.where((lane[:1] >= ATT_HEADS) & (lane[:1] < 2 * ATT_HEADS), 1.0, 0.0), st_ref.shape)
    first = (n == 0).astype(jnp.int32)
    last = (n == nblk - 1).astype(jnp.int32)
    s_refs = (s0_ref, s1_ref)
    ones = jnp.ones((ATT_KEYS, LANES), BF16)

    def window(si, slab, a):
        lo = a * ATT_Q_ROWS - RADIUS
        hi = lo + ATT_KEYS
        parts = []
        if lo < 0:
            parts.append(kvp_ref[si, slab])
        parts.append(kvm_ref[si, slab, max(lo, 0):min(hi, qb), :])
        if hi > qb:
            parts.append(kvn_ref[si, slab])
        return parts[0] if len(parts) == 1 else jnp.concatenate(parts, axis=0)

    def scores(si, step, grp, slot):
        for j in range(hpi):
            h = (step // ng) * hpi + j
            for ai in range(apg):
                a = grp * apg + ai
                q = q_ref[si, h, a * ATT_Q_ROWS:(a + 1) * ATT_Q_ROWS, :]
                s_refs[slot][j * apg + ai] = lax.dot_general(
                    q, window(si, h, a), (((1,), (1,)), ((), ())), preferred_element_type=F32)

    def softmax_pv(si, step, grp, slot):
        for j in range(hpi):
            h = (step // ng) * hpi + j
            for ai in range(apg):
                a = grp * apg + ai
                rows = slice(a * ATT_Q_ROWS, (a + 1) * ATT_Q_ROWS)
                variant = (first if a == 0 else 0) + (2 * last if a == na - 1 else 0)
                s = s_refs[slot][j * apg + ai] + bias_ref[variant, h]
                mx = jnp.max(s, axis=-1, keepdims=True)
                p = jnp.exp2(s - mx)
                v = window(si, ATT_HEADS + h, a)
                pv = jnp.dot(p.astype(BF16), jnp.concatenate([v, ones], axis=1), preferred_element_type=F32)
                o_ref[si, h, rows, :] = pv[:, :HEAD_DIM].astype(BF16)
                st_ref[si, rows, :] = jnp.where(
                    lane == h, mx, jnp.where(lane == h + ATT_HEADS, pv[:, HEAD_DIM:], st_ref[si, rows, :]))

    def sequence(si, carry):
        scores(si, 0, 0, 0)

        def body(tt, c):
            for u in range(trip):
                step = trip * tt + u
                scores(si, (step + 1) % steps, (u + 1) % ng, (u + 1) % 2)
                softmax_pv(si, step, u % ng, u % 2)
            return c

        lax.fori_loop(0, steps // trip, body, 0)
        return carry

    if ns == 1:
        sequence(0, 0)
    else:
        lax.fori_loop(0, ns, sequence, 0)


def _banded_attention(hd, bias):
    nseq, _, length, _ = hd.shape
    qb = min(ATT_BLOCK_ROWS, length)
    nblk = length // qb
    ns = max(1, ATT_MIN_STEP_ROWS // length)
    hb = qb // RADIUS
    nh = length // RADIUS
    units = max(1, ATT_UNITS // (qb // ATT_Q_ROWS)) * min(ATT_UNITS, qb // ATT_Q_ROWS)
    return pl.pallas_call(
        functools.partial(_attn_kernel, qb=qb, nblk=nblk, ns=ns),
        grid=(nseq // ns, nblk),
        in_specs=[pl.BlockSpec((ns, ATT_HEADS, qb, LANES), lambda s, n: (s, 2, n, 0)),
                  pl.BlockSpec((ns, 2 * ATT_HEADS, RADIUS, LANES),
                               lambda s, n: (s, 0, jnp.maximum(n * hb - 1, 0), 0)),
                  pl.BlockSpec((ns, 2 * ATT_HEADS, qb, LANES), lambda s, n: (s, 0, n, 0)),
                  pl.BlockSpec((ns, 2 * ATT_HEADS, RADIUS, LANES),
                               lambda s, n: (s, 0, jnp.minimum((n + 1) * hb, nh - 1), 0)),
                  pl.BlockSpec(bias.shape, lambda s, n: (0, 0, 0, 0), pipeline_mode=pl.Buffered(1))],
        out_specs=[pl.BlockSpec((ns, ATT_HEADS, qb, LANES), lambda s, n: (s, 0, n, 0)),
                   pl.BlockSpec((ns, qb, LANES), lambda s, n: (s, n, 0))],
        out_shape=[jax.ShapeDtypeStruct((nseq, ATT_HEADS, length, LANES), BF16),
                   jax.ShapeDtypeStruct((nseq, length, LANES), F32)],
        scratch_shapes=[pltpu.VMEM((units, ATT_Q_ROWS, ATT_KEYS), F32)] * 2,
        compiler_params=pltpu.CompilerParams(
            dimension_semantics=("parallel", "parallel"), vmem_limit_bytes=VMEM_LIMIT),
        name="banded_attention",
    )(hd, hd, hd, hd, bias)


def _merge_kernel(o1_ref, o4_ref, o16_ref, s1_ref, s4_ref, s16_ref,
                  ga_ref, ob_ref, x_ref, w_ref, g_ref, b_ref, out_ref, outb_ref,
                  y0_ref, y1_ref, st_ref, r_ref, f0_ref, f1_ref, *, tm, total):
    t = pl.program_id(0)
    ys = (y0_ref, y1_ref)
    fs = (f0_ref, f1_ref)
    ncol = w_ref.shape[1] // (2 * LANES)
    hpp = ATT_HEADS // ncol

    def merge_weights():
        st_ref[0] = s1_ref[0]
        for p, (d, sr) in enumerate(((4, s4_ref), (16, s16_ref)), start=1):
            for g in range(d):
                st_ref[p, pl.ds(g, tm // d, stride=d), :] = sr[g]
        ms = [st_ref[p] for p in range(3)]
        mx = jnp.maximum(jnp.maximum(ms[0], ms[1]), ms[2])
        es = [jnp.exp2(m - mx) for m in ms]
        den = sum(e * pltpu.roll(m, LANES - ATT_HEADS, axis=1) for e, m in zip(es, ms))
        head_lane = lax.broadcasted_iota(jnp.int32, den.shape, 1) < ATT_HEADS
        den = jnp.where(head_lane, den, 1.0)
        for p in range(3):
            st_ref[p] = es[p] / den

    def merge_head(dst, h):
        for p, (d, o_ref) in enumerate(((4, o4_ref), (16, o16_ref))):
            for g in range(d):
                r_ref[p, pl.ds(g, tm // d, stride=d), :] = o_ref[g, h].astype(F32)
        oa = (st_ref[0, :, h:h + 1] * o1_ref[0, h].astype(F32)
              + st_ref[1, :, h:h + 1] * r_ref[0]
              + st_ref[2, :, h:h + 1] * r_ref[1])
        cols = slice(h * HEAD_DIM, (h + 1) * HEAD_DIM)
        dst[:, cols] = (oa * ga_ref[:, cols].astype(F32)).astype(BF16)

    def product(src, dst, k):
        cols = slice(k * 2 * LANES, (k + 1) * 2 * LANES)
        dst[:, cols] = jnp.dot(src[...], w_ref[:, cols], preferred_element_type=F32)

    def normalise(src, r):
        rows = slice(r * ROW_CHUNK, (r + 1) * ROW_CHUNK)
        y = _layernorm(DN_ALPHA * x_ref[rows, :] + src[rows, :], g_ref[...], b_ref[...])
        out_ref[rows, :] = y
        outb_ref[rows, :] = y.astype(BF16)

    nln = tm // ROW_CHUNK
    ln_after = {(i + 1) * ncol // (nln + 1): i for i in range(nln)}

    def step(parity, merge, project, norm):
        if merge:
            merge_weights()
        for k in range(ncol):
            if project:
                product(ys[1 - parity], fs[1 - parity], k)
            if merge:
                for h in range(k * hpp, (k + 1) * hpp):
                    merge_head(ys[parity], h)
            if norm and k in ln_after:
                normalise(fs[parity], ln_after[k])
        if merge:
            ys[parity][:, ATT_WIDTH:] = ob_ref[...]

    @pl.when(t == 0)
    def _():
        step(0, True, False, False)

    @pl.when(t == 1)
    def _():
        step(1, True, True, False)

    for parity in (0, 1):
        @pl.when((t > 1) & (t < total) & (t % 2 == parity))
        def _(parity=parity):
            step(parity, True, True, True)

    @pl.when(t == total)
    def _():
        step(total % 2, False, True, True)

    @pl.when(t == total + 1)
    def _():
        step((total + 1) % 2, False, False, True)


def _merge_outproj(o_pats, stats, ga, ob, x, w_out, ln_g, ln_b, *, tm=256):
    b, s, dm = x.shape
    nt = s // tm
    total = b * nt
    ga = ga.reshape(b, s, ATT_WIDTH)
    ob = ob.reshape(b, s, POOL_WIDTH)

    def cur(t):
        t = jnp.minimum(t, total - 1)
        return t // nt, t % nt

    def prev(t):
        t = jnp.clip(t - 2, 0, total - 1)
        return t // nt, t % nt

    def o_map(t):
        bi, i = cur(t)
        return bi, 0, 0, i, 0

    def st_map(t):
        bi, i = cur(t)
        return bi, 0, i, 0

    def cur_map(t):
        bi, i = cur(t)
        return bi, i, 0

    def prev_map(t):
        bi, i = prev(t)
        return bi, i, 0

    o_specs = [pl.BlockSpec((None, d, ATT_HEADS, tm // d, LANES), o_map) for d in DILATIONS]
    st_specs = [pl.BlockSpec((None, d, tm // d, LANES), st_map) for d in DILATIONS]
    const2 = lambda t: (0, 0)
    args = list(o_pats) + list(stats)
    return pl.pallas_call(
        functools.partial(_merge_kernel, tm=tm, total=total),
        grid=(total + 2,),
        in_specs=o_specs + st_specs + [
            pl.BlockSpec((None, tm, ATT_WIDTH), cur_map),
            pl.BlockSpec((None, tm, POOL_WIDTH), cur_map),
            pl.BlockSpec((None, tm, dm), prev_map),
            pl.BlockSpec(w_out.shape, const2, pipeline_mode=pl.Buffered(1)),
            pl.BlockSpec((1, dm), const2),
            pl.BlockSpec((1, dm), const2)],
        out_specs=[pl.BlockSpec((None, tm, dm), prev_map), pl.BlockSpec((None, tm, dm), prev_map)],
        out_shape=[jax.ShapeDtypeStruct((b, s, dm), F32), jax.ShapeDtypeStruct((b, s, dm), BF16)],
        scratch_shapes=[pltpu.VMEM((tm, ATT_WIDTH + POOL_WIDTH), BF16),
                        pltpu.VMEM((tm, ATT_WIDTH + POOL_WIDTH), BF16),
                        pltpu.VMEM((3, tm, LANES), F32),
                        pltpu.VMEM((2, tm, LANES), F32),
                        pltpu.VMEM((tm, dm), F32),
                        pltpu.VMEM((tm, dm), F32)],
        compiler_params=pltpu.CompilerParams(dimension_semantics=("arbitrary",), vmem_limit_bytes=VMEM_LIMIT),
        name="merge_outproj",
    )(*args, ga, ob, x, w_out, ln_g, ln_b)


def _conv_in_kernel(xp_ref, x_ref, xn_ref, wb_ref, wc_ref, wv_ref, wg_ref, cw_ref, o_ref, cv_ref, *, tm, seq):
    i = pl.program_id(0)
    tiles_per_seq = seq // tm
    keep_prev = jnp.where((i % tiles_per_seq) == 0, 0.0, 1.0)
    keep_next = jnp.where((i % tiles_per_seq) == tiles_per_seq - 1, 0.0, 1.0)
    nch = tm // ROW_CHUNK

    def conv_input(r):
        parts = [x_ref[r * ROW_CHUNK:(r + 1) * ROW_CHUNK, :]]
        if r == 0:
            parts.insert(0, xp_ref[...])
        if r == nch - 1:
            parts.append(xn_ref[...])
        xa = parts[0] if len(parts) == 1 else jnp.concatenate(parts, axis=0)
        lo = HALO + r * ROW_CHUNK - (HALO if r == 0 else 0)
        cv_ref[lo:lo + xa.shape[0]] = (jnp.dot(xa, wc_ref[...], preferred_element_type=F32)
                                       * jnp.dot(xa, wv_ref[...], preferred_element_type=F32))
        if r == 0:
            cv_ref[0:HALO] = cv_ref[0:HALO] * keep_prev
        if r == nch - 1:
            cv_ref[HALO + tm:] = cv_ref[HALO + tm:] * keep_next

    def gated_conv(r):
        base = HALO + r * ROW_CHUNK
        xm = x_ref[r * ROW_CHUNK:(r + 1) * ROW_CHUNK, :]
        gb = jnp.dot(xm, wb_ref[...], preferred_element_type=F32)
        gate = jnp.dot(xm, wg_ref[...], preferred_element_type=F32)
        conv = (cv_ref[pl.ds(base - 1, ROW_CHUNK), :] * cw_ref[0:1, :]
                + cv_ref[pl.ds(base, ROW_CHUNK), :] * cw_ref[1:2, :]
                + cv_ref[pl.ds(base + 1, ROW_CHUNK), :] * cw_ref[2:3, :])
        o_ref[r * ROW_CHUNK:(r + 1) * ROW_CHUNK, :] = (gb * conv * _silu(gate)).astype(BF16)

    conv_input(0)
    for r in range(nch):
        if r + 1 < nch:
            conv_input(r + 1)
        gated_conv(r)


def _conv_inproj(xb, w_in, conv_w, *, seq, tm=1024, tc=1024):
    m, dm = xb.shape
    hb = tm // HALO
    nhb = m // HALO
    nc = CONV_WIDTH // tc

    def wspec(part):
        return pl.BlockSpec((dm, tc), lambda i, c: (0, part * nc + c))

    return pl.pallas_call(
        functools.partial(_conv_in_kernel, tm=tm, seq=seq),
        grid=(m // tm, nc),
        in_specs=[pl.BlockSpec((HALO, dm), lambda i, c: (jnp.maximum(i * hb - 1, 0), 0)),
                  pl.BlockSpec((tm, dm), lambda i, c: (i, 0)),
                  pl.BlockSpec((HALO, dm), lambda i, c: (jnp.minimum((i + 1) * hb, nhb - 1), 0)),
                  wspec(0), wspec(1), wspec(2), wspec(3),
                  pl.BlockSpec((3, tc), lambda i, c: (0, c))],
        out_specs=pl.BlockSpec((tm, tc), lambda i, c: (i, c)),
        out_shape=jax.ShapeDtypeStruct((m, CONV_WIDTH), BF16),
        scratch_shapes=[pltpu.VMEM((tm + 2 * HALO, tc), F32)],
        compiler_params=pltpu.CompilerParams(
            dimension_semantics=("parallel", "parallel"), vmem_limit_bytes=VMEM_LIMIT),
        name="conv_inproj",
    )(xb, xb, xb, w_in, w_in, w_in, w_in, conv_w)


def _outproj_ln_kernel(y_ref, x_ref, w_ref, g_ref, b_ref, o_ref):
    for r in range(0, y_ref.shape[0], ROW_CHUNK):
        rows = slice(r, r + ROW_CHUNK)
        f = jnp.dot(y_ref[rows, :], w_ref[...], preferred_element_type=F32)
        o_ref[rows, :] = _layernorm(DN_ALPHA * x_ref[rows, :] + f, g_ref[...], b_ref[...])


def _outproj_ln(y, x2, w_out, ln_g, ln_b, *, tm=512):
    m, dm = x2.shape
    const2 = lambda i: (0, 0)
    return pl.pallas_call(
        _outproj_ln_kernel,
        grid=(m // tm,),
        in_specs=[pl.BlockSpec((tm, y.shape[1]), lambda i: (i, 0)),
                  pl.BlockSpec((tm, dm), lambda i: (i, 0)),
                  pl.BlockSpec(w_out.shape, const2, pipeline_mode=pl.Buffered(1)),
                  pl.BlockSpec((1, dm), const2),
                  pl.BlockSpec((1, dm), const2)],
        out_specs=pl.BlockSpec((tm, dm), lambda i: (i, 0)),
        out_shape=jax.ShapeDtypeStruct((m, dm), F32),
        compiler_params=pltpu.CompilerParams(dimension_semantics=("parallel",), vmem_limit_bytes=VMEM_LIMIT),
        name="outproj_ln",
    )(y, x2, w_out, ln_g, ln_b)


def _trunk(x, biases, w_ab, pool_w, pool_scale, w_out_ab, w_in_c, conv_w, w_out_c, ln_g, ln_b):
    b, s, dm = x.shape
    x2 = x.reshape(b * s, dm)
    *hds, ga = _qkvg_proj(x, w_ab)
    ob = _pool_branch(x2, w_ab, pool_w, pool_scale, seq=s)
    o_pats, stats = [], []
    for d, hd, bias in zip(DILATIONS, hds, biases):
        ld = s // d
        o, st = _banded_attention(hd.reshape(b * d, 3 * ATT_HEADS, ld, LANES), bias)
        o_pats.append(o.reshape(b, d, ATT_HEADS, ld, LANES))
        stats.append(st.reshape(b, d, ld, LANES))
    x1, x1b = _merge_outproj(o_pats, stats, ga, ob, x, w_out_ab, ln_g[0:1], ln_b[0:1])
    y = _conv_inproj(x1b.reshape(b * s, dm), w_in_c, conv_w, seq=s)
    out = _outproj_ln(y, x1.reshape(b * s, dm), w_out_c, ln_g[1:2], ln_b[1:2])
    return out.reshape(b, s, dm)


def kernel(x_prompt, x_sample, rel_bias, w_in_ab, pool_w, pool_scale, w_out_ab, w_in_c, conv_w, w_out_c, ln_g, ln_b):
    assert DEPTH == 2 and w_in_ab.shape[0] == 1 and w_in_c.shape[0] == 1
    params = dict(
        biases=[_band_bias(rel_bias, d) for d in DILATIONS],
        w_ab=w_in_ab[0].astype(BF16),
        pool_w=pool_w[0].astype(BF16),
        pool_scale=pool_scale[0].reshape(1, POOL_WIDTH),
        w_out_ab=w_out_ab[0].astype(BF16),
        w_in_c=w_in_c[0].astype(BF16),
        conv_w=conv_w[0],
        w_out_c=w_out_c[0].astype(BF16),
        ln_g=ln_g,
        ln_b=ln_b,
    )
    return (_trunk(x_prompt, **params), _trunk(x_sample, **params))
```

```python
import functools
import math

import numpy as np
import jax
import jax.numpy as jnp
from jax import lax
from jax.experimental import pallas as pl
from jax.experimental.pallas import tpu as pltpu

D_MODEL = 2048
DEPTH = 2
ATT_HEADS = 16
HEAD_DIM = 128
ATT_WIDTH = ATT_HEADS * HEAD_DIM
DILATED_PATTERNS = ((128, 1), (512, 4), (2048, 16))
DILATIONS = tuple(d for _, d in DILATED_PATTERNS)
RADIUS = 64
POOL_WINDOWS = (2, 4, 8, 16)
POOL_WIDTH = D_MODEL // 2
POOL_GROUP = POOL_WIDTH // len(POOL_WINDOWS)
CONV_WIDTH = D_MODEL
REL_BUCKETS = 32
REL_MAX_DISTANCE = 1024
DN_ALPHA = (2 * DEPTH) ** 0.25
LN_EPS = 1e-5
NEG_INF = -1e30
LOG2E = math.log2(math.e)

LANES = 128
HALO = 16
ATT_Q_ROWS = 128
ATT_KEYS = ATT_Q_ROWS + 2 * RADIUS
ATT_UNITS = 4
ATT_TRIP_STEPS = 16
ATT_BLOCK_ROWS = 1024
ATT_MIN_STEP_ROWS = 512
PROJ_ROWS = 256
ROW_CHUNK = 128
VMEM_LIMIT = 56 * 1024 * 1024

BF16 = jnp.bfloat16
F32 = jnp.float32

assert all(w // (2 * d) == RADIUS for w, d in DILATED_PATTERNS)


def _t5_bucket(rel):
    nb = REL_BUCKETS // 2
    max_exact = nb // 2
    ret = np.where(rel > 0, nb, 0)
    n = np.abs(rel)
    n_safe = np.maximum(n, 1).astype(np.float64)
    large = max_exact + (np.log(n_safe / max_exact) / math.log(REL_MAX_DISTANCE / max_exact)
                         * (nb - max_exact)).astype(np.int64)
    large = np.minimum(large, nb - 1)
    return (ret + np.where(n < max_exact, n, large)).astype(np.int32)


def _band_bias(rel_bias, dilation):
    rel = np.arange(ATT_KEYS)[None, :] - RADIUS - np.arange(ATT_Q_ROWS)[:, None]
    bucket = jnp.asarray(_t5_bucket(rel * dilation).reshape(1, -1))
    onehot = (bucket == jnp.arange(REL_BUCKETS, dtype=jnp.int32)[:, None]).astype(F32)
    bias = jnp.dot(rel_bias.astype(F32).T, onehot, precision=lax.Precision.HIGHEST)
    bias = bias.reshape(1, ATT_HEADS, ATT_Q_ROWS, ATT_KEYS)
    key = np.arange(ATT_KEYS)[None, :]
    in_band = np.abs(rel) <= RADIUS
    after_start = key >= RADIUS
    before_end = key < ATT_Q_ROWS + RADIUS
    keep = np.stack([in_band, in_band & after_start, in_band & before_end, in_band & after_start & before_end])
    return jnp.where(jnp.asarray(keep[:, None]), bias * LOG2E, NEG_INF)


def _silu(x):
    return x / (1.0 + jnp.exp(-x))


def _layernorm(z, g, b):
    mu = jnp.mean(z, axis=-1, keepdims=True)
    zc = z - mu
    var = jnp.mean(zc * zc, axis=-1, keepdims=True)
    return zc * lax.rsqrt(var + LN_EPS) * g + b


def _qkvg_kernel(x_ref, w_ref, o1_ref, o4_ref, o16_ref, ga_ref, xb_ref, s0_ref, s1_ref, mid0_ref, mid1_ref,
                 *, tm, tn, nq):
    j = pl.program_id(2)
    chunk_bufs = ((s0_ref, mid0_ref), (s1_ref, mid1_ref))
    r4, r16 = PROJ_ROWS // 4, PROJ_ROWS // 16

    @pl.when(j == 0)
    def _():
        xb_ref[...] = x_ref[...].astype(BF16)

    def product(rc, cc):
        rows = slice(rc * PROJ_ROWS, (rc + 1) * PROJ_ROWS)
        cols = slice(cc * 2 * LANES, (cc + 1) * 2 * LANES)
        return jnp.dot(xb_ref[rows, :], w_ref[:, cols], preferred_element_type=F32)

    def emit_qkv(acc, rc, cc, s_ref, mid_ref):
        acc = acc * jnp.where(j < ATT_WIDTH // tn, LOG2E * HEAD_DIM ** -0.5, 1.0)
        for c in range(2):
            slab = 2 * cc + c
            blk = acc[:, c * LANES:(c + 1) * LANES]
            o1_ref[0, slab, rc * PROJ_ROWS:(rc + 1) * PROJ_ROWS, :] = blk.astype(BF16)
            s_ref[c] = blk
            for g4 in range(4):
                rows4 = s_ref[c, pl.ds(g4, r4, stride=4), :]
                o4_ref[g4, slab, rc * r4:(rc + 1) * r4, :] = rows4.astype(BF16)
                mid_ref[c, g4] = rows4
            for g4 in range(4):
                for k in range(4):
                    o16_ref[g4 + 4 * k, slab, rc * r16:(rc + 1) * r16, :] = (
                        mid_ref[c, g4, pl.ds(k, r16, stride=4), :].astype(BF16))

    def emit_gate(acc, rc, cc):
        ga_ref[rc * PROJ_ROWS:(rc + 1) * PROJ_ROWS, cc * 2 * LANES:(cc + 1) * 2 * LANES] = _silu(acc).astype(BF16)

    chunks = [(rc, cc) for cc in range(tn // (2 * LANES)) for rc in range(tm // PROJ_ROWS)]

    @pl.when(j < nq)
    def _():
        for n, (rc, cc) in enumerate(chunks):
            emit_qkv(product(rc, cc), rc, cc, *chunk_bufs[n % 2])

    @pl.when(j >= nq)
    def _():
        for rc, cc in chunks:
            emit_gate(product(rc, cc), rc, cc)


def _qkvg_proj(x, w, *, tm=1024, tn=1024):
    b, s, dm = x.shape
    nj = 4 * ATT_WIDTH // tn
    nq = 3 * ATT_WIDTH // tn
    qt = ATT_WIDTH // tn
    nslab = 3 * ATT_WIDTH // LANES
    outs = [jax.ShapeDtypeStruct((b, d, nslab, s // d, LANES), BF16) for d in DILATIONS]
    outs.append(jax.ShapeDtypeStruct((b, s, ATT_WIDTH), BF16))
    out_specs = [pl.BlockSpec((None, d, tn // LANES, tm // d, LANES),
                              lambda bi, i, j: (bi, 0, (jnp.minimum(j, nq - 1) + 2 * qt) % nq, i, 0)) for d in DILATIONS]
    out_specs.append(pl.BlockSpec((None, tm, tn), lambda bi, i, j: (bi, i, jnp.maximum(j - nq, 0))))
    return pl.pallas_call(
        functools.partial(_qkvg_kernel, tm=tm, tn=tn, nq=nq),
        grid=(b, s // tm, nj),
        in_specs=[pl.BlockSpec((None, tm, dm), lambda bi, i, j: (bi, i, 0)),
                  pl.BlockSpec((dm, tn), lambda bi, i, j: (0, j))],
        out_specs=out_specs,
        out_shape=outs,
        scratch_shapes=[pltpu.VMEM((tm, dm), BF16)]
        + [pltpu.VMEM((2, PROJ_ROWS, LANES), F32)] * 2
        + [pltpu.VMEM((2, 4, PROJ_ROWS // 4, LANES), F32)] * 2,
        compiler_params=pltpu.CompilerParams(
            dimension_semantics=("parallel", "parallel", "arbitrary"), vmem_limit_bytes=VMEM_LIMIT),
        name="qkvg_proj",
    )(x, w)


def _pool_kernel(xp_ref, x_ref, xn_ref, wu_ref, wg_ref, pw_ref, ps_ref, o_ref, xb_ref, u_ref, *, tm, seq):
    i = pl.program_id(0)
    tiles_per_seq = seq // tm
    keep_prev = jnp.where((i % tiles_per_seq) == 0, 0.0, 1.0)
    keep_next = jnp.where((i % tiles_per_seq) == tiles_per_seq - 1, 0.0, 1.0)
    nch = tm // ROW_CHUNK

    xb_ref[0:HALO] = xp_ref[...].astype(BF16)
    xb_ref[HALO:HALO + tm] = x_ref[...].astype(BF16)
    xb_ref[HALO + tm:] = xn_ref[...].astype(BF16)

    def pool_input(r):
        lo = HALO + r * ROW_CHUNK - (HALO if r == 0 else 0)
        hi = HALO + (r + 1) * ROW_CHUNK + (HALO if r == nch - 1 else 0)
        u_ref[lo:hi] = jnp.dot(xb_ref[lo:hi], wu_ref[...], preferred_element_type=F32)
        if r == 0:
            u_ref[0:HALO] = u_ref[0:HALO] * keep_prev
        if r == nch - 1:
            u_ref[HALO + tm:] = u_ref[HALO + tm:] * keep_next

    def pooled_out(r):
        base = HALO + r * ROW_CHUNK
        rows = slice(r * ROW_CHUNK, (r + 1) * ROW_CHUNK)
        gate = jnp.dot(xb_ref[base:base + ROW_CHUNK], wg_ref[...], preferred_element_type=F32)
        pos = ((i % tiles_per_seq) * tm + r * ROW_CHUNK
               + lax.broadcasted_iota(jnp.int32, (ROW_CHUNK, POOL_GROUP), 0))
        for gi, w in enumerate(POOL_WINDOWS):
            cols = slice(gi * POOL_GROUP, (gi + 1) * POOL_GROUP)
            tot = u_ref[pl.ds(base - w // 2, ROW_CHUNK), cols]
            for off in range(-w // 2 + 1, w // 2):
                tot = tot + u_ref[pl.ds(base + off, ROW_CHUNK), cols]
            lo = jnp.maximum(pos - w // 2, 0)
            hi = jnp.minimum(pos + w // 2 - 1, seq - 1)
            cnt = (hi - lo + 1).astype(F32)
            pooled = tot / cnt - u_ref[pl.ds(base, ROW_CHUNK), cols]
            ob = jnp.dot(pooled.astype(BF16), pw_ref[gi], preferred_element_type=F32)
            o_ref[rows, cols] = (ob * ps_ref[:, cols] * _silu(gate[:, cols])).astype(BF16)

    pool_input(0)
    for r in range(nch):
        if r + 1 < nch:
            pool_input(r + 1)
        pooled_out(r)


def _pool_branch(x2, w_ab, pool_w, pool_scale, *, seq, tm=512):
    m, dm = x2.shape
    ub = 4 * ATT_WIDTH // POOL_WIDTH
    hb = tm // HALO
    nhb = m // HALO
    return pl.pallas_call(
        functools.partial(_pool_kernel, tm=tm, seq=seq),
        grid=(m // tm,),
        in_specs=[pl.BlockSpec((HALO, dm), lambda i: (jnp.maximum(i * hb - 1, 0), 0)),
                  pl.BlockSpec((tm, dm), lambda i: (i, 0)),
                  pl.BlockSpec((HALO, dm), lambda i: (jnp.minimum((i + 1) * hb, nhb - 1), 0)),
                  pl.BlockSpec((dm, POOL_WIDTH), lambda i: (0, ub)),
                  pl.BlockSpec((dm, POOL_WIDTH), lambda i: (0, ub + 1)),
                  pl.BlockSpec((len(POOL_WINDOWS), POOL_GROUP, POOL_GROUP), lambda i: (0, 0, 0)),
                  pl.BlockSpec((1, POOL_WIDTH), lambda i: (0, 0))],
        out_specs=pl.BlockSpec((tm, POOL_WIDTH), lambda i: (i, 0)),
        out_shape=jax.ShapeDtypeStruct((m, POOL_WIDTH), BF16),
        scratch_shapes=[pltpu.VMEM((tm + 2 * HALO, dm), BF16), pltpu.VMEM((tm + 2 * HALO, POOL_WIDTH), F32)],
        compiler_params=pltpu.CompilerParams(dimension_semantics=("parallel",), vmem_limit_bytes=VMEM_LIMIT),
        name="pool_branch",
    )(x2, x2, x2, w_ab, w_ab, pool_w, pool_scale)


def _attn_kernel(q_ref, kvp_ref, kvm_ref, kvn_ref, bias_ref, o_ref, st_ref, s0_ref, s1_ref, *, qb, nblk, ns):
    n = pl.program_id(1)
    na = qb // ATT_Q_ROWS
    ng = max(1, na // ATT_UNITS)
    apg = na // ng
    hpi = max(1, ATT_UNITS // na)
    steps = (ATT_HEADS // hpi) * ng
    trip = min(ATT_TRIP_STEPS, steps)
    assert steps % trip == 0 and trip % (2 * ng) == 0
    lane = lax.broadcasted_iota(jnp.int32, (ATT_Q_ROWS, LANES), 1)
    st_ref[...] = jnp.broadcast_to(
        jnp.where((lane[:1] >= ATT_HEADS) & (lane[:1] < 2 * ATT_HEADS), 1.0, 0.0), st_ref.shape)
    first = (n == 0).astype(jnp.int32)
    last = (n == nblk - 1).astype(jnp.int32)
    s_refs = (s0_ref, s1_ref)
    ones = jnp.ones((ATT_KEYS, LANES), BF16)

    def window(si, slab, a):
        lo = a * ATT_Q_ROWS - RADIUS
        hi = lo + ATT_KEYS
        parts = []
        if lo < 0:
            parts.append(kvp_ref[si, slab])
        parts.append(kvm_ref[si, slab, max(lo, 0):min(hi, qb), :])
        if hi > qb:
            parts.append(kvn_ref[si, slab])
        return parts[0] if len(parts) == 1 else jnp.concatenate(parts, axis=0)

    def scores(si, step, grp, slot):
        for j in range(hpi):
            h = (step // ng) * hpi + j
            for ai in range(apg):
                a = grp * apg + ai
                q = q_ref[si, h, a * ATT_Q_ROWS:(a + 1) * ATT_Q_ROWS, :]
                s_refs[slot][j * apg + ai] = lax.dot_general(
                    q, window(si, h, a), (((1,), (1,)), ((), ())), preferred_element_type=F32)

    def softmax_pv(si, step, grp, slot):
        for j in range(hpi):
            h = (step // ng) * hpi + j
            for ai in range(apg):
                a = grp * apg + ai
                rows = slice(a * ATT_Q_ROWS, (a + 1) * ATT_Q_ROWS)
                variant = (first if a == 0 else 0) + (2 * last if a == na - 1 else 0)
                s = s_refs[slot][j * apg + ai] + bias_ref[variant, h]
                mx = jnp.max(s, axis=-1, keepdims=True)
                p = jnp.exp2(s - mx)
                v = window(si, ATT_HEADS + h, a)
                pv = jnp.dot(p.astype(BF16), jnp.concatenate([v, ones], axis=1), preferred_element_type=F32)
                o_ref[si, h, rows, :] = pv[:, :HEAD_DIM].astype(BF16)
                st_ref[si, rows, :] = jnp.where(
                    lane == h, mx, jnp.where(lane == h + ATT_HEADS, pv[:, HEAD_DIM:], st_ref[si, rows, :]))

    def sequence(si, carry):
        scores(si, 0, 0, 0)

        def body(tt, c):
            for u in range(trip):
                step = trip * tt + u
                scores(si, (step + 1) % steps, (u + 1) % ng, (u + 1) % 2)
                softmax_pv(si, step, u % ng, u % 2)
            return c

        lax.fori_loop(0, steps // trip, body, 0)
        return carry

    if ns == 1:
        sequence(0, 0)
    else:
        lax.fori_loop(0, ns, sequence, 0)


def _banded_attention(hd, bias):
    nseq, _, length, _ = hd.shape
    qb = min(ATT_BLOCK_ROWS, length)
    nblk = length // qb
    ns = max(1, ATT_MIN_STEP_ROWS // length)
    hb = qb // RADIUS
    nh = length // RADIUS
    units = max(1, ATT_UNITS // (qb // ATT_Q_ROWS)) * min(ATT_UNITS, qb // ATT_Q_ROWS)
    return pl.pallas_call(
        functools.partial(_attn_kernel, qb=qb, nblk=nblk, ns=ns),
        grid=(nseq // ns, nblk),
        in_specs=[pl.BlockSpec((ns, ATT_HEADS, qb, LANES), lambda s, n: (s, 2, n, 0)),
                  pl.BlockSpec((ns, 2 * ATT_HEADS, RADIUS, LANES),
                               lambda s, n: (s, 0, jnp.maximum(n * hb - 1, 0), 0)),
                  pl.BlockSpec((ns, 2 * ATT_HEADS, qb, LANES), lambda s, n: (s, 0, n, 0)),
                  pl.BlockSpec((ns, 2 * ATT_HEADS, RADIUS, LANES),
                               lambda s, n: (s, 0, jnp.minimum((n + 1) * hb, nh - 1), 0)),
                  pl.BlockSpec(bias.shape, lambda s, n: (0, 0, 0, 0), pipeline_mode=pl.Buffered(1))],
        out_specs=[pl.BlockSpec((ns, ATT_HEADS, qb, LANES), lambda s, n: (s, 0, n, 0)),
                   pl.BlockSpec((ns, qb, LANES), lambda s, n: (s, n, 0))],
        out_shape=[jax.ShapeDtypeStruct((nseq, ATT_HEADS, length, LANES), BF16),
                   jax.ShapeDtypeStruct((nseq, length, LANES), F32)],
        scratch_shapes=[pltpu.VMEM((units, ATT_Q_ROWS, ATT_KEYS), F32)] * 2,
        compiler_params=pltpu.CompilerParams(
            dimension_semantics=("parallel", "parallel"), vmem_limit_bytes=VMEM_LIMIT),
        name="banded_attention",
    )(hd, hd, hd, hd, bias)


def _merge_kernel(o1_ref, o4_ref, o16_ref, s1_ref, s4_ref, s16_ref,
                  ga_ref, ob_ref, x_ref, w_ref, g_ref, b_ref, out_ref, outb_ref,
                  y0_ref, y1_ref, st_ref, r_ref, f_ref, *, tm, total):
    t = pl.program_id(0)
    ys = (y0_ref, y1_ref)
    ncol = w_ref.shape[1] // (2 * LANES)
    hpp = ATT_HEADS // ncol

    def merge_weights():
        st_ref[0] = s1_ref[0]
        for p, (d, sr) in enumerate(((4, s4_ref), (16, s16_ref)), start=1):
            for g in range(d):
                st_ref[p, pl.ds(g, tm // d, stride=d), :] = sr[g]
        ms = [st_ref[p] for p in range(3)]
        mx = jnp.maximum(jnp.maximum(ms[0], ms[1]), ms[2])
        es = [jnp.exp2(m - mx) for m in ms]
        den = sum(e * pltpu.roll(m, LANES - ATT_HEADS, axis=1) for e, m in zip(es, ms))
        head_lane = lax.broadcasted_iota(jnp.int32, den.shape, 1) < ATT_HEADS
        den = jnp.where(head_lane, den, 1.0)
        for p in range(3):
            st_ref[p] = es[p] / den

    def merge_head(dst, h):
        for p, (d, o_ref) in enumerate(((4, o4_ref), (16, o16_ref))):
            for g in range(d):
                r_ref[p, pl.ds(g, tm // d, stride=d), :] = o_ref[g, h].astype(F32)
        oa = (st_ref[0, :, h:h + 1] * o1_ref[0, h].astype(F32)
              + st_ref[1, :, h:h + 1] * r_ref[0]
              + st_ref[2, :, h:h + 1] * r_ref[1])
        cols = slice(h * HEAD_DIM, (h + 1) * HEAD_DIM)
        dst[:, cols] = (oa * ga_ref[:, cols].astype(F32)).astype(BF16)

    def product(src, k):
        cols = slice(k * 2 * LANES, (k + 1) * 2 * LANES)
        f_ref[:, cols] = jnp.dot(src[...], w_ref[:, cols], preferred_element_type=F32)
        if k == ncol - 1:
            for r in range(0, tm, ROW_CHUNK):
                rows = slice(r, r + ROW_CHUNK)
                y = _layernorm(DN_ALPHA * x_ref[rows, :] + f_ref[rows, :], g_ref[...], b_ref[...])
                out_ref[rows, :] = y
                outb_ref[rows, :] = y.astype(BF16)

    @pl.when(t == 0)
    def _():
        merge_weights()
        for h in range(ATT_HEADS):
            merge_head(ys[0], h)
        ys[0][:, ATT_WIDTH:] = ob_ref[...]

    @pl.when(t == total)
    def _():
        for k in range(ncol):
            product(ys[(total - 1) % 2], k)

    for parity in (0, 1):
        @pl.when((t > 0) & (t < total) & (t % 2 == parity))
        def _(parity=parity):
            merge_weights()
            for k in range(ncol):
                product(ys[1 - parity], k)
                for h in range(k * hpp, (k + 1) * hpp):
                    merge_head(ys[parity], h)
            ys[parity][:, ATT_WIDTH:] = ob_ref[...]


def _merge_outproj(o_pats, stats, ga, ob, x, w_out, ln_g, ln_b, *, tm=256):
    b, s, dm = x.shape
    nt = s // tm
    total = b * nt
    ga = ga.reshape(b, s, ATT_WIDTH)
    ob = ob.reshape(b, s, POOL_WIDTH)

    def cur(t):
        t = jnp.minimum(t, total - 1)
        return t // nt, t % nt

    def prev(t):
        t = jnp.maximum(t - 1, 0)
        return t // nt, t % nt

    def o_map(t):
        bi, i = cur(t)
        return bi, 0, 0, i, 0

    def st_map(t):
        bi, i = cur(t)
        return bi, 0, i, 0

    def cur_map(t):
        bi, i = cur(t)
        return bi, i, 0

    def prev_map(t):
        bi, i = prev(t)
        return bi, i, 0

    o_specs = [pl.BlockSpec((None, d, ATT_HEADS, tm // d, LANES), o_map) for d in DILATIONS]
    st_specs = [pl.BlockSpec((None, d, tm // d, LANES), st_map) for d in DILATIONS]
    const2 = lambda t: (0, 0)
    args = list(o_pats) + list(stats)
    return pl.pallas_call(
        functools.partial(_merge_kernel, tm=tm, total=total),
        grid=(total + 1,),
        in_specs=o_specs + st_specs + [
            pl.BlockSpec((None, tm, ATT_WIDTH), cur_map),
            pl.BlockSpec((None, tm, POOL_WIDTH), cur_map),
            pl.BlockSpec((None, tm, dm), prev_map),
            pl.BlockSpec(w_out.shape, const2, pipeline_mode=pl.Buffered(1)),
            pl.BlockSpec((1, dm), const2),
            pl.BlockSpec((1, dm), const2)],
        out_specs=[pl.BlockSpec((None, tm, dm), prev_map), pl.BlockSpec((None, tm, dm), prev_map)],
        out_shape=[jax.ShapeDtypeStruct((b, s, dm), F32), jax.ShapeDtypeStruct((b, s, dm), BF16)],
        scratch_shapes=[pltpu.VMEM((tm, ATT_WIDTH + POOL_WIDTH), BF16),
                        pltpu.VMEM((tm, ATT_WIDTH + POOL_WIDTH), BF16),
                        pltpu.VMEM((3, tm, LANES), F32),
                        pltpu.VMEM((2, tm, LANES), F32),
                        pltpu.VMEM((tm, dm), F32)],
        compiler_params=pltpu.CompilerParams(dimension_semantics=("arbitrary",), vmem_limit_bytes=VMEM_LIMIT),
        name="merge_outproj",
    )(*args, ga, ob, x, w_out, ln_g, ln_b)


def _conv_in_kernel(xp_ref, x_ref, xn_ref, wb_ref, wc_ref, wv_ref, wg_ref, cw_ref, o_ref, cv_ref, *, tm, seq):
    i = pl.program_id(0)
    tiles_per_seq = seq // tm
    keep_prev = jnp.where((i % tiles_per_seq) == 0, 0.0, 1.0)
    keep_next = jnp.where((i % tiles_per_seq) == tiles_per_seq - 1, 0.0, 1.0)
    nch = tm // ROW_CHUNK

    def conv_input(r):
        parts = [x_ref[r * ROW_CHUNK:(r + 1) * ROW_CHUNK, :]]
        if r == 0:
            parts.insert(0, xp_ref[...])
        if r == nch - 1:
            parts.append(xn_ref[...])
        xa = parts[0] if len(parts) == 1 else jnp.concatenate(parts, axis=0)
        lo = HALO + r * ROW_CHUNK - (HALO if r == 0 else 0)
        cv_ref[lo:lo + xa.shape[0]] = (jnp.dot(xa, wc_ref[...], preferred_element_type=F32)
                                       * jnp.dot(xa, wv_ref[...], preferred_element_type=F32))
        if r == 0:
            cv_ref[0:HALO] = cv_ref[0:HALO] * keep_prev
        if r == nch - 1:
            cv_ref[HALO + tm:] = cv_ref[HALO + tm:] * keep_next

    def gated_conv(r):
        base = HALO + r * ROW_CHUNK
        xm = x_ref[r * ROW_CHUNK:(r + 1) * ROW_CHUNK, :]
        gb = jnp.dot(xm, wb_ref[...], preferred_element_type=F32)
        gate = jnp.dot(xm, wg_ref[...], preferred_element_type=F32)
        conv = (cv_ref[pl.ds(base - 1, ROW_CHUNK), :] * cw_ref[0:1, :]
                + cv_ref[pl.ds(base, ROW_CHUNK), :] * cw_ref[1:2, :]
                + cv_ref[pl.ds(base + 1, ROW_CHUNK), :] * cw_ref[2:3, :])
        o_ref[r * ROW_CHUNK:(r + 1) * ROW_CHUNK, :] = (gb * conv * _silu(gate)).astype(BF16)

    conv_input(0)
    for r in range(nch):
        if r + 1 < nch:
            conv_input(r + 1)
        gated_conv(r)


def _conv_inproj(xb, w_in, conv_w, *, seq, tm=1024, tc=1024):
    m, dm = xb.shape
    hb = tm // HALO
    nhb = m // HALO
    nc = CONV_WIDTH // tc

    def wspec(part):
        return pl.BlockSpec((dm, tc), lambda i, c: (0, part * nc + c))

    return pl.pallas_call(
        functools.partial(_conv_in_kernel, tm=tm, seq=seq),
        grid=(m // tm, nc),
        in_specs=[pl.BlockSpec((HALO, dm), lambda i, c: (jnp.maximum(i * hb - 1, 0), 0)),
                  pl.BlockSpec((tm, dm), lambda i, c: (i, 0)),
                  pl.BlockSpec((HALO, dm), lambda i, c: (jnp.minimum((i + 1) * hb, nhb - 1), 0)),
                  wspec(0), wspec(1), wspec(2), wspec(3),
                  pl.BlockSpec((3, tc), lambda i, c: (0, c))],
        out_specs=pl.BlockSpec((tm, tc), lambda i, c: (i, c)),
        out_shape=jax.ShapeDtypeStruct((m, CONV_WIDTH), BF16),
        scratch_shapes=[pltpu.VMEM((tm + 2 * HALO, tc), F32)],
        compiler_params=pltpu.CompilerParams(
            dimension_semantics=("parallel", "parallel"), vmem_limit_bytes=VMEM_LIMIT),
        name="conv_inproj",
    )(xb, xb, xb, w_in, w_in, w_in, w_in, conv_w)


def _outproj_ln_kernel(y_ref, x_ref, w_ref, g_ref, b_ref, o_ref):
    for r in range(0, y_ref.shape[0], ROW_CHUNK):
        rows = slice(r, r + ROW_CHUNK)
        f = jnp.dot(y_ref[rows, :], w_ref[...], preferred_element_type=F32)
        o_ref[rows, :] = _layernorm(DN_ALPHA * x_ref[rows, :] + f, g_ref[...], b_ref[...])


def _outproj_ln(y, x2, w_out, ln_g, ln_b, *, tm=512):
    m, dm = x2.shape
    const2 = lambda i: (0, 0)
    return pl.pallas_call(
        _outproj_ln_kernel,
        grid=(m // tm,),
        in_specs=[pl.BlockSpec((tm, y.shape[1]), lambda i: (i, 0)),
                  pl.BlockSpec((tm, dm), lambda i: (i, 0)),
                  pl.BlockSpec(w_out.shape, const2, pipeline_mode=pl.Buffered(1)),
                  pl.BlockSpec((1, dm), const2),
                  pl.BlockSpec((1, dm), const2)],
        out_specs=pl.BlockSpec((tm, dm), lambda i: (i, 0)),
        out_shape=jax.ShapeDtypeStruct((m, dm), F32),
        compiler_params=pltpu.CompilerParams(dimension_semantics=("parallel",), vmem_limit_bytes=VMEM_LIMIT),
        name="outproj_ln",
    )(y, x2, w_out, ln_g, ln_b)


def _trunk(x, biases, w_ab, pool_w, pool_scale, w_out_ab, w_in_c, conv_w, w_out_c, ln_g, ln_b):
    b, s, dm = x.shape
    x2 = x.reshape(b * s, dm)
    *hds, ga = _qkvg_proj(x, w_ab)
    ob = _pool_branch(x2, w_ab, pool_w, pool_scale, seq=s)
    o_pats, stats = [], []
    for d, hd, bias in zip(DILATIONS, hds, biases):
        ld = s // d
        o, st = _banded_attention(hd.reshape(b * d, 3 * ATT_HEADS, ld, LANES), bias)
        o_pats.append(o.reshape(b, d, ATT_HEADS, ld, LANES))
        stats.append(st.reshape(b, d, ld, LANES))
    x1, x1b = _merge_outproj(o_pats, stats, ga, ob, x, w_out_ab, ln_g[0:1], ln_b[0:1])
    y = _conv_inproj(x1b.reshape(b * s, dm), w_in_c, conv_w, seq=s)
    out = _outproj_ln(y, x1.reshape(b * s, dm), w_out_c, ln_g[1:2], ln_b[1:2])
    return out.reshape(b, s, dm)


def kernel(x_prompt, x_sample, rel_bias, w_in_ab, pool_w, pool_scale, w_out_ab, w_in_c, conv_w, w_out_c, ln_g, ln_b):
    assert DEPTH == 2 and w_in_ab.shape[0] == 1 and w_in_c.shape[0] == 1
    params = dict(
        biases=[_band_bias(rel_bias, d) for d in DILATIONS],
        w_ab=w_in_ab[0].astype(BF16),
        pool_w=pool_w[0].astype(BF16),
        pool_scale=pool_scale[0].reshape(1, POOL_WIDTH),
        w_out_ab=w_out_ab[0].astype(BF16),
        w_in_c=w_in_c[0].astype(BF16),
        conv_w=conv_w[0],
        w_out_c=w_out_c[0].astype(BF16),
        ln_g=ln_g,
        ln_b=ln_b,
    )
    return (_trunk(x_prompt, **params), _trunk(x_sample, **params))
```

```python
import functools
import math

import numpy as np
import jax
import jax.numpy as jnp
from jax import lax
from jax.experimental import pallas as pl
from jax.experimental.pallas import tpu as pltpu

D_MODEL = 2048
DEPTH = 2
ATT_HEADS = 16
HEAD_DIM = 128
ATT_WIDTH = ATT_HEADS * HEAD_DIM
DILATED_PATTERNS = ((128, 1), (512, 4), (2048, 16))
DILATIONS = tuple(d for _, d in DILATED_PATTERNS)
RADIUS = 64
POOL_WINDOWS = (2, 4, 8, 16)
POOL_WIDTH = D_MODEL // 2
POOL_GROUP = POOL_WIDTH // len(POOL_WINDOWS)
CONV_WIDTH = D_MODEL
REL_BUCKETS = 32
REL_MAX_DISTANCE = 1024
DN_ALPHA = (2 * DEPTH) ** 0.25
LN_EPS = 1e-5
NEG_INF = -1e30
LOG2E = math.log2(math.e)

LANES = 128
HALO = 16
ATT_Q_ROWS = 128
ATT_KEYS = ATT_Q_ROWS + 2 * RADIUS
ATT_UNITS = 4
ATT_TRIP_STEPS = 32
ATT_BLOCK_ROWS = 1024
ATT_MIN_STEP_ROWS = 512
PROJ_ROWS = 256
ROW_CHUNK = 128
VMEM_LIMIT = 56 * 1024 * 1024

BF16 = jnp.bfloat16
F32 = jnp.float32

assert all(w // (2 * d) == RADIUS for w, d in DILATED_PATTERNS)


def _t5_bucket(rel):
    nb = REL_BUCKETS // 2
    max_exact = nb // 2
    ret = np.where(rel > 0, nb, 0)
    n = np.abs(rel)
    n_safe = np.maximum(n, 1).astype(np.float64)
    large = max_exact + (np.log(n_safe / max_exact) / math.log(REL_MAX_DISTANCE / max_exact)
                         * (nb - max_exact)).astype(np.int64)
    large = np.minimum(large, nb - 1)
    return (ret + np.where(n < max_exact, n, large)).astype(np.int32)


def _band_bias(rel_bias, dilation):
    rel = np.arange(ATT_KEYS)[None, :] - RADIUS - np.arange(ATT_Q_ROWS)[:, None]
    bucket = jnp.asarray(_t5_bucket(rel * dilation).reshape(1, -1))
    onehot = (bucket == jnp.arange(REL_BUCKETS, dtype=jnp.int32)[:, None]).astype(F32)
    bias = jnp.dot(rel_bias.astype(F32).T, onehot, precision=lax.Precision.HIGHEST)
    bias = bias.reshape(1, ATT_HEADS, ATT_Q_ROWS, ATT_KEYS)
    key = np.arange(ATT_KEYS)[None, :]
    in_band = np.abs(rel) <= RADIUS
    after_start = key >= RADIUS
    before_end = key < ATT_Q_ROWS + RADIUS
    keep = np.stack([in_band, in_band & after_start, in_band & before_end, in_band & after_start & before_end])
    return jnp.where(jnp.asarray(keep[:, None]), bias * LOG2E, NEG_INF)


def _silu(x):
    return x / (1.0 + jnp.exp(-x))


def _layernorm(z, g, b):
    mu = jnp.mean(z, axis=-1, keepdims=True)
    zc = z - mu
    var = jnp.mean(zc * zc, axis=-1, keepdims=True)
    return zc * lax.rsqrt(var + LN_EPS) * g + b


def _qkvg_kernel(x_ref, w_ref, o1_ref, o4_ref, o16_ref, ga_ref, xb_ref, s0_ref, s1_ref, mid0_ref, mid1_ref,
                 *, tm, tn, nq):
    j = pl.program_id(2)
    chunk_bufs = ((s0_ref, mid0_ref), (s1_ref, mid1_ref))
    r4, r16 = PROJ_ROWS // 4, PROJ_ROWS // 16

    @pl.when(j == 0)
    def _():
        xb_ref[...] = x_ref[...].astype(BF16)

    def product(rc, cc):
        rows = slice(rc * PROJ_ROWS, (rc + 1) * PROJ_ROWS)
        cols = slice(cc * 2 * LANES, (cc + 1) * 2 * LANES)
        return jnp.dot(xb_ref[rows, :], w_ref[:, cols], preferred_element_type=F32)

    def emit_qkv(acc, rc, cc, s_ref, mid_ref):
        acc = acc * jnp.where(j < ATT_WIDTH // tn, LOG2E * HEAD_DIM ** -0.5, 1.0)
        for c in range(2):
            slab = 2 * cc + c
            blk = acc[:, c * LANES:(c + 1) * LANES]
            o1_ref[0, slab, rc * PROJ_ROWS:(rc + 1) * PROJ_ROWS, :] = blk.astype(BF16)
            s_ref[c] = blk
            for g4 in range(4):
                rows4 = s_ref[c, pl.ds(g4, r4, stride=4), :]
                o4_ref[g4, slab, rc * r4:(rc + 1) * r4, :] = rows4.astype(BF16)
                mid_ref[c, g4] = rows4
            for g4 in range(4):
                for k in range(4):
                    o16_ref[g4 + 4 * k, slab, rc * r16:(rc + 1) * r16, :] = (
                        mid_ref[c, g4, pl.ds(k, r16, stride=4), :].astype(BF16))

    def emit_gate(acc, rc, cc):
        ga_ref[rc * PROJ_ROWS:(rc + 1) * PROJ_ROWS, cc * 2 * LANES:(cc + 1) * 2 * LANES] = _silu(acc).astype(BF16)

    chunks = [(rc, cc) for cc in range(tn // (2 * LANES)) for rc in range(tm // PROJ_ROWS)]

    @pl.when(j < nq)
    def _():
        for n, (rc, cc) in enumerate(chunks):
            emit_qkv(product(rc, cc), rc, cc, *chunk_bufs[n % 2])

    @pl.when(j >= nq)
    def _():
        for rc, cc in chunks:
            emit_gate(product(rc, cc), rc, cc)


def _qkvg_proj(x, w, *, tm=1024, tn=1024):
    b, s, dm = x.shape
    nj = 4 * ATT_WIDTH // tn
    nq = 3 * ATT_WIDTH // tn
    qt = ATT_WIDTH // tn
    nslab = 3 * ATT_WIDTH // LANES
    outs = [jax.ShapeDtypeStruct((b, d, nslab, s // d, LANES), BF16) for d in DILATIONS]
    outs.append(jax.ShapeDtypeStruct((b, s, ATT_WIDTH), BF16))
    out_specs = [pl.BlockSpec((None, d, tn // LANES, tm // d, LANES),
                              lambda bi, i, j: (bi, 0, (jnp.minimum(j, nq - 1) + 2 * qt) % nq, i, 0)) for d in DILATIONS]
    out_specs.append(pl.BlockSpec((None, tm, tn), lambda bi, i, j: (bi, i, jnp.maximum(j - nq, 0))))
    return pl.pallas_call(
        functools.partial(_qkvg_kernel, tm=tm, tn=tn, nq=nq),
        grid=(b, s // tm, nj),
        in_specs=[pl.BlockSpec((None, tm, dm), lambda bi, i, j: (bi, i, 0)),
                  pl.BlockSpec((dm, tn), lambda bi, i, j: (0, j))],
        out_specs=out_specs,
        out_shape=outs,
        scratch_shapes=[pltpu.VMEM((tm, dm), BF16)]
        + [pltpu.VMEM((2, PROJ_ROWS, LANES), F32)] * 2
        + [pltpu.VMEM((2, 4, PROJ_ROWS // 4, LANES), F32)] * 2,
        compiler_params=pltpu.CompilerParams(
            dimension_semantics=("parallel", "parallel", "arbitrary"), vmem_limit_bytes=VMEM_LIMIT),
        name="qkvg_proj",
    )(x, w)


def _pool_kernel(xp_ref, x_ref, xn_ref, wu_ref, wg_ref, pw_ref, ps_ref, o_ref, xb_ref, u_ref, *, tm, seq):
    i = pl.program_id(0)
    tiles_per_seq = seq // tm
    keep_prev = jnp.where((i % tiles_per_seq) == 0, 0.0, 1.0)
    keep_next = jnp.where((i % tiles_per_seq) == tiles_per_seq - 1, 0.0, 1.0)
    nch = tm // ROW_CHUNK

    xb_ref[0:HALO] = xp_ref[...].astype(BF16)
    xb_ref[HALO:HALO + tm] = x_ref[...].astype(BF16)
    xb_ref[HALO + tm:] = xn_ref[...].astype(BF16)

    def pool_input(r):
        lo = HALO + r * ROW_CHUNK - (HALO if r == 0 else 0)
        hi = HALO + (r + 1) * ROW_CHUNK + (HALO if r == nch - 1 else 0)
        u_ref[lo:hi] = jnp.dot(xb_ref[lo:hi], wu_ref[...], preferred_element_type=F32)
        if r == 0:
            u_ref[0:HALO] = u_ref[0:HALO] * keep_prev
        if r == nch - 1:
            u_ref[HALO + tm:] = u_ref[HALO + tm:] * keep_next

    def pooled_out(r):
        base = HALO + r * ROW_CHUNK
        rows = slice(r * ROW_CHUNK, (r + 1) * ROW_CHUNK)
        gate = jnp.dot(xb_ref[base:base + ROW_CHUNK], wg_ref[...], preferred_element_type=F32)
        pos = ((i % tiles_per_seq) * tm + r * ROW_CHUNK
               + lax.broadcasted_iota(jnp.int32, (ROW_CHUNK, POOL_GROUP), 0))
        for gi, w in enumerate(POOL_WINDOWS):
            cols = slice(gi * POOL_GROUP, (gi + 1) * POOL_GROUP)
            tot = u_ref[pl.ds(base - w // 2, ROW_CHUNK), cols]
            for off in range(-w // 2 + 1, w // 2):
                tot = tot + u_ref[pl.ds(base + off, ROW_CHUNK), cols]
            lo = jnp.maximum(pos - w // 2, 0)
            hi = jnp.minimum(pos + w // 2 - 1, seq - 1)
            cnt = (hi - lo + 1).astype(F32)
            pooled = tot / cnt - u_ref[pl.ds(base, ROW_CHUNK), cols]
            ob = jnp.dot(pooled.astype(BF16), pw_ref[gi], preferred_element_type=F32)
            o_ref[rows, cols] = (ob * ps_ref[:, cols] * _silu(gate[:, cols])).astype(BF16)

    pool_input(0)
    for r in range(nch):
        if r + 1 < nch:
            pool_input(r + 1)
        pooled_out(r)


def _pool_branch(x2, w_ab, pool_w, pool_scale, *, seq, tm=512):
    m, dm = x2.shape
    ub = 4 * ATT_WIDTH // POOL_WIDTH
    hb = tm // HALO
    nhb = m // HALO
    return pl.pallas_call(
        functools.partial(_pool_kernel, tm=tm, seq=seq),
        grid=(m // tm,),
        in_specs=[pl.BlockSpec((HALO, dm), lambda i: (jnp.maximum(i * hb - 1, 0), 0)),
                  pl.BlockSpec((tm, dm), lambda i: (i, 0)),
                  pl.BlockSpec((HALO, dm), lambda i: (jnp.minimum((i + 1) * hb, nhb - 1), 0)),
                  pl.BlockSpec((dm, POOL_WIDTH), lambda i: (0, ub)),
                  pl.BlockSpec((dm, POOL_WIDTH), lambda i: (0, ub + 1)),
                  pl.BlockSpec((len(POOL_WINDOWS), POOL_GROUP, POOL_GROUP), lambda i: (0, 0, 0)),
                  pl.BlockSpec((1, POOL_WIDTH), lambda i: (0, 0))],
        out_specs=pl.BlockSpec((tm, POOL_WIDTH), lambda i: (i, 0)),
        out_shape=jax.ShapeDtypeStruct((m, POOL_WIDTH), BF16),
        scratch_shapes=[pltpu.VMEM((tm + 2 * HALO, dm), BF16), pltpu.VMEM((tm + 2 * HALO, POOL_WIDTH), F32)],
        compiler_params=pltpu.CompilerParams(dimension_semantics=("parallel",), vmem_limit_bytes=VMEM_LIMIT),
        name="pool_branch",
    )(x2, x2, x2, w_ab, w_ab, pool_w, pool_scale)


def _attn_kernel(q_ref, kvp_ref, kvm_ref, kvn_ref, bias_ref, o_ref, st_ref, s0_ref, s1_ref, *, qb, nblk, ns):
    n = pl.program_id(1)
    na = qb // ATT_Q_ROWS
    ng = max(1, na // ATT_UNITS)
    apg = na // ng
    hpi = max(1, ATT_UNITS // na)
    steps = (ATT_HEADS // hpi) * ng
    trip = min(ATT_TRIP_STEPS, steps)
    assert steps % trip == 0 and trip % (2 * ng) == 0
    lane = lax.broadcasted_iota(jnp.int32, (ATT_Q_ROWS, LANES), 1)
    st_ref[...] = jnp.broadcast_to(
        jnp.where((lane[:1] >= ATT_HEADS) & (lane[:1] < 2 * ATT_HEADS), 1.0, 0.0), st_ref.shape)
    first = (n == 0).astype(jnp.int32)
    last = (n == nblk - 1).astype(jnp.int32)
    s_refs = (s0_ref, s1_ref)
    ones = jnp.ones((ATT_KEYS, LANES), BF16)

    def window(si, slab, a):
        lo = a * ATT_Q_ROWS - RADIUS
        hi = lo + ATT_KEYS
        parts = []
        if lo < 0:
            parts.append(kvp_ref[si, slab])
        parts.append(kvm_ref[si, slab, max(lo, 0):min(hi, qb), :])
        if hi > qb:
            parts.append(kvn_ref[si, slab])
        return parts[0] if len(parts) == 1 else jnp.concatenate(parts, axis=0)

    def scores(si, step, grp, slot):
        for j in range(hpi):
            h = (step // ng) * hpi + j
            for ai in range(apg):
                a = grp * apg + ai
                q = q_ref[si, h, a * ATT_Q_ROWS:(a + 1) * ATT_Q_ROWS, :]
                s_refs[slot][j * apg + ai] = lax.dot_general(
                    q, window(si, h, a), (((1,), (1,)), ((), ())), preferred_element_type=F32)

    def softmax_pv(si, step, grp, slot):
        for j in range(hpi):
            h = (step // ng) * hpi + j
            for ai in range(apg):
                a = grp * apg + ai
                rows = slice(a * ATT_Q_ROWS, (a + 1) * ATT_Q_ROWS)
                variant = (first if a == 0 else 0) + (2 * last if a == na - 1 else 0)
                s = s_refs[slot][j * apg + ai] + bias_ref[variant, h]
                mx = jnp.max(s, axis=-1, keepdims=True)
                p = jnp.exp2(s - mx)
                v = window(si, ATT_HEADS + h, a)
                pv = jnp.dot(p.astype(BF16), jnp.concatenate([v, ones], axis=1), preferred_element_type=F32)
                o_ref[si, h, rows, :] = pv[:, :HEAD_DIM].astype(BF16)
                st_ref[si, rows, :] = jnp.where(
                    lane == h, mx, jnp.where(lane == h + ATT_HEADS, pv[:, HEAD_DIM:], st_ref[si, rows, :]))

    def sequence(si, carry):
        scores(si, 0, 0, 0)

        def body(tt, c):
            for u in range(trip):
                step = trip * tt + u
                scores(si, (step + 1) % steps, (u + 1) % ng, (u + 1) % 2)
                softmax_pv(si, step, u % ng, u % 2)
            return c

        lax.fori_loop(0, steps // trip, body, 0)
        return carry

    if ns == 1:
        sequence(0, 0)
    else:
        lax.fori_loop(0, ns, sequence, 0)


def _banded_attention(hd, bias):
    nseq, _, length, _ = hd.shape
    qb = min(ATT_BLOCK_ROWS, length)
    nblk = length // qb
    ns = max(1, ATT_MIN_STEP_ROWS // length)
    hb = qb // RADIUS
    nh = length // RADIUS
    units = max(1, ATT_UNITS // (qb // ATT_Q_ROWS)) * min(ATT_UNITS, qb // ATT_Q_ROWS)
    return pl.pallas_call(
        functools.partial(_attn_kernel, qb=qb, nblk=nblk, ns=ns),
        grid=(nseq // ns, nblk),
        in_specs=[pl.BlockSpec((ns, ATT_HEADS, qb, LANES), lambda s, n: (s, 2, n, 0)),
                  pl.BlockSpec((ns, 2 * ATT_HEADS, RADIUS, LANES),
                               lambda s, n: (s, 0, jnp.maximum(n * hb - 1, 0), 0)),
                  pl.BlockSpec((ns, 2 * ATT_HEADS, qb, LANES), lambda s, n: (s, 0, n, 0)),
                  pl.BlockSpec((ns, 2 * ATT_HEADS, RADIUS, LANES),
                               lambda s, n: (s, 0, jnp.minimum((n + 1) * hb, nh - 1), 0)),
                  pl.BlockSpec(bias.shape, lambda s, n: (0, 0, 0, 0), pipeline_mode=pl.Buffered(1))],
        out_specs=[pl.BlockSpec((ns, ATT_HEADS, qb, LANES), lambda s, n: (s, 0, n, 0)),
                   pl.BlockSpec((ns, qb, LANES), lambda s, n: (s, n, 0))],
        out_shape=[jax.ShapeDtypeStruct((nseq, ATT_HEADS, length, LANES), BF16),
                   jax.ShapeDtypeStruct((nseq, length, LANES), F32)],
        scratch_shapes=[pltpu.VMEM((units, ATT_Q_ROWS, ATT_KEYS), F32)] * 2,
        compiler_params=pltpu.CompilerParams(
            dimension_semantics=("parallel", "parallel"), vmem_limit_bytes=VMEM_LIMIT),
        name="banded_attention",
    )(hd, hd, hd, hd, bias)


def _merge_kernel(o1_ref, o4_ref, o16_ref, s1_ref, s4_ref, s16_ref,
                  ga_ref, ob_ref, x_ref, w_ref, g_ref, b_ref, out_ref, outb_ref,
                  y0_ref, y1_ref, st_ref, r_ref, f_ref, *, tm, total):
    t = pl.program_id(0)
    ys = (y0_ref, y1_ref)
    ncol = w_ref.shape[1] // (2 * LANES)
    hpp = ATT_HEADS // ncol

    def merge_weights():
        st_ref[0] = s1_ref[0]
        for p, (d, sr) in enumerate(((4, s4_ref), (16, s16_ref)), start=1):
            for g in range(d):
                st_ref[p, pl.ds(g, tm // d, stride=d), :] = sr[g]
        ms = [st_ref[p] for p in range(3)]
        mx = jnp.maximum(jnp.maximum(ms[0], ms[1]), ms[2])
        es = [jnp.exp2(m - mx) for m in ms]
        den = sum(e * pltpu.roll(m, LANES - ATT_HEADS, axis=1) for e, m in zip(es, ms))
        head_lane = lax.broadcasted_iota(jnp.int32, den.shape, 1) < ATT_HEADS
        den = jnp.where(head_lane, den, 1.0)
        for p in range(3):
            st_ref[p] = es[p] / den

    def merge_head(dst, h):
        for p, (d, o_ref) in enumerate(((4, o4_ref), (16, o16_ref))):
            for g in range(d):
                r_ref[p, pl.ds(g, tm // d, stride=d), :] = o_ref[g, h].astype(F32)
        oa = (st_ref[0, :, h:h + 1] * o1_ref[0, h].astype(F32)
              + st_ref[1, :, h:h + 1] * r_ref[0]
              + st_ref[2, :, h:h + 1] * r_ref[1])
        cols = slice(h * HEAD_DIM, (h + 1) * HEAD_DIM)
        dst[:, cols] = (oa * ga_ref[:, cols].astype(F32)).astype(BF16)

    def product(src, k):
        cols = slice(k * 2 * LANES, (k + 1) * 2 * LANES)
        f_ref[:, cols] = jnp.dot(src[...], w_ref[:, cols], preferred_element_type=F32)
        if k == ncol - 1:
            for r in range(0, tm, ROW_CHUNK):
                rows = slice(r, r + ROW_CHUNK)
                y = _layernorm(DN_ALPHA * x_ref[rows, :] + f_ref[rows, :], g_ref[...], b_ref[...])
                out_ref[rows, :] = y
                outb_ref[rows, :] = y.astype(BF16)

    @pl.when(t == 0)
    def _():
        merge_weights()
        for h in range(ATT_HEADS):
            merge_head(ys[0], h)
        ys[0][:, ATT_WIDTH:] = ob_ref[...]

    @pl.when(t == total)
    def _():
        for k in range(ncol):
            product(ys[(total - 1) % 2], k)

    for parity in (0, 1):
        @pl.when((t > 0) & (t < total) & (t % 2 == parity))
        def _(parity=parity):
            merge_weights()
            for k in range(ncol):
                product(ys[1 - parity], k)
                for h in range(k * hpp, (k + 1) * hpp):
                    merge_head(ys[parity], h)
            ys[parity][:, ATT_WIDTH:] = ob_ref[...]


def _merge_outproj(o_pats, stats, ga, ob, x, w_out, ln_g, ln_b, *, tm=256):
    b, s, dm = x.shape
    nt = s // tm
    total = b * nt
    ga = ga.reshape(b, s, ATT_WIDTH)
    ob = ob.reshape(b, s, POOL_WIDTH)

    def cur(t):
        t = jnp.minimum(t, total - 1)
        return t // nt, t % nt

    def prev(t):
        t = jnp.maximum(t - 1, 0)
        return t // nt, t % nt

    def o_map(t):
        bi, i = cur(t)
        return bi, 0, 0, i, 0

    def st_map(t):
        bi, i = cur(t)
        return bi, 0, i, 0

    def cur_map(t):
        bi, i = cur(t)
        return bi, i, 0

    def prev_map(t):
        bi, i = prev(t)
        return bi, i, 0

    o_specs = [pl.BlockSpec((None, d, ATT_HEADS, tm // d, LANES), o_map) for d in DILATIONS]
    st_specs = [pl.BlockSpec((None, d, tm // d, LANES), st_map) for d in DILATIONS]
    const2 = lambda t: (0, 0)
    args = list(o_pats) + list(stats)
    return pl.pallas_call(
        functools.partial(_merge_kernel, tm=tm, total=total),
        grid=(total + 1,),
        in_specs=o_specs + st_specs + [
            pl.BlockSpec((None, tm, ATT_WIDTH), cur_map),
            pl.BlockSpec((None, tm, POOL_WIDTH), cur_map),
            pl.BlockSpec((None, tm, dm), prev_map),
            pl.BlockSpec(w_out.shape, const2, pipeline_mode=pl.Buffered(1)),
            pl.BlockSpec((1, dm), const2),
            pl.BlockSpec((1, dm), const2)],
        out_specs=[pl.BlockSpec((None, tm, dm), prev_map), pl.BlockSpec((None, tm, dm), prev_map)],
        out_shape=[jax.ShapeDtypeStruct((b, s, dm), F32), jax.ShapeDtypeStruct((b, s, dm), BF16)],
        scratch_shapes=[pltpu.VMEM((tm, ATT_WIDTH + POOL_WIDTH), BF16),
                        pltpu.VMEM((tm, ATT_WIDTH + POOL_WIDTH), BF16),
                        pltpu.VMEM((3, tm, LANES), F32),
                        pltpu.VMEM((2, tm, LANES), F32),
                        pltpu.VMEM((tm, dm), F32)],
        compiler_params=pltpu.CompilerParams(dimension_semantics=("arbitrary",), vmem_limit_bytes=VMEM_LIMIT),
        name="merge_outproj",
    )(*args, ga, ob, x, w_out, ln_g, ln_b)


def _conv_in_kernel(xp_ref, x_ref, xn_ref, wb_ref, wc_ref, wv_ref, wg_ref, cw_ref, o_ref, cv_ref, *, tm, seq):
    i = pl.program_id(0)
    tiles_per_seq = seq // tm
    keep_prev = jnp.where((i % tiles_per_seq) == 0, 0.0, 1.0)
    keep_next = jnp.where((i % tiles_per_seq) == tiles_per_seq - 1, 0.0, 1.0)
    nch = tm // ROW_CHUNK

    def conv_input(r):
        parts = [x_ref[r * ROW_CHUNK:(r + 1) * ROW_CHUNK, :]]
        if r == 0:
            parts.insert(0, xp_ref[...])
        if r == nch - 1:
            parts.append(xn_ref[...])
        xa = parts[0] if len(parts) == 1 else jnp.concatenate(parts, axis=0)
        lo = HALO + r * ROW_CHUNK - (HALO if r == 0 else 0)
        cv_ref[lo:lo + xa.shape[0]] = (jnp.dot(xa, wc_ref[...], preferred_element_type=F32)
                                       * jnp.dot(xa, wv_ref[...], preferred_element_type=F32))
        if r == 0:
            cv_ref[0:HALO] = cv_ref[0:HALO] * keep_prev
        if r == nch - 1:
            cv_ref[HALO + tm:] = cv_ref[HALO + tm:] * keep_next

    def gated_conv(r):
        base = HALO + r * ROW_CHUNK
        xm = x_ref[r * ROW_CHUNK:(r + 1) * ROW_CHUNK, :]
        gb = jnp.dot(xm, wb_ref[...], preferred_element_type=F32)
        gate = jnp.dot(xm, wg_ref[...], preferred_element_type=F32)
        conv = (cv_ref[pl.ds(base - 1, ROW_CHUNK), :] * cw_ref[0:1, :]
                + cv_ref[pl.ds(base, ROW_CHUNK), :] * cw_ref[1:2, :]
                + cv_ref[pl.ds(base + 1, ROW_CHUNK), :] * cw_ref[2:3, :])
        o_ref[r * ROW_CHUNK:(r + 1) * ROW_CHUNK, :] = (gb * conv * _silu(gate)).astype(BF16)

    conv_input(0)
    for r in range(nch):
        if r + 1 < nch:
            conv_input(r + 1)
        gated_conv(r)


def _conv_inproj(xb, w_in, conv_w, *, seq, tm=1024, tc=1024):
    m, dm = xb.shape
    hb = tm // HALO
    nhb = m // HALO
    nc = CONV_WIDTH // tc

    def wspec(part):
        return pl.BlockSpec((dm, tc), lambda i, c: (0, part * nc + c))

    return pl.pallas_call(
        functools.partial(_conv_in_kernel, tm=tm, seq=seq),
        grid=(m // tm, nc),
        in_specs=[pl.BlockSpec((HALO, dm), lambda i, c: (jnp.maximum(i * hb - 1, 0), 0)),
                  pl.BlockSpec((tm, dm), lambda i, c: (i, 0)),
                  pl.BlockSpec((HALO, dm), lambda i, c: (jnp.minimum((i + 1) * hb, nhb - 1), 0)),
                  wspec(0), wspec(1), wspec(2), wspec(3),
                  pl.BlockSpec((3, tc), lambda i, c: (0, c))],
        out_specs=pl.BlockSpec((tm, tc), lambda i, c: (i, c)),
        out_shape=jax.ShapeDtypeStruct((m, CONV_WIDTH), BF16),
        scratch_shapes=[pltpu.VMEM((tm + 2 * HALO, tc), F32)],
        compiler_params=pltpu.CompilerParams(
            dimension_semantics=("parallel", "parallel"), vmem_limit_bytes=VMEM_LIMIT),
        name="conv_inproj",
    )(xb, xb, xb, w_in, w_in, w_in, w_in, conv_w)


def _outproj_ln_kernel(y_ref, x_ref, w_ref, g_ref, b_ref, o_ref):
    for r in range(0, y_ref.shape[0], ROW_CHUNK):
        rows = slice(r, r + ROW_CHUNK)
        f = jnp.dot(y_ref[rows, :], w_ref[...], preferred_element_type=F32)
        o_ref[rows, :] = _layernorm(DN_ALPHA * x_ref[rows, :] + f, g_ref[...], b_ref[...])


def _outproj_ln(y, x2, w_out, ln_g, ln_b, *, tm=512):
    m, dm = x2.shape
    const2 = lambda i: (0, 0)
    return pl.pallas_call(
        _outproj_ln_kernel,
        grid=(m // tm,),
        in_specs=[pl.BlockSpec((tm, y.shape[1]), lambda i: (i, 0)),
                  pl.BlockSpec((tm, dm), lambda i: (i, 0)),
                  pl.BlockSpec(w_out.shape, const2, pipeline_mode=pl.Buffered(1)),
                  pl.BlockSpec((1, dm), const2),
                  pl.BlockSpec((1, dm), const2)],
        out_specs=pl.BlockSpec((tm, dm), lambda i: (i, 0)),
        out_shape=jax.ShapeDtypeStruct((m, dm), F32),
        compiler_params=pltpu.CompilerParams(dimension_semantics=("parallel",), vmem_limit_bytes=VMEM_LIMIT),
        name="outproj_ln",
    )(y, x2, w_out, ln_g, ln_b)


def _trunk(x, biases, w_ab, pool_w, pool_scale, w_out_ab, w_in_c, conv_w, w_out_c, ln_g, ln_b):
    b, s, dm = x.shape
    x2 = x.reshape(b * s, dm)
    *hds, ga = _qkvg_proj(x, w_ab)
    ob = _pool_branch(x2, w_ab, pool_w, pool_scale, seq=s)
    o_pats, stats = [], []
    for d, hd, bias in zip(DILATIONS, hds, biases):
        ld = s // d
        o, st = _banded_attention(hd.reshape(b * d, 3 * ATT_HEADS, ld, LANES), bias)
        o_pats.append(o.reshape(b, d, ATT_HEADS, ld, LANES))
        stats.append(st.reshape(b, d, ld, LANES))
    x1, x1b = _merge_outproj(o_pats, stats, ga, ob, x, w_out_ab, ln_g[0:1], ln_b[0:1])
    y = _conv_inproj(x1b.reshape(b * s, dm), w_in_c, conv_w, seq=s)
    out = _outproj_ln(y, x1.reshape(b * s, dm), w_out_c, ln_g[1:2], ln_b[1:2])
    return out.reshape(b, s, dm)


def kernel(x_prompt, x_sample, rel_bias, w_in_ab, pool_w, pool_scale, w_out_ab, w_in_c, conv_w, w_out_c, ln_g, ln_b):
    assert DEPTH == 2 and w_in_ab.shape[0] == 1 and w_in_c.shape[0] == 1
    params = dict(
        biases=[_band_bias(rel_bias, d) for d in DILATIONS],
        w_ab=w_in_ab[0].astype(BF16),
        pool_w=pool_w[0].astype(BF16),
        pool_scale=pool_scale[0].reshape(1, POOL_WIDTH),
        w_out_ab=w_out_ab[0].astype(BF16),
        w_in_c=w_in_c[0].astype(BF16),
        conv_w=conv_w[0],
        w_out_c=w_out_c[0].astype(BF16),
        ln_g=ln_g,
        ln_b=ln_b,
    )
    return (_trunk(x_prompt, **params), _trunk(x_sample, **params))
```

```python
import functools
import math

import numpy as np
import jax
import jax.numpy as jnp
from jax import lax
from jax.experimental import pallas as pl
from jax.experimental.pallas import tpu as pltpu

D_MODEL = 2048
DEPTH = 2
ATT_HEADS = 16
HEAD_DIM = 128
ATT_WIDTH = ATT_HEADS * HEAD_DIM
DILATED_PATTERNS = ((128, 1), (512, 4), (2048, 16))
DILATIONS = tuple(d for _, d in DILATED_PATTERNS)
RADIUS = 64
POOL_WINDOWS = (2, 4, 8, 16)
POOL_WIDTH = D_MODEL // 2
POOL_GROUP = POOL_WIDTH // len(POOL_WINDOWS)
CONV_WIDTH = D_MODEL
REL_BUCKETS = 32
REL_MAX_DISTANCE = 1024
DN_ALPHA = (2 * DEPTH) ** 0.25
LN_EPS = 1e-5
NEG_INF = -1e30
LOG2E = math.log2(math.e)

LANES = 128
HALO = 16
ATT_Q_ROWS = 128
ATT_KEYS = ATT_Q_ROWS + 2 * RADIUS
ATT_UNITS = 4
ATT_TRIP_STEPS = 32
ATT_BLOCK_ROWS = 1024
ATT_MIN_STEP_ROWS = 512
PROJ_ROWS = 256
ROW_CHUNK = 128
VMEM_LIMIT = 56 * 1024 * 1024

BF16 = jnp.bfloat16
F32 = jnp.float32

assert all(w // (2 * d) == RADIUS for w, d in DILATED_PATTERNS)


def _t5_bucket(rel):
    nb = REL_BUCKETS // 2
    max_exact = nb // 2
    ret = np.where(rel > 0, nb, 0)
    n = np.abs(rel)
    n_safe = np.maximum(n, 1).astype(np.float64)
    large = max_exact + (np.log(n_safe / max_exact) / math.log(REL_MAX_DISTANCE / max_exact)
                         * (nb - max_exact)).astype(np.int64)
    large = np.minimum(large, nb - 1)
    return (ret + np.where(n < max_exact, n, large)).astype(np.int32)


def _band_bias(rel_bias, dilation):
    rel = np.arange(ATT_KEYS)[None, :] - RADIUS - np.arange(ATT_Q_ROWS)[:, None]
    bucket = jnp.asarray(_t5_bucket(rel * dilation).reshape(1, -1))
    onehot = (bucket == jnp.arange(REL_BUCKETS, dtype=jnp.int32)[:, None]).astype(F32)
    bias = jnp.dot(rel_bias.astype(F32).T, onehot, precision=lax.Precision.HIGHEST)
    bias = bias.reshape(1, ATT_HEADS, ATT_Q_ROWS, ATT_KEYS)
    key = np.arange(ATT_KEYS)[None, :]
    in_band = np.abs(rel) <= RADIUS
    after_start = key >= RADIUS
    before_end = key < ATT_Q_ROWS + RADIUS
    keep = np.stack([in_band, in_band & after_start, in_band & before_end, in_band & after_start & before_end])
    return jnp.where(jnp.asarray(keep[:, None]), bias * LOG2E, NEG_INF)


def _silu(x):
    return x / (1.0 + jnp.exp(-x))


def _layernorm(z, g, b):
    mu = jnp.mean(z, axis=-1, keepdims=True)
    zc = z - mu
    var = jnp.mean(zc * zc, axis=-1, keepdims=True)
    return zc * lax.rsqrt(var + LN_EPS) * g + b


def _qkvg_kernel(x_ref, w_ref, o1_ref, o4_ref, o16_ref, ga_ref, xb_ref, s0_ref, s1_ref, mid0_ref, mid1_ref,
                 *, tm, tn, nq):
    j = pl.program_id(2)
    chunk_bufs = ((s0_ref, mid0_ref), (s1_ref, mid1_ref))
    r4, r16 = PROJ_ROWS // 4, PROJ_ROWS // 16

    @pl.when(j == 0)
    def _():
        xb_ref[...] = x_ref[...].astype(BF16)

    def product(rc, cc):
        rows = slice(rc * PROJ_ROWS, (rc + 1) * PROJ_ROWS)
        cols = slice(cc * 2 * LANES, (cc + 1) * 2 * LANES)
        return jnp.dot(xb_ref[rows, :], w_ref[:, cols], preferred_element_type=F32)

    def emit_qkv(acc, rc, cc, s_ref, mid_ref):
        acc = acc * jnp.where(j < ATT_WIDTH // tn, LOG2E * HEAD_DIM ** -0.5, 1.0)
        for c in range(2):
            slab = 2 * cc + c
            blk = acc[:, c * LANES:(c + 1) * LANES]
            o1_ref[0, slab, rc * PROJ_ROWS:(rc + 1) * PROJ_ROWS, :] = blk.astype(BF16)
            s_ref[c] = blk
            for g4 in range(4):
                rows4 = s_ref[c, pl.ds(g4, r4, stride=4), :]
                o4_ref[g4, slab, rc * r4:(rc + 1) * r4, :] = rows4.astype(BF16)
                mid_ref[c, g4] = rows4
            for g4 in range(4):
                for k in range(4):
                    o16_ref[g4 + 4 * k, slab, rc * r16:(rc + 1) * r16, :] = (
                        mid_ref[c, g4, pl.ds(k, r16, stride=4), :].astype(BF16))

    def emit_gate(acc, rc, cc):
        ga_ref[rc * PROJ_ROWS:(rc + 1) * PROJ_ROWS, cc * 2 * LANES:(cc + 1) * 2 * LANES] = _silu(acc).astype(BF16)

    chunks = [(rc, cc) for cc in range(tn // (2 * LANES)) for rc in range(tm // PROJ_ROWS)]

    @pl.when(j < nq)
    def _():
        for n, (rc, cc) in enumerate(chunks):
            emit_qkv(product(rc, cc), rc, cc, *chunk_bufs[n % 2])

    @pl.when(j >= nq)
    def _():
        for rc, cc in chunks:
            emit_gate(product(rc, cc), rc, cc)


def _qkvg_proj(x, w, *, tm=1024, tn=1024):
    b, s, dm = x.shape
    nj = 4 * ATT_WIDTH // tn
    nq = 3 * ATT_WIDTH // tn
    qt = ATT_WIDTH // tn
    nslab = 3 * ATT_WIDTH // LANES
    outs = [jax.ShapeDtypeStruct((b, d, nslab, s // d, LANES), BF16) for d in DILATIONS]
    outs.append(jax.ShapeDtypeStruct((b, s, ATT_WIDTH), BF16))
    out_specs = [pl.BlockSpec((None, d, tn // LANES, tm // d, LANES),
                              lambda bi, i, j: (bi, 0, (jnp.minimum(j, nq - 1) + 2 * qt) % nq, i, 0)) for d in DILATIONS]
    out_specs.append(pl.BlockSpec((None, tm, tn), lambda bi, i, j: (bi, i, jnp.maximum(j - nq, 0))))
    return pl.pallas_call(
        functools.partial(_qkvg_kernel, tm=tm, tn=tn, nq=nq),
        grid=(b, s // tm, nj),
        in_specs=[pl.BlockSpec((None, tm, dm), lambda bi, i, j: (bi, i, 0)),
                  pl.BlockSpec((dm, tn), lambda bi, i, j: (0, j))],
        out_specs=out_specs,
        out_shape=outs,
        scratch_shapes=[pltpu.VMEM((tm, dm), BF16)]
        + [pltpu.VMEM((2, PROJ_ROWS, LANES), F32)] * 2
        + [pltpu.VMEM((2, 4, PROJ_ROWS // 4, LANES), F32)] * 2,
        compiler_params=pltpu.CompilerParams(
            dimension_semantics=("parallel", "parallel", "arbitrary"), vmem_limit_bytes=VMEM_LIMIT),
        name="qkvg_proj",
    )(x, w)


def _pool_kernel(xp_ref, x_ref, xn_ref, wu_ref, wg_ref, pw_ref, ps_ref, o_ref, xb_ref, u_ref, *, tm, seq):
    i = pl.program_id(0)
    tiles_per_seq = seq // tm
    keep_prev = jnp.where((i % tiles_per_seq) == 0, 0.0, 1.0)
    keep_next = jnp.where((i % tiles_per_seq) == tiles_per_seq - 1, 0.0, 1.0)
    nch = tm // ROW_CHUNK

    xb_ref[0:HALO] = xp_ref[...].astype(BF16)
    xb_ref[HALO:HALO + tm] = x_ref[...].astype(BF16)
    xb_ref[HALO + tm:] = xn_ref[...].astype(BF16)

    def pool_input(r):
        lo = HALO + r * ROW_CHUNK - (HALO if r == 0 else 0)
        hi = HALO + (r + 1) * ROW_CHUNK + (HALO if r == nch - 1 else 0)
        u_ref[lo:hi] = jnp.dot(xb_ref[lo:hi], wu_ref[...], preferred_element_type=F32)
        if r == 0:
            u_ref[0:HALO] = u_ref[0:HALO] * keep_prev
        if r == nch - 1:
            u_ref[HALO + tm:] = u_ref[HALO + tm:] * keep_next

    def pooled_out(r):
        base = HALO + r * ROW_CHUNK
        rows = slice(r * ROW_CHUNK, (r + 1) * ROW_CHUNK)
        gate = jnp.dot(xb_ref[base:base + ROW_CHUNK], wg_ref[...], preferred_element_type=F32)
        pos = ((i % tiles_per_seq) * tm + r * ROW_CHUNK
               + lax.broadcasted_iota(jnp.int32, (ROW_CHUNK, POOL_GROUP), 0))
        for gi, w in enumerate(POOL_WINDOWS):
            cols = slice(gi * POOL_GROUP, (gi + 1) * POOL_GROUP)
            tot = u_ref[pl.ds(base - w // 2, ROW_CHUNK), cols]
            for off in range(-w // 2 + 1, w // 2):
                tot = tot + u_ref[pl.ds(base + off, ROW_CHUNK), cols]
            lo = jnp.maximum(pos - w // 2, 0)
            hi = jnp.minimum(pos + w // 2 - 1, seq - 1)
            cnt = (hi - lo + 1).astype(F32)
            pooled = tot / cnt - u_ref[pl.ds(base, ROW_CHUNK), cols]
            ob = jnp.dot(pooled.astype(BF16), pw_ref[gi], preferred_element_type=F32)
            o_ref[rows, cols] = (ob * ps_ref[:, cols] * _silu(gate[:, cols])).astype(BF16)

    pool_input(0)
    for r in range(nch):
        if r + 1 < nch:
            pool_input(r + 1)
        pooled_out(r)


def _pool_branch(x2, w_ab, pool_w, pool_scale, *, seq, tm=512):
    m, dm = x2.shape
    ub = 4 * ATT_WIDTH // POOL_WIDTH
    hb = tm // HALO
    nhb = m // HALO
    return pl.pallas_call(
        functools.partial(_pool_kernel, tm=tm, seq=seq),
        grid=(m // tm,),
        in_specs=[pl.BlockSpec((HALO, dm), lambda i: (jnp.maximum(i * hb - 1, 0), 0)),
                  pl.BlockSpec((tm, dm), lambda i: (i, 0)),
                  pl.BlockSpec((HALO, dm), lambda i: (jnp.minimum((i + 1) * hb, nhb - 1), 0)),
                  pl.BlockSpec((dm, POOL_WIDTH), lambda i: (0, ub)),
                  pl.BlockSpec((dm, POOL_WIDTH), lambda i: (0, ub + 1)),
                  pl.BlockSpec((len(POOL_WINDOWS), POOL_GROUP, POOL_GROUP), lambda i: (0, 0, 0)),
                  pl.BlockSpec((1, POOL_WIDTH), lambda i: (0, 0))],
        out_specs=pl.BlockSpec((tm, POOL_WIDTH), lambda i: (i, 0)),
        out_shape=jax.ShapeDtypeStruct((m, POOL_WIDTH), BF16),
        scratch_shapes=[pltpu.VMEM((tm + 2 * HALO, dm), BF16), pltpu.VMEM((tm + 2 * HALO, POOL_WIDTH), F32)],
        compiler_params=pltpu.CompilerParams(dimension_semantics=("parallel",), vmem_limit_bytes=VMEM_LIMIT),
        name="pool_branch",
    )(x2, x2, x2, w_ab, w_ab, pool_w, pool_scale)


def _attn_kernel(q_ref, kvp_ref, kvm_ref, kvn_ref, bias_ref, o_ref, st_ref, s0_ref, s1_ref, *, qb, nblk, ns):
    n = pl.program_id(1)
    na = qb // ATT_Q_ROWS
    ng = max(1, na // ATT_UNITS)
    apg = na // ng
    hpi = max(1, ATT_UNITS // na)
    steps = (ATT_HEADS // hpi) * ng
    all_steps = ns * steps
    trip = min(ATT_TRIP_STEPS, all_steps)
    assert all_steps % trip == 0 and trip % (2 * ng) == 0 and steps % ng == 0
    lane = lax.broadcasted_iota(jnp.int32, (ATT_Q_ROWS, LANES), 1)
    st_ref[...] = jnp.broadcast_to(
        jnp.where((lane[:1] >= ATT_HEADS) & (lane[:1] < 2 * ATT_HEADS), 1.0, 0.0), st_ref.shape)
    first = (n == 0).astype(jnp.int32)
    last = (n == nblk - 1).astype(jnp.int32)
    s_refs = (s0_ref, s1_ref)
    ones = jnp.ones((ATT_KEYS, LANES), BF16)

    def window(si, slab, a):
        lo = a * ATT_Q_ROWS - RADIUS
        hi = lo + ATT_KEYS
        parts = []
        if lo < 0:
            parts.append(kvp_ref[si, slab])
        parts.append(kvm_ref[si, slab, max(lo, 0):min(hi, qb), :])
        if hi > qb:
            parts.append(kvn_ref[si, slab])
        return parts[0] if len(parts) == 1 else jnp.concatenate(parts, axis=0)

    def scores(si, step, grp, slot):
        for j in range(hpi):
            h = (step // ng) * hpi + j
            for ai in range(apg):
                a = grp * apg + ai
                q = q_ref[si, h, a * ATT_Q_ROWS:(a + 1) * ATT_Q_ROWS, :]
                s_refs[slot][j * apg + ai] = lax.dot_general(
                    q, window(si, h, a), (((1,), (1,)), ((), ())), preferred_element_type=F32)

    def softmax_pv(si, step, grp, slot):
        for j in range(hpi):
            h = (step // ng) * hpi + j
            for ai in range(apg):
                a = grp * apg + ai
                rows = slice(a * ATT_Q_ROWS, (a + 1) * ATT_Q_ROWS)
                variant = (first if a == 0 else 0) + (2 * last if a == na - 1 else 0)
                s = s_refs[slot][j * apg + ai] + bias_ref[variant, h]
                mx = jnp.max(s, axis=-1, keepdims=True)
                p = jnp.exp2(s - mx)
                v = window(si, ATT_HEADS + h, a)
                pv = jnp.dot(p.astype(BF16), jnp.concatenate([v, ones], axis=1), preferred_element_type=F32)
                o_ref[si, h, rows, :] = pv[:, :HEAD_DIM].astype(BF16)
                st_ref[si, rows, :] = jnp.where(
                    lane == h, mx, jnp.where(lane == h + ATT_HEADS, pv[:, HEAD_DIM:], st_ref[si, rows, :]))

    def locate(g):
        return (g // steps, g % steps) if ns > 1 else (0, g)

    scores(0, 0, 0, 0)

    def body(tt, c):
        for u in range(trip):
            g = trip * tt + u
            scores(*locate((g + 1) % all_steps), (u + 1) % ng, (u + 1) % 2)
            softmax_pv(*locate(g), u % ng, u % 2)
        return c

    lax.fori_loop(0, all_steps // trip, body, 0)


def _banded_attention(hd, bias):
    nseq, _, length, _ = hd.shape
    qb = min(ATT_BLOCK_ROWS, length)
    nblk = length // qb
    ns = max(1, ATT_MIN_STEP_ROWS // length)
    hb = qb // RADIUS
    nh = length // RADIUS
    units = max(1, ATT_UNITS // (qb // ATT_Q_ROWS)) * min(ATT_UNITS, qb // ATT_Q_ROWS)
    return pl.pallas_call(
        functools.partial(_attn_kernel, qb=qb, nblk=nblk, ns=ns),
        grid=(nseq // ns, nblk),
        in_specs=[pl.BlockSpec((ns, ATT_HEADS, qb, LANES), lambda s, n: (s, 2, n, 0)),
                  pl.BlockSpec((ns, 2 * ATT_HEADS, RADIUS, LANES),
                               lambda s, n: (s, 0, jnp.maximum(n * hb - 1, 0), 0)),
                  pl.BlockSpec((ns, 2 * ATT_HEADS, qb, LANES), lambda s, n: (s, 0, n, 0)),
                  pl.BlockSpec((ns, 2 * ATT_HEADS, RADIUS, LANES),
                               lambda s, n: (s, 0, jnp.minimum((n + 1) * hb, nh - 1), 0)),
                  pl.BlockSpec(bias.shape, lambda s, n: (0, 0, 0, 0), pipeline_mode=pl.Buffered(1))],
        out_specs=[pl.BlockSpec((ns, ATT_HEADS, qb, LANES), lambda s, n: (s, 0, n, 0)),
                   pl.BlockSpec((ns, qb, LANES), lambda s, n: (s, n, 0))],
        out_shape=[jax.ShapeDtypeStruct((nseq, ATT_HEADS, length, LANES), BF16),
                   jax.ShapeDtypeStruct((nseq, length, LANES), F32)],
        scratch_shapes=[pltpu.VMEM((units, ATT_Q_ROWS, ATT_KEYS), F32)] * 2,
        compiler_params=pltpu.CompilerParams(
            dimension_semantics=("parallel", "parallel"), vmem_limit_bytes=VMEM_LIMIT),
        name="banded_attention",
    )(hd, hd, hd, hd, bias)


def _merge_kernel(o1_ref, o4_ref, o16_ref, s1_ref, s4_ref, s16_ref,
                  ga_ref, ob_ref, x_ref, w_ref, g_ref, b_ref, out_ref, outb_ref,
                  y0_ref, y1_ref, st_ref, r_ref, f_ref, *, tm, total):
    t = pl.program_id(0)
    ys = (y0_ref, y1_ref)
    ncol = w_ref.shape[1] // (2 * LANES)
    hpp = ATT_HEADS // ncol

    def merge_weights():
        st_ref[0] = s1_ref[0]
        for p, (d, sr) in enumerate(((4, s4_ref), (16, s16_ref)), start=1):
            for g in range(d):
                st_ref[p, pl.ds(g, tm // d, stride=d), :] = sr[g]
        ms = [st_ref[p] for p in range(3)]
        mx = jnp.maximum(jnp.maximum(ms[0], ms[1]), ms[2])
        es = [jnp.exp2(m - mx) for m in ms]
        den = sum(e * pltpu.roll(m, LANES - ATT_HEADS, axis=1) for e, m in zip(es, ms))
        head_lane = lax.broadcasted_iota(jnp.int32, den.shape, 1) < ATT_HEADS
        den = jnp.where(head_lane, den, 1.0)
        for p in range(3):
            st_ref[p] = es[p] / den

    def merge_head(dst, h):
        for p, (d, o_ref) in enumerate(((4, o4_ref), (16, o16_ref))):
            for g in range(d):
                r_ref[p, pl.ds(g, tm // d, stride=d), :] = o_ref[g, h].astype(F32)
        oa = (st_ref[0, :, h:h + 1] * o1_ref[0, h].astype(F32)
              + st_ref[1, :, h:h + 1] * r_ref[0]
              + st_ref[2, :, h:h + 1] * r_ref[1])
        cols = slice(h * HEAD_DIM, (h + 1) * HEAD_DIM)
        dst[:, cols] = (oa * ga_ref[:, cols].astype(F32)).astype(BF16)

    def product(src, k):
        cols = slice(k * 2 * LANES, (k + 1) * 2 * LANES)
        f_ref[:, cols] = jnp.dot(src[...], w_ref[:, cols], preferred_element_type=F32)
        if k == ncol - 1:
            for r in range(0, tm, ROW_CHUNK):
                rows = slice(r, r + ROW_CHUNK)
                y = _layernorm(DN_ALPHA * x_ref[rows, :] + f_ref[rows, :], g_ref[...], b_ref[...])
                out_ref[rows, :] = y
                outb_ref[rows, :] = y.astype(BF16)

    @pl.when(t == 0)
    def _():
        merge_weights()
        for h in range(ATT_HEADS):
            merge_head(ys[0], h)
        ys[0][:, ATT_WIDTH:] = ob_ref[...]

    @pl.when(t == total)
    def _():
        for k in range(ncol):
            product(ys[(total - 1) % 2], k)

    for parity in (0, 1):
        @pl.when((t > 0) & (t < total) & (t % 2 == parity))
        def _(parity=parity):
            merge_weights()
            for k in range(ncol):
                product(ys[1 - parity], k)
                for h in range(k * hpp, (k + 1) * hpp):
                    merge_head(ys[parity], h)
            ys[parity][:, ATT_WIDTH:] = ob_ref[...]


def _merge_outproj(o_pats, stats, ga, ob, x, w_out, ln_g, ln_b, *, tm=256):
    b, s, dm = x.shape
    nt = s // tm
    total = b * nt
    ga = ga.reshape(b, s, ATT_WIDTH)
    ob = ob.reshape(b, s, POOL_WIDTH)

    def cur(t):
        t = jnp.minimum(t, total - 1)
        return t // nt, t % nt

    def prev(t):
        t = jnp.maximum(t - 1, 0)
        return t // nt, t % nt

    def o_map(t):
        bi, i = cur(t)
        return bi, 0, 0, i, 0

    def st_map(t):
        bi, i = cur(t)
        return bi, 0, i, 0

    def cur_map(t):
        bi, i = cur(t)
        return bi, i, 0

    def prev_map(t):
        bi, i = prev(t)
        return bi, i, 0

    o_specs = [pl.BlockSpec((None, d, ATT_HEADS, tm // d, LANES), o_map) for d in DILATIONS]
    st_specs = [pl.BlockSpec((None, d, tm // d, LANES), st_map) for d in DILATIONS]
    const2 = lambda t: (0, 0)
    args = list(o_pats) + list(stats)
    return pl.pallas_call(
        functools.partial(_merge_kernel, tm=tm, total=total),
        grid=(total + 1,),
        in_specs=o_specs + st_specs + [
            pl.BlockSpec((None, tm, ATT_WIDTH), cur_map),
            pl.BlockSpec((None, tm, POOL_WIDTH), cur_map),
            pl.BlockSpec((None, tm, dm), prev_map),
            pl.BlockSpec(w_out.shape, const2, pipeline_mode=pl.Buffered(1)),
            pl.BlockSpec((1, dm), const2),
            pl.BlockSpec((1, dm), const2)],
        out_specs=[pl.BlockSpec((None, tm, dm), prev_map), pl.BlockSpec((None, tm, dm), prev_map)],
        out_shape=[jax.ShapeDtypeStruct((b, s, dm), F32), jax.ShapeDtypeStruct((b, s, dm), BF16)],
        scratch_shapes=[pltpu.VMEM((tm, ATT_WIDTH + POOL_WIDTH), BF16),
                        pltpu.VMEM((tm, ATT_WIDTH + POOL_WIDTH), BF16),
                        pltpu.VMEM((3, tm, LANES), F32),
                        pltpu.VMEM((2, tm, LANES), F32),
                        pltpu.VMEM((tm, dm), F32)],
        compiler_params=pltpu.CompilerParams(dimension_semantics=("arbitrary",), vmem_limit_bytes=VMEM_LIMIT),
        name="merge_outproj",
    )(*args, ga, ob, x, w_out, ln_g, ln_b)


def _conv_in_kernel(xp_ref, x_ref, xn_ref, wb_ref, wc_ref, wv_ref, wg_ref, cw_ref, o_ref, cv_ref, *, tm, seq):
    i = pl.program_id(0)
    tiles_per_seq = seq // tm
    keep_prev = jnp.where((i % tiles_per_seq) == 0, 0.0, 1.0)
    keep_next = jnp.where((i % tiles_per_seq) == tiles_per_seq - 1, 0.0, 1.0)
    nch = tm // ROW_CHUNK

    def conv_input(r):
        parts = [x_ref[r * ROW_CHUNK:(r + 1) * ROW_CHUNK, :]]
        if r == 0:
            parts.insert(0, xp_ref[...])
        if r == nch - 1:
            parts.append(xn_ref[...])
        xa = parts[0] if len(parts) == 1 else jnp.concatenate(parts, axis=0)
        lo = HALO + r * ROW_CHUNK - (HALO if r == 0 else 0)
        cv_ref[lo:lo + xa.shape[0]] = (jnp.dot(xa, wc_ref[...], preferred_element_type=F32)
                                       * jnp.dot(xa, wv_ref[...], preferred_element_type=F32))
        if r == 0:
            cv_ref[0:HALO] = cv_ref[0:HALO] * keep_prev
        if r == nch - 1:
            cv_ref[HALO + tm:] = cv_ref[HALO + tm:] * keep_next

    def gated_conv(r):
        base = HALO + r * ROW_CHUNK
        xm = x_ref[r * ROW_CHUNK:(r + 1) * ROW_CHUNK, :]
        gb = jnp.dot(xm, wb_ref[...], preferred_element_type=F32)
        gate = jnp.dot(xm, wg_ref[...], preferred_element_type=F32)
        conv = (cv_ref[pl.ds(base - 1, ROW_CHUNK), :] * cw_ref[0:1, :]
                + cv_ref[pl.ds(base, ROW_CHUNK), :] * cw_ref[1:2, :]
                + cv_ref[pl.ds(base + 1, ROW_CHUNK), :] * cw_ref[2:3, :])
        o_ref[r * ROW_CHUNK:(r + 1) * ROW_CHUNK, :] = (gb * conv * _silu(gate)).astype(BF16)

    conv_input(0)
    for r in range(nch):
        if r + 1 < nch:
            conv_input(r + 1)
        gated_conv(r)


def _conv_inproj(xb, w_in, conv_w, *, seq, tm=1024, tc=1024):
    m, dm = xb.shape
    hb = tm // HALO
    nhb = m // HALO
    nc = CONV_WIDTH // tc

    def wspec(part):
        return pl.BlockSpec((dm, tc), lambda i, c: (0, part * nc + c))

    return pl.pallas_call(
        functools.partial(_conv_in_kernel, tm=tm, seq=seq),
        grid=(m // tm, nc),
        in_specs=[pl.BlockSpec((HALO, dm), lambda i, c: (jnp.maximum(i * hb - 1, 0), 0)),
                  pl.BlockSpec((tm, dm), lambda i, c: (i, 0)),
                  pl.BlockSpec((HALO, dm), lambda i, c: (jnp.minimum((i + 1) * hb, nhb - 1), 0)),
                  wspec(0), wspec(1), wspec(2), wspec(3),
                  pl.BlockSpec((3, tc), lambda i, c: (0, c))],
        out_specs=pl.BlockSpec((tm, tc), lambda i, c: (i, c)),
        out_shape=jax.ShapeDtypeStruct((m, CONV_WIDTH), BF16),
        scratch_shapes=[pltpu.VMEM((tm + 2 * HALO, tc), F32)],
        compiler_params=pltpu.CompilerParams(
            dimension_semantics=("parallel", "parallel"), vmem_limit_bytes=VMEM_LIMIT),
        name="conv_inproj",
    )(xb, xb, xb, w_in, w_in, w_in, w_in, conv_w)


def _outproj_ln_kernel(y_ref, x_ref, w_ref, g_ref, b_ref, o_ref):
    for r in range(0, y_ref.shape[0], ROW_CHUNK):
        rows = slice(r, r + ROW_CHUNK)
        f = jnp.dot(y_ref[rows, :], w_ref[...], preferred_element_type=F32)
        o_ref[rows, :] = _layernorm(DN_ALPHA * x_ref[rows, :] + f, g_ref[...], b_ref[...])


def _outproj_ln(y, x2, w_out, ln_g, ln_b, *, tm=512):
    m, dm = x2.shape
    const2 = lambda i: (0, 0)
    return pl.pallas_call(
        _outproj_ln_kernel,
        grid=(m // tm,),
        in_specs=[pl.BlockSpec((tm, y.shape[1]), lambda i: (i, 0)),
                  pl.BlockSpec((tm, dm), lambda i: (i, 0)),
                  pl.BlockSpec(w_out.shape, const2, pipeline_mode=pl.Buffered(1)),
                  pl.BlockSpec((1, dm), const2),
                  pl.BlockSpec((1, dm), const2)],
        out_specs=pl.BlockSpec((tm, dm), lambda i: (i, 0)),
        out_shape=jax.ShapeDtypeStruct((m, dm), F32),
        compiler_params=pltpu.CompilerParams(dimension_semantics=("parallel",), vmem_limit_bytes=VMEM_LIMIT),
        name="outproj_ln",
    )(y, x2, w_out, ln_g, ln_b)


def _trunk(x, biases, w_ab, pool_w, pool_scale, w_out_ab, w_in_c, conv_w, w_out_c, ln_g, ln_b):
    b, s, dm = x.shape
    x2 = x.reshape(b * s, dm)
    *hds, ga = _qkvg_proj(x, w_ab)
    ob = _pool_branch(x2, w_ab, pool_w, pool_scale, seq=s)
    o_pats, stats = [], []
    for d, hd, bias in zip(DILATIONS, hds, biases):
        ld = s // d
        o, st = _banded_attention(hd.reshape(b * d, 3 * ATT_HEADS, ld, LANES), bias)
        o_pats.append(o.reshape(b, d, ATT_HEADS, ld, LANES))
        stats.append(st.reshape(b, d, ld, LANES))
    x1, x1b = _merge_outproj(o_pats, stats, ga, ob, x, w_out_ab, ln_g[0:1], ln_b[0:1])
    y = _conv_inproj(x1b.reshape(b * s, dm), w_in_c, conv_w, seq=s)
    out = _outproj_ln(y, x1.reshape(b * s, dm), w_out_c, ln_g[1:2], ln_b[1:2])
    return out.reshape(b, s, dm)


def kernel(x_prompt, x_sample, rel_bias, w_in_ab, pool_w, pool_scale, w_out_ab, w_in_c, conv_w, w_out_c, ln_g, ln_b):
    assert DEPTH == 2 and w_in_ab.shape[0] == 1 and w_in_c.shape[0] == 1
    params = dict(
        biases=[_band_bias(rel_bias, d) for d in DILATIONS],
        w_ab=w_in_ab[0].astype(BF16),
        pool_w=pool_w[0].astype(BF16),
        pool_scale=pool_scale[0].reshape(1, POOL_WIDTH),
        w_out_ab=w_out_ab[0].astype(BF16),
        w_in_c=w_in_c[0].astype(BF16),
        conv_w=conv_w[0],
        w_out_c=w_out_c[0].astype(BF16),
        ln_g=ln_g,
        ln_b=ln_b,
    )
    return (_trunk(x_prompt, **params), _trunk(x_sample, **params))
```

```python
import functools
import math

import numpy as np
import jax
import jax.numpy as jnp
from jax import lax
from jax.experimental import pallas as pl
from jax.experimental.pallas import tpu as pltpu

D_MODEL = 2048
DEPTH = 2
ATT_HEADS = 16
HEAD_DIM = 128
ATT_WIDTH = ATT_HEADS * HEAD_DIM
DILATED_PATTERNS = ((128, 1), (512, 4), (2048, 16))
DILATIONS = tuple(d for _, d in DILATED_PATTERNS)
RADIUS = 64
POOL_WINDOWS = (2, 4, 8, 16)
POOL_WIDTH = D_MODEL // 2
POOL_GROUP = POOL_WIDTH // len(POOL_WINDOWS)
CONV_WIDTH = D_MODEL
REL_BUCKETS = 32
REL_MAX_DISTANCE = 1024
DN_ALPHA = (2 * DEPTH) ** 0.25
LN_EPS = 1e-5
NEG_INF = -1e30
LOG2E = math.log2(math.e)

LANES = 128
MXU_COLS = 256
SLABS_PER_CHUNK = MXU_COLS // LANES
HALO = 16
ATT_Q_ROWS = 128
ATT_KEYS = ATT_Q_ROWS + 2 * RADIUS
ATT_UNITS = 4
ATT_TRIP_STEPS = 32
ATT_BLOCK_ROWS = 1024
ATT_MIN_STEP_ROWS = 512
PROJ_ROWS = 256
ROW_CHUNK = 128
VMEM_LIMIT = 56 * 1024 * 1024

BF16 = jnp.bfloat16
F32 = jnp.float32

assert all(w // (2 * d) == RADIUS for w, d in DILATED_PATTERNS)


def _t5_bucket(rel):
    nb = REL_BUCKETS // 2
    max_exact = nb // 2
    ret = np.where(rel > 0, nb, 0)
    n = np.abs(rel)
    n_safe = np.maximum(n, 1).astype(np.float64)
    large = max_exact + (np.log(n_safe / max_exact) / math.log(REL_MAX_DISTANCE / max_exact)
                         * (nb - max_exact)).astype(np.int64)
    large = np.minimum(large, nb - 1)
    return (ret + np.where(n < max_exact, n, large)).astype(np.int32)


def _band_bias(rel_bias, dilation):
    rel = np.arange(ATT_KEYS)[None, :] - RADIUS - np.arange(ATT_Q_ROWS)[:, None]
    bucket = jnp.asarray(_t5_bucket(rel * dilation).reshape(1, -1))
    onehot = (bucket == jnp.arange(REL_BUCKETS, dtype=jnp.int32)[:, None]).astype(F32)
    bias = jnp.dot(rel_bias.astype(F32).T, onehot, precision=lax.Precision.HIGHEST)
    bias = bias.reshape(1, ATT_HEADS, ATT_Q_ROWS, ATT_KEYS)
    key = np.arange(ATT_KEYS)[None, :]
    in_band = np.abs(rel) <= RADIUS
    after_start = key >= RADIUS
    before_end = key < ATT_Q_ROWS + RADIUS
    keep = np.stack([in_band, in_band & after_start, in_band & before_end, in_band & after_start & before_end])
    return jnp.where(jnp.asarray(keep[:, None]), bias * LOG2E, NEG_INF)


def _silu(x):
    return x / (1.0 + jnp.exp(-x))


def _layernorm(z, g, b):
    mu = jnp.mean(z, axis=-1, keepdims=True)
    zc = z - mu
    var = jnp.mean(zc * zc, axis=-1, keepdims=True)
    return zc * lax.rsqrt(var + LN_EPS) * g + b


def _qkvg_kernel(x_ref, w_ref, o1_ref, o4_ref, o16_ref, ga_ref, xb_ref, s0_ref, s1_ref, mid0_ref, mid1_ref,
                 *, tm, tn, nq):
    j = pl.program_id(2)
    chunk_bufs = ((s0_ref, mid0_ref), (s1_ref, mid1_ref))
    r4, r16 = PROJ_ROWS // 4, PROJ_ROWS // 16

    @pl.when(j == 0)
    def _():
        xb_ref[...] = x_ref[...].astype(BF16)

    def product(rc, cc):
        rows = slice(rc * PROJ_ROWS, (rc + 1) * PROJ_ROWS)
        cols = slice(cc * MXU_COLS, (cc + 1) * MXU_COLS)
        return jnp.dot(xb_ref[rows, :], w_ref[:, cols], preferred_element_type=F32)

    def emit_qkv(acc, rc, cc, s_ref, mid_ref):
        acc = acc * jnp.where(j < ATT_WIDTH // tn, LOG2E * HEAD_DIM ** -0.5, 1.0)
        for c in range(SLABS_PER_CHUNK):
            slab = SLABS_PER_CHUNK * cc + c
            blk = acc[:, c * LANES:(c + 1) * LANES]
            o1_ref[0, slab, rc * PROJ_ROWS:(rc + 1) * PROJ_ROWS, :] = blk.astype(BF16)
            s_ref[c] = blk
            for g4 in range(4):
                rows4 = s_ref[c, pl.ds(g4, r4, stride=4), :]
                o4_ref[g4, slab, rc * r4:(rc + 1) * r4, :] = rows4.astype(BF16)
                mid_ref[c, g4] = rows4
            for g4 in range(4):
                for k in range(4):
                    o16_ref[g4 + 4 * k, slab, rc * r16:(rc + 1) * r16, :] = (
                        mid_ref[c, g4, pl.ds(k, r16, stride=4), :].astype(BF16))

    def emit_gate(acc, rc, cc):
        ga_ref[rc * PROJ_ROWS:(rc + 1) * PROJ_ROWS, cc * MXU_COLS:(cc + 1) * MXU_COLS] = _silu(acc).astype(BF16)

    chunks = [(rc, cc) for cc in range(tn // MXU_COLS) for rc in range(tm // PROJ_ROWS)]

    @pl.when(j < nq)
    def _():
        for n, (rc, cc) in enumerate(chunks):
            emit_qkv(product(rc, cc), rc, cc, *chunk_bufs[n % 2])

    @pl.when(j >= nq)
    def _():
        for rc, cc in chunks:
            emit_gate(product(rc, cc), rc, cc)


def _qkvg_proj(x, w, *, tm=1024, tn=1024):
    b, s, dm = x.shape
    nj = 4 * ATT_WIDTH // tn
    nq = 3 * ATT_WIDTH // tn
    qt = ATT_WIDTH // tn
    nslab = 3 * ATT_WIDTH // LANES
    outs = [jax.ShapeDtypeStruct((b, d, nslab, s // d, LANES), BF16) for d in DILATIONS]
    outs.append(jax.ShapeDtypeStruct((b, s, ATT_WIDTH), BF16))
    out_specs = [pl.BlockSpec((None, d, tn // LANES, tm // d, LANES),
                              lambda bi, i, j: (bi, 0, (jnp.minimum(j, nq - 1) + 2 * qt) % nq, i, 0)) for d in DILATIONS]
    out_specs.append(pl.BlockSpec((None, tm, tn), lambda bi, i, j: (bi, i, jnp.maximum(j - nq, 0))))
    return pl.pallas_call(
        functools.partial(_qkvg_kernel, tm=tm, tn=tn, nq=nq),
        grid=(b, s // tm, nj),
        in_specs=[pl.BlockSpec((None, tm, dm), lambda bi, i, j: (bi, i, 0)),
                  pl.BlockSpec((dm, tn), lambda bi, i, j: (0, j))],
        out_specs=out_specs,
        out_shape=outs,
        scratch_shapes=[pltpu.VMEM((tm, dm), BF16)]
        + [pltpu.VMEM((SLABS_PER_CHUNK, PROJ_ROWS, LANES), F32)] * 2
        + [pltpu.VMEM((SLABS_PER_CHUNK, 4, PROJ_ROWS // 4, LANES), F32)] * 2,
        compiler_params=pltpu.CompilerParams(
            dimension_semantics=("parallel", "parallel", "arbitrary"), vmem_limit_bytes=VMEM_LIMIT),
        name="qkvg_proj",
    )(x, w)


def _pool_kernel(xp_ref, x_ref, xn_ref, wu_ref, wg_ref, pw_ref, ps_ref, o_ref, xb_ref, u_ref, *, tm, seq):
    i = pl.program_id(0)
    tiles_per_seq = seq // tm
    keep_prev = jnp.where((i % tiles_per_seq) == 0, 0.0, 1.0)
    keep_next = jnp.where((i % tiles_per_seq) == tiles_per_seq - 1, 0.0, 1.0)
    nch = tm // ROW_CHUNK

    xb_ref[0:HALO] = xp_ref[...].astype(BF16)
    xb_ref[HALO:HALO + tm] = x_ref[...].astype(BF16)
    xb_ref[HALO + tm:] = xn_ref[...].astype(BF16)

    def pool_input(r):
        lo = HALO + r * ROW_CHUNK - (HALO if r == 0 else 0)
        hi = HALO + (r + 1) * ROW_CHUNK + (HALO if r == nch - 1 else 0)
        u_ref[lo:hi] = jnp.dot(xb_ref[lo:hi], wu_ref[...], preferred_element_type=F32)
        if r == 0:
            u_ref[0:HALO] = u_ref[0:HALO] * keep_prev
        if r == nch - 1:
            u_ref[HALO + tm:] = u_ref[HALO + tm:] * keep_next

    def pooled_out(r):
        base = HALO + r * ROW_CHUNK
        rows = slice(r * ROW_CHUNK, (r + 1) * ROW_CHUNK)
        gate = jnp.dot(xb_ref[base:base + ROW_CHUNK], wg_ref[...], preferred_element_type=F32)
        pos = ((i % tiles_per_seq) * tm + r * ROW_CHUNK
               + lax.broadcasted_iota(jnp.int32, (ROW_CHUNK, POOL_GROUP), 0))
        for gi, w in enumerate(POOL_WINDOWS):
            cols = slice(gi * POOL_GROUP, (gi + 1) * POOL_GROUP)
            tot = u_ref[pl.ds(base - w // 2, ROW_CHUNK), cols]
            for off in range(-w // 2 + 1, w // 2):
                tot = tot + u_ref[pl.ds(base + off, ROW_CHUNK), cols]
            lo = jnp.maximum(pos - w // 2, 0)
            hi = jnp.minimum(pos + w // 2 - 1, seq - 1)
            cnt = (hi - lo + 1).astype(F32)
            pooled = tot / cnt - u_ref[pl.ds(base, ROW_CHUNK), cols]
            ob = jnp.dot(pooled.astype(BF16), pw_ref[gi], preferred_element_type=F32)
            o_ref[rows, cols] = (ob * ps_ref[:, cols] * _silu(gate[:, cols])).astype(BF16)

    pool_input(0)
    for r in range(nch):
        if r + 1 < nch:
            pool_input(r + 1)
        pooled_out(r)


def _pool_branch(x2, w_ab, pool_w, pool_scale, *, seq, tm=512):
    m, dm = x2.shape
    ub = 4 * ATT_WIDTH // POOL_WIDTH
    hb = tm // HALO
    nhb = m // HALO
    return pl.pallas_call(
        functools.partial(_pool_kernel, tm=tm, seq=seq),
        grid=(m // tm,),
        in_specs=[pl.BlockSpec((HALO, dm), lambda i: (jnp.maximum(i * hb - 1, 0), 0)),
                  pl.BlockSpec((tm, dm), lambda i: (i, 0)),
                  pl.BlockSpec((HALO, dm), lambda i: (jnp.minimum((i + 1) * hb, nhb - 1), 0)),
                  pl.BlockSpec((dm, POOL_WIDTH), lambda i: (0, ub)),
                  pl.BlockSpec((dm, POOL_WIDTH), lambda i: (0, ub + 1)),
                  pl.BlockSpec((len(POOL_WINDOWS), POOL_GROUP, POOL_GROUP), lambda i: (0, 0, 0)),
                  pl.BlockSpec((1, POOL_WIDTH), lambda i: (0, 0))],
        out_specs=pl.BlockSpec((tm, POOL_WIDTH), lambda i: (i, 0)),
        out_shape=jax.ShapeDtypeStruct((m, POOL_WIDTH), BF16),
        scratch_shapes=[pltpu.VMEM((tm + 2 * HALO, dm), BF16), pltpu.VMEM((tm + 2 * HALO, POOL_WIDTH), F32)],
        compiler_params=pltpu.CompilerParams(dimension_semantics=("parallel",), vmem_limit_bytes=VMEM_LIMIT),
        name="pool_branch",
    )(x2, x2, x2, w_ab, w_ab, pool_w, pool_scale)


def _attn_kernel(q_ref, kvp_ref, kvm_ref, kvn_ref, bias_ref, o_ref, st_ref, s0_ref, s1_ref, *, qb, nblk, ns):
    n = pl.program_id(1)
    na = qb // ATT_Q_ROWS
    ng = max(1, na // ATT_UNITS)
    apg = na // ng
    hpi = max(1, ATT_UNITS // na)
    steps = (ATT_HEADS // hpi) * ng
    all_steps = ns * steps
    trip = min(ATT_TRIP_STEPS, all_steps)
    assert all_steps % trip == 0 and trip % (2 * ng) == 0 and steps % ng == 0
    lane = lax.broadcasted_iota(jnp.int32, (ATT_Q_ROWS, LANES), 1)
    st_ref[...] = jnp.broadcast_to(
        jnp.where((lane[:1] >= ATT_HEADS) & (lane[:1] < 2 * ATT_HEADS), 1.0, 0.0), st_ref.shape)
    first = (n == 0).astype(jnp.int32)
    last = (n == nblk - 1).astype(jnp.int32)
    s_refs = (s0_ref, s1_ref)
    ones = jnp.ones((ATT_KEYS, LANES), BF16)

    def window(si, slab, a):
        lo = a * ATT_Q_ROWS - RADIUS
        hi = lo + ATT_KEYS
        parts = []
        if lo < 0:
            parts.append(kvp_ref[si, slab])
        parts.append(kvm_ref[si, slab, max(lo, 0):min(hi, qb), :])
        if hi > qb:
            parts.append(kvn_ref[si, slab])
        return parts[0] if len(parts) == 1 else jnp.concatenate(parts, axis=0)

    def scores(si, step, grp, slot):
        for j in range(hpi):
            h = (step // ng) * hpi + j
            for ai in range(apg):
                a = grp * apg + ai
                q = q_ref[si, h, a * ATT_Q_ROWS:(a + 1) * ATT_Q_ROWS, :]
                s_refs[slot][j * apg + ai] = lax.dot_general(
                    q, window(si, h, a), (((1,), (1,)), ((), ())), preferred_element_type=F32)

    def softmax_pv(si, step, grp, slot):
        for j in range(hpi):
            h = (step // ng) * hpi + j
            for ai in range(apg):
                a = grp * apg + ai
                rows = slice(a * ATT_Q_ROWS, (a + 1) * ATT_Q_ROWS)
                variant = (first if a == 0 else 0) + (2 * last if a == na - 1 else 0)
                s = s_refs[slot][j * apg + ai] + bias_ref[variant, h]
                mx = jnp.max(s, axis=-1, keepdims=True)
                p = jnp.exp2(s - mx)
                v = window(si, ATT_HEADS + h, a)
                pv = jnp.dot(p.astype(BF16), jnp.concatenate([v, ones], axis=1), preferred_element_type=F32)
                o_ref[si, h, rows, :] = pv[:, :HEAD_DIM].astype(BF16)
                st_ref[si, rows, :] = jnp.where(
                    lane == h, mx, jnp.where(lane == h + ATT_HEADS, pv[:, HEAD_DIM:], st_ref[si, rows, :]))

    def locate(g):
        return (g // steps, g % steps) if ns > 1 else (0, g)

    scores(0, 0, 0, 0)

    def body(tt, c):
        for u in range(trip):
            g = trip * tt + u
            scores(*locate((g + 1) % all_steps), (u + 1) % ng, (u + 1) % 2)
            softmax_pv(*locate(g), u % ng, u % 2)
        return c

    lax.fori_loop(0, all_steps // trip, body, 0)


def _banded_attention(hd, bias):
    nseq, _, length, _ = hd.shape
    qb = min(ATT_BLOCK_ROWS, length)
    nblk = length // qb
    ns = max(1, ATT_MIN_STEP_ROWS // length)
    hb = qb // RADIUS
    nh = length // RADIUS
    units = max(1, ATT_UNITS // (qb // ATT_Q_ROWS)) * min(ATT_UNITS, qb // ATT_Q_ROWS)
    return pl.pallas_call(
        functools.partial(_attn_kernel, qb=qb, nblk=nblk, ns=ns),
        grid=(nseq // ns, nblk),
        in_specs=[pl.BlockSpec((ns, ATT_HEADS, qb, LANES), lambda s, n: (s, 2, n, 0)),
                  pl.BlockSpec((ns, 2 * ATT_HEADS, RADIUS, LANES),
                               lambda s, n: (s, 0, jnp.maximum(n * hb - 1, 0), 0)),
                  pl.BlockSpec((ns, 2 * ATT_HEADS, qb, LANES), lambda s, n: (s, 0, n, 0)),
                  pl.BlockSpec((ns, 2 * ATT_HEADS, RADIUS, LANES),
                               lambda s, n: (s, 0, jnp.minimum((n + 1) * hb, nh - 1), 0)),
                  pl.BlockSpec(bias.shape, lambda s, n: (0, 0, 0, 0), pipeline_mode=pl.Buffered(1))],
        out_specs=[pl.BlockSpec((ns, ATT_HEADS, qb, LANES), lambda s, n: (s, 0, n, 0)),
                   pl.BlockSpec((ns, qb, LANES), lambda s, n: (s, n, 0))],
        out_shape=[jax.ShapeDtypeStruct((nseq, ATT_HEADS, length, LANES), BF16),
                   jax.ShapeDtypeStruct((nseq, length, LANES), F32)],
        scratch_shapes=[pltpu.VMEM((units, ATT_Q_ROWS, ATT_KEYS), F32)] * 2,
        compiler_params=pltpu.CompilerParams(
            dimension_semantics=("parallel", "parallel"), vmem_limit_bytes=VMEM_LIMIT),
        name="banded_attention",
    )(hd, hd, hd, hd, bias)


def _merge_kernel(o1_ref, o4_ref, o16_ref, s1_ref, s4_ref, s16_ref,
                  ga_ref, ob_ref, x_ref, w_ref, g_ref, b_ref, out_ref, outb_ref,
                  y0_ref, y1_ref, st_ref, r_ref, f_ref, *, tm, total):
    t = pl.program_id(0)
    ys = (y0_ref, y1_ref)
    ncol = w_ref.shape[1] // MXU_COLS
    hpp = ATT_HEADS // ncol

    def merge_weights():
        st_ref[0] = s1_ref[0]
        for p, (d, sr) in enumerate(((4, s4_ref), (16, s16_ref)), start=1):
            for g in range(d):
                st_ref[p, pl.ds(g, tm // d, stride=d), :] = sr[g]
        ms = [st_ref[p] for p in range(3)]
        mx = jnp.maximum(jnp.maximum(ms[0], ms[1]), ms[2])
        es = [jnp.exp2(m - mx) for m in ms]
        den = sum(e * pltpu.roll(m, LANES - ATT_HEADS, axis=1) for e, m in zip(es, ms))
        head_lane = lax.broadcasted_iota(jnp.int32, den.shape, 1) < ATT_HEADS
        den = jnp.where(head_lane, den, 1.0)
        for p in range(3):
            st_ref[p] = es[p] / den

    def merge_head(dst, h):
        for p, (d, o_ref) in enumerate(((4, o4_ref), (16, o16_ref))):
            for g in range(d):
                r_ref[p, pl.ds(g, tm // d, stride=d), :] = o_ref[g, h].astype(F32)
        oa = (st_ref[0, :, h:h + 1] * o1_ref[0, h].astype(F32)
              + st_ref[1, :, h:h + 1] * r_ref[0]
              + st_ref[2, :, h:h + 1] * r_ref[1])
        cols = slice(h * HEAD_DIM, (h + 1) * HEAD_DIM)
        dst[:, cols] = (oa * ga_ref[:, cols].astype(F32)).astype(BF16)

    def product(src, k):
        cols = slice(k * MXU_COLS, (k + 1) * MXU_COLS)
        f_ref[:, cols] = jnp.dot(src[...], w_ref[:, cols], preferred_element_type=F32)
        if k == ncol - 1:
            for r in range(0, tm, ROW_CHUNK):
                rows = slice(r, r + ROW_CHUNK)
                y = _layernorm(DN_ALPHA * x_ref[rows, :] + f_ref[rows, :], g_ref[...], b_ref[...])
                out_ref[rows, :] = y
                outb_ref[rows, :] = y.astype(BF16)

    @pl.when(t == 0)
    def _():
        merge_weights()
        for h in range(ATT_HEADS):
            merge_head(ys[0], h)
        ys[0][:, ATT_WIDTH:] = ob_ref[...]

    @pl.when(t == total)
    def _():
        for k in range(ncol):
            product(ys[(total - 1) % 2], k)

    for parity in (0, 1):
        @pl.when((t > 0) & (t < total) & (t % 2 == parity))
        def _(parity=parity):
            merge_weights()
            for k in range(ncol):
                product(ys[1 - parity], k)
                for h in range(k * hpp, (k + 1) * hpp):
                    merge_head(ys[parity], h)
            ys[parity][:, ATT_WIDTH:] = ob_ref[...]


def _merge_outproj(o_pats, stats, ga, ob, x, w_out, ln_g, ln_b, *, tm=256):
    b, s, dm = x.shape
    nt = s // tm
    total = b * nt
    ga = ga.reshape(b, s, ATT_WIDTH)
    ob = ob.reshape(b, s, POOL_WIDTH)

    def cur(t):
        t = jnp.minimum(t, total - 1)
        return t // nt, t % nt

    def prev(t):
        t = jnp.maximum(t - 1, 0)
        return t // nt, t % nt

    def o_map(t):
        bi, i = cur(t)
        return bi, 0, 0, i, 0

    def st_map(t):
        bi, i = cur(t)
        return bi, 0, i, 0

    def cur_map(t):
        bi, i = cur(t)
        return bi, i, 0

    def prev_map(t):
        bi, i = prev(t)
        return bi, i, 0

    o_specs = [pl.BlockSpec((None, d, ATT_HEADS, tm // d, LANES), o_map) for d in DILATIONS]
    st_specs = [pl.BlockSpec((None, d, tm // d, LANES), st_map) for d in DILATIONS]
    const2 = lambda t: (0, 0)
    args = list(o_pats) + list(stats)
    return pl.pallas_call(
        functools.partial(_merge_kernel, tm=tm, total=total),
        grid=(total + 1,),
        in_specs=o_specs + st_specs + [
            pl.BlockSpec((None, tm, ATT_WIDTH), cur_map),
            pl.BlockSpec((None, tm, POOL_WIDTH), cur_map),
            pl.BlockSpec((None, tm, dm), prev_map),
            pl.BlockSpec(w_out.shape, const2, pipeline_mode=pl.Buffered(1)),
            pl.BlockSpec((1, dm), const2),
            pl.BlockSpec((1, dm), const2)],
        out_specs=[pl.BlockSpec((None, tm, dm), prev_map), pl.BlockSpec((None, tm, dm), prev_map)],
        out_shape=[jax.ShapeDtypeStruct((b, s, dm), F32), jax.ShapeDtypeStruct((b, s, dm), BF16)],
        scratch_shapes=[pltpu.VMEM((tm, ATT_WIDTH + POOL_WIDTH), BF16),
                        pltpu.VMEM((tm, ATT_WIDTH + POOL_WIDTH), BF16),
                        pltpu.VMEM((3, tm, LANES), F32),
                        pltpu.VMEM((2, tm, LANES), F32),
                        pltpu.VMEM((tm, dm), F32)],
        compiler_params=pltpu.CompilerParams(dimension_semantics=("arbitrary",), vmem_limit_bytes=VMEM_LIMIT),
        name="merge_outproj",
    )(*args, ga, ob, x, w_out, ln_g, ln_b)


def _conv_in_kernel(xp_ref, x_ref, xn_ref, wb_ref, wc_ref, wv_ref, wg_ref, cw_ref, o_ref, cv_ref, *, tm, seq):
    i = pl.program_id(0)
    tiles_per_seq = seq // tm
    keep_prev = jnp.where((i % tiles_per_seq) == 0, 0.0, 1.0)
    keep_next = jnp.where((i % tiles_per_seq) == tiles_per_seq - 1, 0.0, 1.0)
    nch = tm // ROW_CHUNK

    def conv_input(r):
        parts = [x_ref[r * ROW_CHUNK:(r + 1) * ROW_CHUNK, :]]
        if r == 0:
            parts.insert(0, xp_ref[...])
        if r == nch - 1:
            parts.append(xn_ref[...])
        xa = parts[0] if len(parts) == 1 else jnp.concatenate(parts, axis=0)
        lo = HALO + r * ROW_CHUNK - (HALO if r == 0 else 0)
        cv_ref[lo:lo + xa.shape[0]] = (jnp.dot(xa, wc_ref[...], preferred_element_type=F32)
                                       * jnp.dot(xa, wv_ref[...], preferred_element_type=F32))
        if r == 0:
            cv_ref[0:HALO] = cv_ref[0:HALO] * keep_prev
        if r == nch - 1:
            cv_ref[HALO + tm:] = cv_ref[HALO + tm:] * keep_next

    def gated_conv(r):
        base = HALO + r * ROW_CHUNK
        xm = x_ref[r * ROW_CHUNK:(r + 1) * ROW_CHUNK, :]
        gb = jnp.dot(xm, wb_ref[...], preferred_element_type=F32)
        gate = jnp.dot(xm, wg_ref[...], preferred_element_type=F32)
        conv = (cv_ref[pl.ds(base - 1, ROW_CHUNK), :] * cw_ref[0:1, :]
                + cv_ref[pl.ds(base, ROW_CHUNK), :] * cw_ref[1:2, :]
                + cv_ref[pl.ds(base + 1, ROW_CHUNK), :] * cw_ref[2:3, :])
        o_ref[r * ROW_CHUNK:(r + 1) * ROW_CHUNK, :] = (gb * conv * _silu(gate)).astype(BF16)

    conv_input(0)
    for r in range(nch):
        if r + 1 < nch:
            conv_input(r + 1)
        gated_conv(r)


def _conv_inproj(xb, w_in, conv_w, *, seq, tm=1024, tc=1024):
    m, dm = xb.shape
    hb = tm // HALO
    nhb = m // HALO
    nc = CONV_WIDTH // tc

    def wspec(part):
        return pl.BlockSpec((dm, tc), lambda i, c: (0, part * nc + c))

    return pl.pallas_call(
        functools.partial(_conv_in_kernel, tm=tm, seq=seq),
        grid=(m // tm, nc),
        in_specs=[pl.BlockSpec((HALO, dm), lambda i, c: (jnp.maximum(i * hb - 1, 0), 0)),
                  pl.BlockSpec((tm, dm), lambda i, c: (i, 0)),
                  pl.BlockSpec((HALO, dm), lambda i, c: (jnp.minimum((i + 1) * hb, nhb - 1), 0)),
                  wspec(0), wspec(1), wspec(2), wspec(3),
                  pl.BlockSpec((3, tc), lambda i, c: (0, c))],
        out_specs=pl.BlockSpec((tm, tc), lambda i, c: (i, c)),
        out_shape=jax.ShapeDtypeStruct((m, CONV_WIDTH), BF16),
        scratch_shapes=[pltpu.VMEM((tm + 2 * HALO, tc), F32)],
        compiler_params=pltpu.CompilerParams(
            dimension_semantics=("parallel", "parallel"), vmem_limit_bytes=VMEM_LIMIT),
        name="conv_inproj",
    )(xb, xb, xb, w_in, w_in, w_in, w_in, conv_w)


def _outproj_ln_kernel(y_ref, x_ref, w_ref, g_ref, b_ref, o_ref):
    for r in range(0, y_ref.shape[0], ROW_CHUNK):
        rows = slice(r, r + ROW_CHUNK)
        f = jnp.dot(y_ref[rows, :], w_ref[...], preferred_element_type=F32)
        o_ref[rows, :] = _layernorm(DN_ALPHA * x_ref[rows, :] + f, g_ref[...], b_ref[...])


def _outproj_ln(y, x2, w_out, ln_g, ln_b, *, tm=512):
    m, dm = x2.shape
    const2 = lambda i: (0, 0)
    return pl.pallas_call(
        _outproj_ln_kernel,
        grid=(m // tm,),
        in_specs=[pl.BlockSpec((tm, y.shape[1]), lambda i: (i, 0)),
                  pl.BlockSpec((tm, dm), lambda i: (i, 0)),
                  pl.BlockSpec(w_out.shape, const2, pipeline_mode=pl.Buffered(1)),
                  pl.BlockSpec((1, dm), const2),
                  pl.BlockSpec((1, dm), const2)],
        out_specs=pl.BlockSpec((tm, dm), lambda i: (i, 0)),
        out_shape=jax.ShapeDtypeStruct((m, dm), F32),
        compiler_params=pltpu.CompilerParams(dimension_semantics=("parallel",), vmem_limit_bytes=VMEM_LIMIT),
        name="outproj_ln",
    )(y, x2, w_out, ln_g, ln_b)


def _trunk(x, biases, w_ab, pool_w, pool_scale, w_out_ab, w_in_c, conv_w, w_out_c, ln_g, ln_b):
    b, s, dm = x.shape
    x2 = x.reshape(b * s, dm)
    *hds, ga = _qkvg_proj(x, w_ab)
    ob = _pool_branch(x2, w_ab, pool_w, pool_scale, seq=s)
    o_pats, stats = [], []
    for d, hd, bias in zip(DILATIONS, hds, biases):
        ld = s // d
        o, st = _banded_attention(hd.reshape(b * d, 3 * ATT_HEADS, ld, LANES), bias)
        o_pats.append(o.reshape(b, d, ATT_HEADS, ld, LANES))
        stats.append(st.reshape(b, d, ld, LANES))
    x1, x1b = _merge_outproj(o_pats, stats, ga, ob, x, w_out_ab, ln_g[0:1], ln_b[0:1])
    y = _conv_inproj(x1b.reshape(b * s, dm), w_in_c, conv_w, seq=s)
    out = _outproj_ln(y, x1.reshape(b * s, dm), w_out_c, ln_g[1:2], ln_b[1:2])
    return out.reshape(b, s, dm)


def kernel(x_prompt, x_sample, rel_bias, w_in_ab, pool_w, pool_scale, w_out_ab, w_in_c, conv_w, w_out_c, ln_g, ln_b):
    assert DEPTH == 2 and w_in_ab.shape[0] == 1 and w_in_c.shape[0] == 1
    params = dict(
        biases=[_band_bias(rel_bias, d) for d in DILATIONS],
        w_ab=w_in_ab[0].astype(BF16),
        pool_w=pool_w[0].astype(BF16),
        pool_scale=pool_scale[0].reshape(1, POOL_WIDTH),
        w_out_ab=w_out_ab[0].astype(BF16),
        w_in_c=w_in_c[0].astype(BF16),
        conv_w=conv_w[0],
        w_out_c=w_out_c[0].astype(BF16),
        ln_g=ln_g,
        ln_b=ln_b,
    )
    return (_trunk(x_prompt, **params), _trunk(x_sample, **params))
```

```python
import functools
import math

import numpy as np
import jax
import jax.numpy as jnp
from jax import lax
from jax.experimental import pallas as pl
from jax.experimental.pallas import tpu as pltpu

D_MODEL = 2048
DEPTH = 2
ATT_HEADS = 16
HEAD_DIM = 128
ATT_WIDTH = ATT_HEADS * HEAD_DIM
DILATED_PATTERNS = ((128, 1), (512, 4), (2048, 16))
DILATIONS = tuple(d for _, d in DILATED_PATTERNS)
RADIUS = 64
POOL_WINDOWS = (2, 4, 8, 16)
POOL_WIDTH = D_MODEL // 2
POOL_GROUP = POOL_WIDTH // len(POOL_WINDOWS)
CONV_WIDTH = D_MODEL
REL_BUCKETS = 32
REL_MAX_DISTANCE = 1024
DN_ALPHA = (2 * DEPTH) ** 0.25
LN_EPS = 1e-5
NEG_INF = -1e30
LOG2E = math.log2(math.e)

LANES = 128
MXU_COLS = 256
SLABS_PER_CHUNK = MXU_COLS // LANES
HALO = 16
ATT_Q_ROWS = 128
ATT_KEYS = ATT_Q_ROWS + 2 * RADIUS
ATT_UNITS = 4
ATT_TRIP_STEPS = 32
ATT_BLOCK_ROWS = 1024
ATT_MIN_STEP_ROWS = 512
PROJ_ROWS = 256
ROW_CHUNK = 128
VMEM_LIMIT = 56 * 1024 * 1024

BF16 = jnp.bfloat16
F32 = jnp.float32

assert all(w // (2 * d) == RADIUS for w, d in DILATED_PATTERNS)


def _t5_bucket(rel):
    nb = REL_BUCKETS // 2
    max_exact = nb // 2
    ret = np.where(rel > 0, nb, 0)
    n = np.abs(rel)
    n_safe = np.maximum(n, 1).astype(np.float64)
    large = max_exact + (np.log(n_safe / max_exact) / math.log(REL_MAX_DISTANCE / max_exact)
                         * (nb - max_exact)).astype(np.int64)
    large = np.minimum(large, nb - 1)
    return (ret + np.where(n < max_exact, n, large)).astype(np.int32)


def _band_bias(rel_bias, dilation):
    rel = np.arange(ATT_KEYS)[None, :] - RADIUS - np.arange(ATT_Q_ROWS)[:, None]
    bucket = jnp.asarray(_t5_bucket(rel * dilation).reshape(1, -1))
    onehot = (bucket == jnp.arange(REL_BUCKETS, dtype=jnp.int32)[:, None]).astype(F32)
    bias = jnp.dot(rel_bias.astype(F32).T, onehot, precision=lax.Precision.HIGHEST)
    bias = bias.reshape(1, ATT_HEADS, ATT_Q_ROWS, ATT_KEYS)
    key = np.arange(ATT_KEYS)[None, :]
    in_band = np.abs(rel) <= RADIUS
    after_start = key >= RADIUS
    before_end = key < ATT_Q_ROWS + RADIUS
    keep = np.stack([in_band, in_band & after_start, in_band & before_end, in_band & after_start & before_end])
    return jnp.where(jnp.asarray(keep[:, None]), bias * LOG2E, NEG_INF)


def _silu(x):
    return x / (1.0 + jnp.exp(-x))


def _layernorm(z, g, b):
    mu = jnp.mean(z, axis=-1, keepdims=True)
    zc = z - mu
    var = jnp.mean(zc * zc, axis=-1, keepdims=True)
    return zc * lax.rsqrt(var + LN_EPS) * g + b


def _qkvg_kernel(x_ref, w_ref, o1_ref, o4_ref, o16_ref, ga_ref, xb_ref, s0_ref, s1_ref, mid0_ref, mid1_ref,
                 *, tm, tn, nq):
    j = pl.program_id(2)
    chunk_bufs = ((s0_ref, mid0_ref), (s1_ref, mid1_ref))
    r4, r16 = PROJ_ROWS // 4, PROJ_ROWS // 16

    @pl.when(j == 0)
    def _():
        xb_ref[...] = x_ref[...].astype(BF16)

    def product(rc, cc):
        rows = slice(rc * PROJ_ROWS, (rc + 1) * PROJ_ROWS)
        cols = slice(cc * MXU_COLS, (cc + 1) * MXU_COLS)
        return jnp.dot(xb_ref[rows, :], w_ref[:, cols], preferred_element_type=F32)

    def emit_qkv(acc, rc, cc, s_ref, mid_ref):
        acc = acc * jnp.where(j < ATT_WIDTH // tn, LOG2E * HEAD_DIM ** -0.5, 1.0)
        for c in range(SLABS_PER_CHUNK):
            slab = SLABS_PER_CHUNK * cc + c
            blk = acc[:, c * LANES:(c + 1) * LANES]
            o1_ref[0, slab, rc * PROJ_ROWS:(rc + 1) * PROJ_ROWS, :] = blk.astype(BF16)
            s_ref[c] = blk
            for g4 in range(4):
                rows4 = s_ref[c, pl.ds(g4, r4, stride=4), :]
                o4_ref[g4, slab, rc * r4:(rc + 1) * r4, :] = rows4.astype(BF16)
                mid_ref[c, g4] = rows4
            for g4 in range(4):
                for k in range(4):
                    o16_ref[g4 + 4 * k, slab, rc * r16:(rc + 1) * r16, :] = (
                        mid_ref[c, g4, pl.ds(k, r16, stride=4), :].astype(BF16))

    def emit_gate(acc, rc, cc):
        ga_ref[rc * PROJ_ROWS:(rc + 1) * PROJ_ROWS, cc * MXU_COLS:(cc + 1) * MXU_COLS] = _silu(acc).astype(BF16)

    chunks = [(rc, cc) for cc in range(tn // MXU_COLS) for rc in range(tm // PROJ_ROWS)]

    @pl.when(j < nq)
    def _():
        for n, (rc, cc) in enumerate(chunks):
            emit_qkv(product(rc, cc), rc, cc, *chunk_bufs[n % 2])

    @pl.when(j >= nq)
    def _():
        for rc, cc in chunks:
            emit_gate(product(rc, cc), rc, cc)


def _qkvg_proj(x, w, *, tm=1024, tn=1024):
    b, s, dm = x.shape
    nj = 4 * ATT_WIDTH // tn
    nq = 3 * ATT_WIDTH // tn
    qt = ATT_WIDTH // tn
    nslab = 3 * ATT_WIDTH // LANES
    outs = [jax.ShapeDtypeStruct((b, d, nslab, s // d, LANES), BF16) for d in DILATIONS]
    outs.append(jax.ShapeDtypeStruct((b, s, ATT_WIDTH), BF16))
    out_specs = [pl.BlockSpec((None, d, tn // LANES, tm // d, LANES),
                              lambda bi, i, j: (bi, 0, (jnp.minimum(j, nq - 1) + 2 * qt) % nq, i, 0)) for d in DILATIONS]
    out_specs.append(pl.BlockSpec((None, tm, tn), lambda bi, i, j: (bi, i, jnp.maximum(j - nq, 0))))
    return pl.pallas_call(
        functools.partial(_qkvg_kernel, tm=tm, tn=tn, nq=nq),
        grid=(b, s // tm, nj),
        in_specs=[pl.BlockSpec((None, tm, dm), lambda bi, i, j: (bi, i, 0)),
                  pl.BlockSpec((dm, tn), lambda bi, i, j: (0, j))],
        out_specs=out_specs,
        out_shape=outs,
        scratch_shapes=[pltpu.VMEM((tm, dm), BF16)]
        + [pltpu.VMEM((SLABS_PER_CHUNK, PROJ_ROWS, LANES), F32)] * 2
        + [pltpu.VMEM((SLABS_PER_CHUNK, 4, PROJ_ROWS // 4, LANES), F32)] * 2,
        compiler_params=pltpu.CompilerParams(
            dimension_semantics=("parallel", "parallel", "arbitrary"), vmem_limit_bytes=VMEM_LIMIT),
        name="qkvg_proj",
    )(x, w)


def _pool_kernel(xp_ref, x_ref, xn_ref, wu_ref, wg_ref, pw_ref, ps_ref, o_ref, xb_ref, u_ref, lv_ref, *, tm, seq):
    i = pl.program_id(0)
    tiles_per_seq = seq // tm
    keep_prev = jnp.where((i % tiles_per_seq) == 0, 0.0, 1.0)
    keep_next = jnp.where((i % tiles_per_seq) == tiles_per_seq - 1, 0.0, 1.0)
    nch = tm // ROW_CHUNK

    xb_ref[0:HALO] = xp_ref[...].astype(BF16)
    xb_ref[HALO:HALO + tm] = x_ref[...].astype(BF16)
    xb_ref[HALO + tm:] = xn_ref[...].astype(BF16)

    def pool_input(r):
        lo = HALO + r * ROW_CHUNK - (HALO if r == 0 else 0)
        hi = HALO + (r + 1) * ROW_CHUNK + (HALO if r == nch - 1 else 0)
        u_ref[lo:hi] = jnp.dot(xb_ref[lo:hi], wu_ref[...], preferred_element_type=F32)
        if r == 0:
            u_ref[0:HALO] = u_ref[0:HALO] * keep_prev
        if r == nch - 1:
            u_ref[HALO + tm:] = u_ref[HALO + tm:] * keep_next

    def window_sum(base, cols, gi, w):
        if w == 2:
            return u_ref[pl.ds(base - 1, ROW_CHUNK), cols] + u_ref[pl.ds(base, ROW_CHUNK), cols]
        shifts = [1 << k for k in range(w.bit_length() - 2)]
        nrows = [ROW_CHUNK + 8]
        for sft in reversed(shifts[1:]):
            nrows.insert(0, nrows[0] + sft)
        lv_ref[gi - 1, 0, 0:nrows[0], :] = (u_ref[pl.ds(base - 8, nrows[0]), cols]
                                            + u_ref[pl.ds(base - 7, nrows[0]), cols])
        for k in range(1, len(shifts)):
            lv_ref[gi - 1, k, 0:nrows[k], :] = (lv_ref[gi - 1, k - 1, 0:nrows[k], :]
                                                + lv_ref[gi - 1, k - 1, pl.ds(shifts[k], nrows[k]), :])
        top = len(shifts) - 1
        return (lv_ref[gi - 1, top, pl.ds(8 - w // 2, ROW_CHUNK), :]
                + lv_ref[gi - 1, top, pl.ds(8, ROW_CHUNK), :])

    def pooled_out(r):
        base = HALO + r * ROW_CHUNK
        rows = slice(r * ROW_CHUNK, (r + 1) * ROW_CHUNK)
        gate = jnp.dot(xb_ref[base:base + ROW_CHUNK], wg_ref[...], preferred_element_type=F32)
        pos = ((i % tiles_per_seq) * tm + r * ROW_CHUNK
               + lax.broadcasted_iota(jnp.int32, (ROW_CHUNK, POOL_GROUP), 0))
        for gi, w in enumerate(POOL_WINDOWS):
            cols = slice(gi * POOL_GROUP, (gi + 1) * POOL_GROUP)
            tot = window_sum(base, cols, gi, w)
            lo = jnp.maximum(pos - w // 2, 0)
            hi = jnp.minimum(pos + w // 2 - 1, seq - 1)
            cnt = (hi - lo + 1).astype(F32)
            pooled = tot / cnt - u_ref[pl.ds(base, ROW_CHUNK), cols]
            ob = jnp.dot(pooled.astype(BF16), pw_ref[gi], preferred_element_type=F32)
            o_ref[rows, cols] = (ob * ps_ref[:, cols] * _silu(gate[:, cols])).astype(BF16)

    pool_input(0)
    for r in range(nch):
        if r + 1 < nch:
            pool_input(r + 1)
        pooled_out(r)


def _pool_branch(x2, w_ab, pool_w, pool_scale, *, seq, tm=512):
    m, dm = x2.shape
    ub = 4 * ATT_WIDTH // POOL_WIDTH
    hb = tm // HALO
    nhb = m // HALO
    return pl.pallas_call(
        functools.partial(_pool_kernel, tm=tm, seq=seq),
        grid=(m // tm,),
        in_specs=[pl.BlockSpec((HALO, dm), lambda i: (jnp.maximum(i * hb - 1, 0), 0)),
                  pl.BlockSpec((tm, dm), lambda i: (i, 0)),
                  pl.BlockSpec((HALO, dm), lambda i: (jnp.minimum((i + 1) * hb, nhb - 1), 0)),
                  pl.BlockSpec((dm, POOL_WIDTH), lambda i: (0, ub)),
                  pl.BlockSpec((dm, POOL_WIDTH), lambda i: (0, ub + 1)),
                  pl.BlockSpec((len(POOL_WINDOWS), POOL_GROUP, POOL_GROUP), lambda i: (0, 0, 0)),
                  pl.BlockSpec((1, POOL_WIDTH), lambda i: (0, 0))],
        out_specs=pl.BlockSpec((tm, POOL_WIDTH), lambda i: (i, 0)),
        out_shape=jax.ShapeDtypeStruct((m, POOL_WIDTH), BF16),
        scratch_shapes=[pltpu.VMEM((tm + 2 * HALO, dm), BF16), pltpu.VMEM((tm + 2 * HALO, POOL_WIDTH), F32),
                        pltpu.VMEM((len(POOL_WINDOWS) - 1, 3, ROW_CHUNK + HALO, POOL_GROUP), F32)],
        compiler_params=pltpu.CompilerParams(dimension_semantics=("parallel",), vmem_limit_bytes=VMEM_LIMIT),
        name="pool_branch",
    )(x2, x2, x2, w_ab, w_ab, pool_w, pool_scale)


def _attn_kernel(q_ref, kvp_ref, kvm_ref, kvn_ref, bias_ref, o_ref, st_ref, s0_ref, s1_ref, *, qb, nblk, ns):
    n = pl.program_id(1)
    na = qb // ATT_Q_ROWS
    ng = max(1, na // ATT_UNITS)
    apg = na // ng
    hpi = max(1, ATT_UNITS // na)
    steps = (ATT_HEADS // hpi) * ng
    all_steps = ns * steps
    trip = min(ATT_TRIP_STEPS, all_steps)
    assert all_steps % trip == 0 and trip % (2 * ng) == 0 and steps % ng == 0
    lane = lax.broadcasted_iota(jnp.int32, (ATT_Q_ROWS, LANES), 1)
    st_ref[...] = jnp.broadcast_to(
        jnp.where((lane[:1] >= ATT_HEADS) & (lane[:1] < 2 * ATT_HEADS), 1.0, 0.0), st_ref.shape)
    first = (n == 0).astype(jnp.int32)
    last = (n == nblk - 1).astype(jnp.int32)
    s_refs = (s0_ref, s1_ref)
    ones = jnp.ones((ATT_KEYS, LANES), BF16)

    def window(si, slab, a):
        lo = a * ATT_Q_ROWS - RADIUS
        hi = lo + ATT_KEYS
        parts = []
        if lo < 0:
            parts.append(kvp_ref[si, slab])
        parts.append(kvm_ref[si, slab, max(lo, 0):min(hi, qb), :])
        if hi > qb:
            parts.append(kvn_ref[si, slab])
        return parts[0] if len(parts) == 1 else jnp.concatenate(parts, axis=0)

    def scores(si, step, grp, slot):
        for j in range(hpi):
            h = (step // ng) * hpi + j
            for ai in range(apg):
                a = grp * apg + ai
                q = q_ref[si, h, a * ATT_Q_ROWS:(a + 1) * ATT_Q_ROWS, :]
                s_refs[slot][j * apg + ai] = lax.dot_general(
                    q, window(si, h, a), (((1,), (1,)), ((), ())), preferred_element_type=F32)

    def softmax_pv(si, step, grp, slot):
        for j in range(hpi):
            h = (step // ng) * hpi + j
            for ai in range(apg):
                a = grp * apg + ai
                rows = slice(a * ATT_Q_ROWS, (a + 1) * ATT_Q_ROWS)
                variant = (first if a == 0 else 0) + (2 * last if a == na - 1 else 0)
                s = s_refs[slot][j * apg + ai] + bias_ref[variant, h]
                mx = jnp.max(s, axis=-1, keepdims=True)
                p = jnp.exp2(s - mx)
                v = window(si, ATT_HEADS + h, a)
                pv = jnp.dot(p.astype(BF16), jnp.concatenate([v, ones], axis=1), preferred_element_type=F32)
                o_ref[si, h, rows, :] = pv[:, :HEAD_DIM].astype(BF16)
                st_ref[si, rows, :] = jnp.where(
                    lane == h, mx, jnp.where(lane == h + ATT_HEADS, pv[:, HEAD_DIM:], st_ref[si, rows, :]))

    def locate(g):
        return (g // steps, g % steps) if ns > 1 else (0, g)

    scores(0, 0, 0, 0)

    def body(tt, c):
        for u in range(trip):
            g = trip * tt + u
            scores(*locate((g + 1) % all_steps), (u + 1) % ng, (u + 1) % 2)
            softmax_pv(*locate(g), u % ng, u % 2)
        return c

    lax.fori_loop(0, all_steps // trip, body, 0)


def _banded_attention(hd, bias):
    nseq, _, length, _ = hd.shape
    qb = min(ATT_BLOCK_ROWS, length)
    nblk = length // qb
    ns = max(1, ATT_MIN_STEP_ROWS // length)
    hb = qb // RADIUS
    nh = length // RADIUS
    units = max(1, ATT_UNITS // (qb // ATT_Q_ROWS)) * min(ATT_UNITS, qb // ATT_Q_ROWS)
    return pl.pallas_call(
        functools.partial(_attn_kernel, qb=qb, nblk=nblk, ns=ns),
        grid=(nseq // ns, nblk),
        in_specs=[pl.BlockSpec((ns, ATT_HEADS, qb, LANES), lambda s, n: (s, 2, n, 0)),
                  pl.BlockSpec((ns, 2 * ATT_HEADS, RADIUS, LANES),
                               lambda s, n: (s, 0, jnp.maximum(n * hb - 1, 0), 0)),
                  pl.BlockSpec((ns, 2 * ATT_HEADS, qb, LANES), lambda s, n: (s, 0, n, 0)),
                  pl.BlockSpec((ns, 2 * ATT_HEADS, RADIUS, LANES),
                               lambda s, n: (s, 0, jnp.minimum((n + 1) * hb, nh - 1), 0)),
                  pl.BlockSpec(bias.shape, lambda s, n: (0, 0, 0, 0), pipeline_mode=pl.Buffered(1))],
        out_specs=[pl.BlockSpec((ns, ATT_HEADS, qb, LANES), lambda s, n: (s, 0, n, 0)),
                   pl.BlockSpec((ns, qb, LANES), lambda s, n: (s, n, 0))],
        out_shape=[jax.ShapeDtypeStruct((nseq, ATT_HEADS, length, LANES), BF16),
                   jax.ShapeDtypeStruct((nseq, length, LANES), F32)],
        scratch_shapes=[pltpu.VMEM((units, ATT_Q_ROWS, ATT_KEYS), F32)] * 2,
        compiler_params=pltpu.CompilerParams(
            dimension_semantics=("parallel", "parallel"), vmem_limit_bytes=VMEM_LIMIT),
        name="banded_attention",
    )(hd, hd, hd, hd, bias)


def _merge_kernel(o1_ref, o4_ref, o16_ref, s1_ref, s4_ref, s16_ref,
                  ga_ref, ob_ref, x_ref, w_ref, g_ref, b_ref, out_ref, outb_ref,
                  y0_ref, y1_ref, st_ref, r_ref, f_ref, *, tm, total):
    t = pl.program_id(0)
    ys = (y0_ref, y1_ref)
    ncol = w_ref.shape[1] // MXU_COLS
    hpp = ATT_HEADS // ncol

    def merge_weights():
        st_ref[0] = s1_ref[0]
        for p, (d, sr) in enumerate(((4, s4_ref), (16, s16_ref)), start=1):
            for g in range(d):
                st_ref[p, pl.ds(g, tm // d, stride=d), :] = sr[g]
        ms = [st_ref[p] for p in range(3)]
        mx = jnp.maximum(jnp.maximum(ms[0], ms[1]), ms[2])
        es = [jnp.exp2(m - mx) for m in ms]
        den = sum(e * pltpu.roll(m, LANES - ATT_HEADS, axis=1) for e, m in zip(es, ms))
        head_lane = lax.broadcasted_iota(jnp.int32, den.shape, 1) < ATT_HEADS
        den = jnp.where(head_lane, den, 1.0)
        for p in range(3):
            st_ref[p] = es[p] / den

    def merge_head(dst, h):
        for p, (d, o_ref) in enumerate(((4, o4_ref), (16, o16_ref))):
            for g in range(d):
                r_ref[p, pl.ds(g, tm // d, stride=d), :] = o_ref[g, h].astype(F32)
        oa = (st_ref[0, :, h:h + 1] * o1_ref[0, h].astype(F32)
              + st_ref[1, :, h:h + 1] * r_ref[0]
              + st_ref[2, :, h:h + 1] * r_ref[1])
        cols = slice(h * HEAD_DIM, (h + 1) * HEAD_DIM)
        dst[:, cols] = (oa * ga_ref[:, cols].astype(F32)).astype(BF16)

    def product(src, k):
        cols = slice(k * MXU_COLS, (k + 1) * MXU_COLS)
        f_ref[:, cols] = jnp.dot(src[...], w_ref[:, cols], preferred_element_type=F32)
        if k == ncol - 1:
            for r in range(0, tm, ROW_CHUNK):
                rows = slice(r, r + ROW_CHUNK)
                y = _layernorm(DN_ALPHA * x_ref[rows, :] + f_ref[rows, :], g_ref[...], b_ref[...])
                out_ref[rows, :] = y
                outb_ref[rows, :] = y.astype(BF16)

    @pl.when(t == 0)
    def _():
        merge_weights()
        for h in range(ATT_HEADS):
            merge_head(ys[0], h)
        ys[0][:, ATT_WIDTH:] = ob_ref[...]

    @pl.when(t == total)
    def _():
        for k in range(ncol):
            product(ys[(total - 1) % 2], k)

    for parity in (0, 1):
        @pl.when((t > 0) & (t < total) & (t % 2 == parity))
        def _(parity=parity):
            merge_weights()
            for k in range(ncol):
                product(ys[1 - parity], k)
                for h in range(k * hpp, (k + 1) * hpp):
                    merge_head(ys[parity], h)
            ys[parity][:, ATT_WIDTH:] = ob_ref[...]


def _merge_outproj(o_pats, stats, ga, ob, x, w_out, ln_g, ln_b, *, tm=256):
    b, s, dm = x.shape
    nt = s // tm
    total = b * nt
    ga = ga.reshape(b, s, ATT_WIDTH)
    ob = ob.reshape(b, s, POOL_WIDTH)

    def cur(t):
        t = jnp.minimum(t, total - 1)
        return t // nt, t % nt

    def prev(t):
        t = jnp.maximum(t - 1, 0)
        return t // nt, t % nt

    def o_map(t):
        bi, i = cur(t)
        return bi, 0, 0, i, 0

    def st_map(t):
        bi, i = cur(t)
        return bi, 0, i, 0

    def cur_map(t):
        bi, i = cur(t)
        return bi, i, 0

    def prev_map(t):
        bi, i = prev(t)
        return bi, i, 0

    o_specs = [pl.BlockSpec((None, d, ATT_HEADS, tm // d, LANES), o_map) for d in DILATIONS]
    st_specs = [pl.BlockSpec((None, d, tm // d, LANES), st_map) for d in DILATIONS]
    const2 = lambda t: (0, 0)
    args = list(o_pats) + list(stats)
    return pl.pallas_call(
        functools.partial(_merge_kernel, tm=tm, total=total),
        grid=(total + 1,),
        in_specs=o_specs + st_specs + [
            pl.BlockSpec((None, tm, ATT_WIDTH), cur_map),
            pl.BlockSpec((None, tm, POOL_WIDTH), cur_map),
            pl.BlockSpec((None, tm, dm), prev_map),
            pl.BlockSpec(w_out.shape, const2, pipeline_mode=pl.Buffered(1)),
            pl.BlockSpec((1, dm), const2),
            pl.BlockSpec((1, dm), const2)],
        out_specs=[pl.BlockSpec((None, tm, dm), prev_map), pl.BlockSpec((None, tm, dm), prev_map)],
        out_shape=[jax.ShapeDtypeStruct((b, s, dm), F32), jax.ShapeDtypeStruct((b, s, dm), BF16)],
        scratch_shapes=[pltpu.VMEM((tm, ATT_WIDTH + POOL_WIDTH), BF16),
                        pltpu.VMEM((tm, ATT_WIDTH + POOL_WIDTH), BF16),
                        pltpu.VMEM((3, tm, LANES), F32),
                        pltpu.VMEM((2, tm, LANES), F32),
                        pltpu.VMEM((tm, dm), F32)],
        compiler_params=pltpu.CompilerParams(dimension_semantics=("arbitrary",), vmem_limit_bytes=VMEM_LIMIT),
        name="merge_outproj",
    )(*args, ga, ob, x, w_out, ln_g, ln_b)


def _conv_in_kernel(xp_ref, x_ref, xn_ref, wb_ref, wc_ref, wv_ref, wg_ref, cw_ref, o_ref, cv_ref, *, tm, seq):
    i = pl.program_id(0)
    tiles_per_seq = seq // tm
    keep_prev = jnp.where((i % tiles_per_seq) == 0, 0.0, 1.0)
    keep_next = jnp.where((i % tiles_per_seq) == tiles_per_seq - 1, 0.0, 1.0)
    nch = tm // ROW_CHUNK

    def conv_input(r):
        parts = [x_ref[r * ROW_CHUNK:(r + 1) * ROW_CHUNK, :]]
        if r == 0:
            parts.insert(0, xp_ref[...])
        if r == nch - 1:
            parts.append(xn_ref[...])
        xa = parts[0] if len(parts) == 1 else jnp.concatenate(parts, axis=0)
        lo = HALO + r * ROW_CHUNK - (HALO if r == 0 else 0)
        cv_ref[lo:lo + xa.shape[0]] = (jnp.dot(xa, wc_ref[...], preferred_element_type=F32)
                                       * jnp.dot(xa, wv_ref[...], preferred_element_type=F32))
        if r == 0:
            cv_ref[0:HALO] = cv_ref[0:HALO] * keep_prev
        if r == nch - 1:
            cv_ref[HALO + tm:] = cv_ref[HALO + tm:] * keep_next

    def gated_conv(r):
        base = HALO + r * ROW_CHUNK
        xm = x_ref[r * ROW_CHUNK:(r + 1) * ROW_CHUNK, :]
        gb = jnp.dot(xm, wb_ref[...], preferred_element_type=F32)
        gate = jnp.dot(xm, wg_ref[...], preferred_element_type=F32)
        conv = (cv_ref[pl.ds(base - 1, ROW_CHUNK), :] * cw_ref[0:1, :]
                + cv_ref[pl.ds(base, ROW_CHUNK), :] * cw_ref[1:2, :]
                + cv_ref[pl.ds(base + 1, ROW_CHUNK), :] * cw_ref[2:3, :])
        o_ref[r * ROW_CHUNK:(r + 1) * ROW_CHUNK, :] = (gb * conv * _silu(gate)).astype(BF16)

    conv_input(0)
    for r in range(nch):
        if r + 1 < nch:
            conv_input(r + 1)
        gated_conv(r)


def _conv_inproj(xb, w_in, conv_w, *, seq, tm=1024, tc=1024):
    m, dm = xb.shape
    hb = tm // HALO
    nhb = m // HALO
    nc = CONV_WIDTH // tc

    def wspec(part):
        return pl.BlockSpec((dm, tc), lambda i, c: (0, part * nc + c))

    return pl.pallas_call(
        functools.partial(_conv_in_kernel, tm=tm, seq=seq),
        grid=(m // tm, nc),
        in_specs=[pl.BlockSpec((HALO, dm), lambda i, c: (jnp.maximum(i * hb - 1, 0), 0)),
                  pl.BlockSpec((tm, dm), lambda i, c: (i, 0)),
                  pl.BlockSpec((HALO, dm), lambda i, c: (jnp.minimum((i + 1) * hb, nhb - 1), 0)),
                  wspec(0), wspec(1), wspec(2), wspec(3),
                  pl.BlockSpec((3, tc), lambda i, c: (0, c))],
        out_specs=pl.BlockSpec((tm, tc), lambda i, c: (i, c)),
        out_shape=jax.ShapeDtypeStruct((m, CONV_WIDTH), BF16),
        scratch_shapes=[pltpu.VMEM((tm + 2 * HALO, tc), F32)],
        compiler_params=pltpu.CompilerParams(
            dimension_semantics=("parallel", "parallel"), vmem_limit_bytes=VMEM_LIMIT),
        name="conv_inproj",
    )(xb, xb, xb, w_in, w_in, w_in, w_in, conv_w)


def _outproj_ln_kernel(y_ref, x_ref, w_ref, g_ref, b_ref, o_ref):
    for r in range(0, y_ref.shape[0], ROW_CHUNK):
        rows = slice(r, r + ROW_CHUNK)
        f = jnp.dot(y_ref[rows, :], w_ref[...], preferred_element_type=F32)
        o_ref[rows, :] = _layernorm(DN_ALPHA * x_ref[rows, :] + f, g_ref[...], b_ref[...])


def _outproj_ln(y, x2, w_out, ln_g, ln_b, *, tm=512):
    m, dm = x2.shape
    const2 = lambda i: (0, 0)
    return pl.pallas_call(
        _outproj_ln_kernel,
        grid=(m // tm,),
        in_specs=[pl.BlockSpec((tm, y.shape[1]), lambda i: (i, 0)),
                  pl.BlockSpec((tm, dm), lambda i: (i, 0)),
                  pl.BlockSpec(w_out.shape, const2, pipeline_mode=pl.Buffered(1)),
                  pl.BlockSpec((1, dm), const2),
                  pl.BlockSpec((1, dm), const2)],
        out_specs=pl.BlockSpec((tm, dm), lambda i: (i, 0)),
        out_shape=jax.ShapeDtypeStruct((m, dm), F32),
        compiler_params=pltpu.CompilerParams(dimension_semantics=("parallel",), vmem_limit_bytes=VMEM_LIMIT),
        name="outproj_ln",
    )(y, x2, w_out, ln_g, ln_b)


def _trunk(x, biases, w_ab, pool_w, pool_scale, w_out_ab, w_in_c, conv_w, w_out_c, ln_g, ln_b):
    b, s, dm = x.shape
    x2 = x.reshape(b * s, dm)
    *hds, ga = _qkvg_proj(x, w_ab)
    ob = _pool_branch(x2, w_ab, pool_w, pool_scale, seq=s)
    o_pats, stats = [], []
    for d, hd, bias in zip(DILATIONS, hds, biases):
        ld = s // d
        o, st = _banded_attention(hd.reshape(b * d, 3 * ATT_HEADS, ld, LANES), bias)
        o_pats.append(o.reshape(b, d, ATT_HEADS, ld, LANES))
        stats.append(st.reshape(b, d, ld, LANES))
    x1, x1b = _merge_outproj(o_pats, stats, ga, ob, x, w_out_ab, ln_g[0:1], ln_b[0:1])
    y = _conv_inproj(x1b.reshape(b * s, dm), w_in_c, conv_w, seq=s)
    out = _outproj_ln(y, x1.reshape(b * s, dm), w_out_c, ln_g[1:2], ln_b[1:2])
    return out.reshape(b, s, dm)


def kernel(x_prompt, x_sample, rel_bias, w_in_ab, pool_w, pool_scale, w_out_ab, w_in_c, conv_w, w_out_c, ln_g, ln_b):
    assert DEPTH == 2 and w_in_ab.shape[0] == 1 and w_in_c.shape[0] == 1
    params = dict(
        biases=[_band_bias(rel_bias, d) for d in DILATIONS],
        w_ab=w_in_ab[0].astype(BF16),
        pool_w=pool_w[0].astype(BF16),
        pool_scale=pool_scale[0].reshape(1, POOL_WIDTH),
        w_out_ab=w_out_ab[0].astype(BF16),
        w_in_c=w_in_c[0].astype(BF16),
        conv_w=conv_w[0],
        w_out_c=w_out_c[0].astype(BF16),
        ln_g=ln_g,
        ln_b=ln_b,
    )
    return (_trunk(x_prompt, **params), _trunk(x_sample, **params))
```

```python
import functools
import math

import numpy as np
import jax
import jax.numpy as jnp
from jax import lax
from jax.experimental import pallas as pl
from jax.experimental.pallas import tpu as pltpu

D_MODEL = 2048
DEPTH = 2
ATT_HEADS = 16
HEAD_DIM = 128
ATT_WIDTH = ATT_HEADS * HEAD_DIM
DILATED_PATTERNS = ((128, 1), (512, 4), (2048, 16))
DILATIONS = tuple(d for _, d in DILATED_PATTERNS)
RADIUS = 64
POOL_WINDOWS = (2, 4, 8, 16)
POOL_WIDTH = D_MODEL // 2
POOL_GROUP = POOL_WIDTH // len(POOL_WINDOWS)
CONV_WIDTH = D_MODEL
REL_BUCKETS = 32
REL_MAX_DISTANCE = 1024
DN_ALPHA = (2 * DEPTH) ** 0.25
LN_EPS = 1e-5
NEG_INF = -1e30
LOG2E = math.log2(math.e)

LANES = 128
HALO = 16
ATT_Q_ROWS = 128
ATT_KEYS = ATT_Q_ROWS + 2 * RADIUS
ATT_UNITS = 4
ATT_TRIP_STEPS = 32
ATT_BLOCK_ROWS = 1024
ATT_MIN_STEP_ROWS = 512
PROJ_ROWS = 256
ROW_CHUNK = 128
VMEM_LIMIT = 56 * 1024 * 1024

BF16 = jnp.bfloat16
F32 = jnp.float32

assert all(w // (2 * d) == RADIUS for w, d in DILATED_PATTERNS)


def _t5_bucket(rel):
    nb = REL_BUCKETS // 2
    max_exact = nb // 2
    ret = np.where(rel > 0, nb, 0)
    n = np.abs(rel)
    n_safe = np.maximum(n, 1).astype(np.float64)
    large = max_exact + (np.log(n_safe / max_exact) / math.log(REL_MAX_DISTANCE / max_exact)
                         * (nb - max_exact)).astype(np.int64)
    large = np.minimum(large, nb - 1)
    return (ret + np.where(n < max_exact, n, large)).astype(np.int32)


def _band_bias(rel_bias, dilation):
    rel = np.arange(ATT_KEYS)[None, :] - RADIUS - np.arange(ATT_Q_ROWS)[:, None]
    bucket = jnp.asarray(_t5_bucket(rel * dilation).reshape(1, -1))
    onehot = (bucket == jnp.arange(REL_BUCKETS, dtype=jnp.int32)[:, None]).astype(F32)
    bias = jnp.dot(rel_bias.astype(F32).T, onehot, precision=lax.Precision.HIGHEST)
    bias = bias.reshape(1, ATT_HEADS, ATT_Q_ROWS, ATT_KEYS)
    key = np.arange(ATT_KEYS)[None, :]
    in_band = np.abs(rel) <= RADIUS
    after_start = key >= RADIUS
    before_end = key < ATT_Q_ROWS + RADIUS
    keep = np.stack([in_band, in_band & after_start, in_band & before_end, in_band & after_start & before_end])
    table = jnp.where(jnp.asarray(keep[:, None]), bias * LOG2E, NEG_INF)
    return table.transpose(0, 1, 3, 2).astype(BF16)


def _silu(x):
    return x / (1.0 + jnp.exp(-x))


def _layernorm(z, g, b):
    mu = jnp.mean(z, axis=-1, keepdims=True)
    zc = z - mu
    var = jnp.mean(zc * zc, axis=-1, keepdims=True)
    return zc * lax.rsqrt(var + LN_EPS) * g + b


def _qkvg_kernel(x_ref, w_ref, o1_ref, o4_ref, o16_ref, ga_ref, xb_ref, s0_ref, s1_ref, mid0_ref, mid1_ref,
                 *, tm, tn, nq):
    j = pl.program_id(2)
    chunk_bufs = ((s0_ref, mid0_ref), (s1_ref, mid1_ref))
    r4, r16 = PROJ_ROWS // 4, PROJ_ROWS // 16

    @pl.when(j == 0)
    def _():
        xb_ref[...] = x_ref[...].astype(BF16)

    def product(rc, cc):
        rows = slice(rc * PROJ_ROWS, (rc + 1) * PROJ_ROWS)
        cols = slice(cc * 2 * LANES, (cc + 1) * 2 * LANES)
        return jnp.dot(xb_ref[rows, :], w_ref[:, cols], preferred_element_type=F32)

    def emit_qkv(acc, rc, cc, s_ref, mid_ref):
        acc = acc * jnp.where(j < ATT_WIDTH // tn, LOG2E * HEAD_DIM ** -0.5, 1.0)
        for c in range(2):
            slab = 2 * cc + c
            blk = acc[:, c * LANES:(c + 1) * LANES]
            o1_ref[0, slab, rc * PROJ_ROWS:(rc + 1) * PROJ_ROWS, :] = blk.astype(BF16)
            s_ref[c] = blk
            for g4 in range(4):
                rows4 = s_ref[c, pl.ds(g4, r4, stride=4), :]
                o4_ref[g4, slab, rc * r4:(rc + 1) * r4, :] = rows4.astype(BF16)
                mid_ref[c, g4] = rows4
            for g4 in range(4):
                for k in range(4):
                    o16_ref[g4 + 4 * k, slab, rc * r16:(rc + 1) * r16, :] = (
                        mid_ref[c, g4, pl.ds(k, r16, stride=4), :].astype(BF16))

    def emit_gate(acc, rc, cc):
        ga_ref[rc * PROJ_ROWS:(rc + 1) * PROJ_ROWS, cc * 2 * LANES:(cc + 1) * 2 * LANES] = _silu(acc).astype(BF16)

    chunks = [(rc, cc) for cc in range(tn // (2 * LANES)) for rc in range(tm // PROJ_ROWS)]

    @pl.when(j < nq)
    def _():
        for n, (rc, cc) in enumerate(chunks):
            emit_qkv(product(rc, cc), rc, cc, *chunk_bufs[n % 2])

    @pl.when(j >= nq)
    def _():
        for rc, cc in chunks:
            emit_gate(product(rc, cc), rc, cc)


def _qkvg_proj(x, w, *, tm=1024, tn=1024):
    b, s, dm = x.shape
    nj = 4 * ATT_WIDTH // tn
    nq = 3 * ATT_WIDTH // tn
    qt = ATT_WIDTH // tn
    nslab = 3 * ATT_WIDTH // LANES
    outs = [jax.ShapeDtypeStruct((b, d, nslab, s // d, LANES), BF16) for d in DILATIONS]
    outs.append(jax.ShapeDtypeStruct((b, s, ATT_WIDTH), BF16))
    out_specs = [pl.BlockSpec((None, d, tn // LANES, tm // d, LANES),
                              lambda bi, i, j: (bi, 0, (jnp.minimum(j, nq - 1) + 2 * qt) % nq, i, 0)) for d in DILATIONS]
    out_specs.append(pl.BlockSpec((None, tm, tn), lambda bi, i, j: (bi, i, jnp.maximum(j - nq, 0))))
    return pl.pallas_call(
        functools.partial(_qkvg_kernel, tm=tm, tn=tn, nq=nq),
        grid=(b, s // tm, nj),
        in_specs=[pl.BlockSpec((None, tm, dm), lambda bi, i, j: (bi, i, 0)),
                  pl.BlockSpec((dm, tn), lambda bi, i, j: (0, j))],
        out_specs=out_specs,
        out_shape=outs,
        scratch_shapes=[pltpu.VMEM((tm, dm), BF16)]
        + [pltpu.VMEM((2, PROJ_ROWS, LANES), F32)] * 2
        + [pltpu.VMEM((2, 4, PROJ_ROWS // 4, LANES), F32)] * 2,
        compiler_params=pltpu.CompilerParams(
            dimension_semantics=("parallel", "parallel", "arbitrary"), vmem_limit_bytes=VMEM_LIMIT),
        name="qkvg_proj",
    )(x, w)


def _pool_kernel(xp_ref, x_ref, xn_ref, wu_ref, wg_ref, pw_ref, ps_ref, o_ref, xb_ref, u_ref, *, tm, seq):
    i = pl.program_id(0)
    tiles_per_seq = seq // tm
    keep_prev = jnp.where((i % tiles_per_seq) == 0, 0.0, 1.0)
    keep_next = jnp.where((i % tiles_per_seq) == tiles_per_seq - 1, 0.0, 1.0)
    nch = tm // ROW_CHUNK

    xb_ref[0:HALO] = xp_ref[...].astype(BF16)
    xb_ref[HALO:HALO + tm] = x_ref[...].astype(BF16)
    xb_ref[HALO + tm:] = xn_ref[...].astype(BF16)

    def pool_input(r):
        lo = HALO + r * ROW_CHUNK - (HALO if r == 0 else 0)
        hi = HALO + (r + 1) * ROW_CHUNK + (HALO if r == nch - 1 else 0)
        u_ref[lo:hi] = jnp.dot(xb_ref[lo:hi], wu_ref[...], preferred_element_type=F32)
        if r == 0:
            u_ref[0:HALO] = u_ref[0:HALO] * keep_prev
        if r == nch - 1:
            u_ref[HALO + tm:] = u_ref[HALO + tm:] * keep_next

    def pooled_out(r):
        base = HALO + r * ROW_CHUNK
        rows = slice(r * ROW_CHUNK, (r + 1) * ROW_CHUNK)
        gate = jnp.dot(xb_ref[base:base + ROW_CHUNK], wg_ref[...], preferred_element_type=F32)
        pos = ((i % tiles_per_seq) * tm + r * ROW_CHUNK
               + lax.broadcasted_iota(jnp.int32, (ROW_CHUNK, POOL_GROUP), 0))
        for gi, w in enumerate(POOL_WINDOWS):
            cols = slice(gi * POOL_GROUP, (gi + 1) * POOL_GROUP)
            tot = u_ref[pl.ds(base - w // 2, ROW_CHUNK), cols]
            for off in range(-w // 2 + 1, w // 2):
                tot = tot + u_ref[pl.ds(base + off, ROW_CHUNK), cols]
            lo = jnp.maximum(pos - w // 2, 0)
            hi = jnp.minimum(pos + w // 2 - 1, seq - 1)
            cnt = (hi - lo + 1).astype(F32)
            pooled = tot / cnt - u_ref[pl.ds(base, ROW_CHUNK), cols]
            ob = jnp.dot(pooled.astype(BF16), pw_ref[gi], preferred_element_type=F32)
            o_ref[rows, cols] = (ob * ps_ref[:, cols] * _silu(gate[:, cols])).astype(BF16)

    pool_input(0)
    for r in range(nch):
        if r + 1 < nch:
            pool_input(r + 1)
        pooled_out(r)


def _pool_branch(x2, w_ab, pool_w, pool_scale, *, seq, tm=512):
    m, dm = x2.shape
    ub = 4 * ATT_WIDTH // POOL_WIDTH
    hb = tm // HALO
    nhb = m // HALO
    return pl.pallas_call(
        functools.partial(_pool_kernel, tm=tm, seq=seq),
        grid=(m // tm,),
        in_specs=[pl.BlockSpec((HALO, dm), lambda i: (jnp.maximum(i * hb - 1, 0), 0)),
                  pl.BlockSpec((tm, dm), lambda i: (i, 0)),
                  pl.BlockSpec((HALO, dm), lambda i: (jnp.minimum((i + 1) * hb, nhb - 1), 0)),
                  pl.BlockSpec((dm, POOL_WIDTH), lambda i: (0, ub)),
                  pl.BlockSpec((dm, POOL_WIDTH), lambda i: (0, ub + 1)),
                  pl.BlockSpec((len(POOL_WINDOWS), POOL_GROUP, POOL_GROUP), lambda i: (0, 0, 0)),
                  pl.BlockSpec((1, POOL_WIDTH), lambda i: (0, 0))],
        out_specs=pl.BlockSpec((tm, POOL_WIDTH), lambda i: (i, 0)),
        out_shape=jax.ShapeDtypeStruct((m, POOL_WIDTH), BF16),
        scratch_shapes=[pltpu.VMEM((tm + 2 * HALO, dm), BF16), pltpu.VMEM((tm + 2 * HALO, POOL_WIDTH), F32)],
        compiler_params=pltpu.CompilerParams(dimension_semantics=("parallel",), vmem_limit_bytes=VMEM_LIMIT),
        name="pool_branch",
    )(x2, x2, x2, w_ab, w_ab, pool_w, pool_scale)


def _attn_kernel(q_ref, kvp_ref, kvm_ref, kvn_ref, bias_ref, o_ref, st_ref, s0_ref, s1_ref, *, qb, nblk, ns):
    n = pl.program_id(1)
    na = qb // ATT_Q_ROWS
    ng = max(1, na // ATT_UNITS)
    apg = na // ng
    hpi = max(1, ATT_UNITS // na)
    steps = (ATT_HEADS // hpi) * ng
    all_steps = ns * steps
    trip = min(ATT_TRIP_STEPS, all_steps)
    assert all_steps % trip == 0 and trip % (2 * ng) == 0 and steps % ng == 0
    lane = lax.broadcasted_iota(jnp.int32, (ATT_Q_ROWS, LANES), 1)
    st_ref[...] = jnp.broadcast_to(
        jnp.where((lane[:1] >= ATT_HEADS) & (lane[:1] < 2 * ATT_HEADS), 1.0, 0.0), st_ref.shape)
    first = (n == 0).astype(jnp.int32)
    last = (n == nblk - 1).astype(jnp.int32)
    s_refs = (s0_ref, s1_ref)
    ones = jnp.ones((ATT_KEYS, LANES), BF16)
    ident = (lax.broadcasted_iota(jnp.int32, (ATT_Q_ROWS, LANES), 0) == lane).astype(BF16)

    def window(si, slab, a):
        lo = a * ATT_Q_ROWS - RADIUS
        hi = lo + ATT_KEYS
        parts = []
        if lo < 0:
            parts.append(kvp_ref[si, slab])
        parts.append(kvm_ref[si, slab, max(lo, 0):min(hi, qb), :])
        if hi > qb:
            parts.append(kvn_ref[si, slab])
        return parts[0] if len(parts) == 1 else jnp.concatenate(parts, axis=0)

    def scores(si, step, grp, slot):
        for j in range(hpi):
            h = (step // ng) * hpi + j
            for ai in range(apg):
                a = grp * apg + ai
                variant = (first if a == 0 else 0) + (2 * last if a == na - 1 else 0)
                q = jnp.concatenate([q_ref[si, h, a * ATT_Q_ROWS:(a + 1) * ATT_Q_ROWS, :], ident], axis=1)
                k = jnp.concatenate([window(si, h, a), bias_ref[variant, h]], axis=1)
                s_refs[slot][j * apg + ai] = lax.dot_general(
                    q, k, (((1,), (1,)), ((), ())), preferred_element_type=F32)

    def softmax_pv(si, step, grp, slot):
        for j in range(hpi):
            h = (step // ng) * hpi + j
            for ai in range(apg):
                a = grp * apg + ai
                rows = slice(a * ATT_Q_ROWS, (a + 1) * ATT_Q_ROWS)
                s = s_refs[slot][j * apg + ai]
                mx = jnp.max(s, axis=-1, keepdims=True)
                p = jnp.exp2(s - mx)
                v = window(si, ATT_HEADS + h, a)
                pv = jnp.dot(p.astype(BF16), jnp.concatenate([v, ones], axis=1), preferred_element_type=F32)
                o_ref[si, h, rows, :] = pv[:, :HEAD_DIM].astype(BF16)
                st_ref[si, rows, :] = jnp.where(
                    lane == h, mx, jnp.where(lane == h + ATT_HEADS, pv[:, HEAD_DIM:], st_ref[si, rows, :]))

    def locate(g):
        return (g // steps, g % steps) if ns > 1 else (0, g)

    scores(0, 0, 0, 0)

    def body(tt, c):
        for u in range(trip):
            g = trip * tt + u
            scores(*locate((g + 1) % all_steps), (u + 1) % ng, (u + 1) % 2)
            softmax_pv(*locate(g), u % ng, u % 2)
        return c

    lax.fori_loop(0, all_steps // trip, body, 0)


def _banded_attention(hd, bias):
    nseq, _, length, _ = hd.shape
    qb = min(ATT_BLOCK_ROWS, length)
    nblk = length // qb
    ns = max(1, ATT_MIN_STEP_ROWS // length)
    hb = qb // RADIUS
    nh = length // RADIUS
    units = max(1, ATT_UNITS // (qb // ATT_Q_ROWS)) * min(ATT_UNITS, qb // ATT_Q_ROWS)
    return pl.pallas_call(
        functools.partial(_attn_kernel, qb=qb, nblk=nblk, ns=ns),
        grid=(nseq // ns, nblk),
        in_specs=[pl.BlockSpec((ns, ATT_HEADS, qb, LANES), lambda s, n: (s, 2, n, 0)),
                  pl.BlockSpec((ns, 2 * ATT_HEADS, RADIUS, LANES),
                               lambda s, n: (s, 0, jnp.maximum(n * hb - 1, 0), 0)),
                  pl.BlockSpec((ns, 2 * ATT_HEADS, qb, LANES), lambda s, n: (s, 0, n, 0)),
                  pl.BlockSpec((ns, 2 * ATT_HEADS, RADIUS, LANES),
                               lambda s, n: (s, 0, jnp.minimum((n + 1) * hb, nh - 1), 0)),
                  pl.BlockSpec(bias.shape, lambda s, n: (0, 0, 0, 0), pipeline_mode=pl.Buffered(1))],
        out_specs=[pl.BlockSpec((ns, ATT_HEADS, qb, LANES), lambda s, n: (s, 0, n, 0)),
                   pl.BlockSpec((ns, qb, LANES), lambda s, n: (s, n, 0))],
        out_shape=[jax.ShapeDtypeStruct((nseq, ATT_HEADS, length, LANES), BF16),
                   jax.ShapeDtypeStruct((nseq, length, LANES), F32)],
        scratch_shapes=[pltpu.VMEM((units, ATT_Q_ROWS, ATT_KEYS), F32)] * 2,
        compiler_params=pltpu.CompilerParams(
            dimension_semantics=("parallel", "parallel"), vmem_limit_bytes=VMEM_LIMIT),
        name="banded_attention",
    )(hd, hd, hd, hd, bias)


def _merge_kernel(o1_ref, o4_ref, o16_ref, s1_ref, s4_ref, s16_ref,
                  ga_ref, ob_ref, x_ref, w_ref, g_ref, b_ref, out_ref, outb_ref,
                  y0_ref, y1_ref, st_ref, r_ref, f_ref, *, tm, total):
    t = pl.program_id(0)
    ys = (y0_ref, y1_ref)
    ncol = w_ref.shape[1] // (2 * LANES)
    hpp = ATT_HEADS // ncol

    def merge_weights():
        st_ref[0] = s1_ref[0]
        for p, (d, sr) in enumerate(((4, s4_ref), (16, s16_ref)), start=1):
            for g in range(d):
                st_ref[p, pl.ds(g, tm // d, stride=d), :] = sr[g]
        ms = [st_ref[p] for p in range(3)]
        mx = jnp.maximum(jnp.maximum(ms[0], ms[1]), ms[2])
        es = [jnp.exp2(m - mx) for m in ms]
        den = sum(e * pltpu.roll(m, LANES - ATT_HEADS, axis=1) for e, m in zip(es, ms))
        head_lane = lax.broadcasted_iota(jnp.int32, den.shape, 1) < ATT_HEADS
        den = jnp.where(head_lane, den, 1.0)
        for p in range(3):
            st_ref[p] = es[p] / den

    def merge_head(dst, h):
        for p, (d, o_ref) in enumerate(((4, o4_ref), (16, o16_ref))):
            for g in range(d):
                r_ref[p, pl.ds(g, tm // d, stride=d), :] = o_ref[g, h].astype(F32)
        oa = (st_ref[0, :, h:h + 1] * o1_ref[0, h].astype(F32)
              + st_ref[1, :, h:h + 1] * r_ref[0]
              + st_ref[2, :, h:h + 1] * r_ref[1])
        cols = slice(h * HEAD_DIM, (h + 1) * HEAD_DIM)
        dst[:, cols] = (oa * ga_ref[:, cols].astype(F32)).astype(BF16)

    def product(src, k):
        cols = slice(k * 2 * LANES, (k + 1) * 2 * LANES)
        f_ref[:, cols] = jnp.dot(src[...], w_ref[:, cols], preferred_element_type=F32)
        if k == ncol - 1:
            for r in range(0, tm, ROW_CHUNK):
                rows = slice(r, r + ROW_CHUNK)
                y = _layernorm(DN_ALPHA * x_ref[rows, :] + f_ref[rows, :], g_ref[...], b_ref[...])
                out_ref[rows, :] = y
                outb_ref[rows, :] = y.astype(BF16)

    @pl.when(t == 0)
    def _():
        merge_weights()
        for h in range(ATT_HEADS):
            merge_head(ys[0], h)
        ys[0][:, ATT_WIDTH:] = ob_ref[...]

    @pl.when(t == total)
    def _():
        for k in range(ncol):
            product(ys[(total - 1) % 2], k)

    for parity in (0, 1):
        @pl.when((t > 0) & (t < total) & (t % 2 == parity))
        def _(parity=parity):
            merge_weights()
            for k in range(ncol):
                product(ys[1 - parity], k)
                for h in range(k * hpp, (k + 1) * hpp):
                    merge_head(ys[parity], h)
            ys[parity][:, ATT_WIDTH:] = ob_ref[...]


def _merge_outproj(o_pats, stats, ga, ob, x, w_out, ln_g, ln_b, *, tm=256):
    b, s, dm = x.shape
    nt = s // tm
    total = b * nt
    ga = ga.reshape(b, s, ATT_WIDTH)
    ob = ob.reshape(b, s, POOL_WIDTH)

    def cur(t):
        t = jnp.minimum(t, total - 1)
        return t // nt, t % nt

    def prev(t):
        t = jnp.maximum(t - 1, 0)
        return t // nt, t % nt

    def o_map(t):
        bi, i = cur(t)
        return bi, 0, 0, i, 0

    def st_map(t):
        bi, i = cur(t)
        return bi, 0, i, 0

    def cur_map(t):
        bi, i = cur(t)
        return bi, i, 0

    def prev_map(t):
        bi, i = prev(t)
        return bi, i, 0

    o_specs = [pl.BlockSpec((None, d, ATT_HEADS, tm // d, LANES), o_map) for d in DILATIONS]
    st_specs = [pl.BlockSpec((None, d, tm // d, LANES), st_map) for d in DILATIONS]
    const2 = lambda t: (0, 0)
    args = list(o_pats) + list(stats)
    return pl.pallas_call(
        functools.partial(_merge_kernel, tm=tm, total=total),
        grid=(total + 1,),
        in_specs=o_specs + st_specs + [
            pl.BlockSpec((None, tm, ATT_WIDTH), cur_map),
            pl.BlockSpec((None, tm, POOL_WIDTH), cur_map),
            pl.BlockSpec((None, tm, dm), prev_map),
            pl.BlockSpec(w_out.shape, const2, pipeline_mode=pl.Buffered(1)),
            pl.BlockSpec((1, dm), const2),
            pl.BlockSpec((1, dm), const2)],
        out_specs=[pl.BlockSpec((None, tm, dm), prev_map), pl.BlockSpec((None, tm, dm), prev_map)],
        out_shape=[jax.ShapeDtypeStruct((b, s, dm), F32), jax.ShapeDtypeStruct((b, s, dm), BF16)],
        scratch_shapes=[pltpu.VMEM((tm, ATT_WIDTH + POOL_WIDTH), BF16),
                        pltpu.VMEM((tm, ATT_WIDTH + POOL_WIDTH), BF16),
                        pltpu.VMEM((3, tm, LANES), F32),
                        pltpu.VMEM((2, tm, LANES), F32),
                        pltpu.VMEM((tm, dm), F32)],
        compiler_params=pltpu.CompilerParams(dimension_semantics=("arbitrary",), vmem_limit_bytes=VMEM_LIMIT),
        name="merge_outproj",
    )(*args, ga, ob, x, w_out, ln_g, ln_b)


def _conv_in_kernel(xp_ref, x_ref, xn_ref, wb_ref, wc_ref, wv_ref, wg_ref, cw_ref, o_ref, cv_ref, *, tm, seq):
    i = pl.program_id(0)
    tiles_per_seq = seq // tm
    keep_prev = jnp.where((i % tiles_per_seq) == 0, 0.0, 1.0)
    keep_next = jnp.where((i % tiles_per_seq) == tiles_per_seq - 1, 0.0, 1.0)
    nch = tm // ROW_CHUNK

    def conv_input(r):
        parts = [x_ref[r * ROW_CHUNK:(r + 1) * ROW_CHUNK, :]]
        if r == 0:
            parts.insert(0, xp_ref[...])
        if r == nch - 1:
            parts.append(xn_ref[...])
        xa = parts[0] if len(parts) == 1 else jnp.concatenate(parts, axis=0)
        lo = HALO + r * ROW_CHUNK - (HALO if r == 0 else 0)
        cv_ref[lo:lo + xa.shape[0]] = (jnp.dot(xa, wc_ref[...], preferred_element_type=F32)
                                       * jnp.dot(xa, wv_ref[...], preferred_element_type=F32))
        if r == 0:
            cv_ref[0:HALO] = cv_ref[0:HALO] * keep_prev
        if r == nch - 1:
            cv_ref[HALO + tm:] = cv_ref[HALO + tm:] * keep_next

    def gated_conv(r):
        base = HALO + r * ROW_CHUNK
        xm = x_ref[r * ROW_CHUNK:(r + 1) * ROW_CHUNK, :]
        gb = jnp.dot(xm, wb_ref[...], preferred_element_type=F32)
        gate = jnp.dot(xm, wg_ref[...], preferred_element_type=F32)
        conv = (cv_ref[pl.ds(base - 1, ROW_CHUNK), :] * cw_ref[0:1, :]
                + cv_ref[pl.ds(base, ROW_CHUNK), :] * cw_ref[1:2, :]
                + cv_ref[pl.ds(base + 1, ROW_CHUNK), :] * cw_ref[2:3, :])
        o_ref[r * ROW_CHUNK:(r + 1) * ROW_CHUNK, :] = (gb * conv * _silu(gate)).astype(BF16)

    conv_input(0)
    for r in range(nch):
        if r + 1 < nch:
            conv_input(r + 1)
        gated_conv(r)


def _conv_inproj(xb, w_in, conv_w, *, seq, tm=1024, tc=1024):
    m, dm = xb.shape
    hb = tm // HALO
    nhb = m // HALO
    nc = CONV_WIDTH // tc

    def wspec(part):
        return pl.BlockSpec((dm, tc), lambda i, c: (0, part * nc + c))

    return pl.pallas_call(
        functools.partial(_conv_in_kernel, tm=tm, seq=seq),
        grid=(m // tm, nc),
        in_specs=[pl.BlockSpec((HALO, dm), lambda i, c: (jnp.maximum(i * hb - 1, 0), 0)),
                  pl.BlockSpec((tm, dm), lambda i, c: (i, 0)),
                  pl.BlockSpec((HALO, dm), lambda i, c: (jnp.minimum((i + 1) * hb, nhb - 1), 0)),
                  wspec(0), wspec(1), wspec(2), wspec(3),
                  pl.BlockSpec((3, tc), lambda i, c: (0, c))],
        out_specs=pl.BlockSpec((tm, tc), lambda i, c: (i, c)),
        out_shape=jax.ShapeDtypeStruct((m, CONV_WIDTH), BF16),
        scratch_shapes=[pltpu.VMEM((tm + 2 * HALO, tc), F32)],
        compiler_params=pltpu.CompilerParams(
            dimension_semantics=("parallel", "parallel"), vmem_limit_bytes=VMEM_LIMIT),
        name="conv_inproj",
    )(xb, xb, xb, w_in, w_in, w_in, w_in, conv_w)


def _outproj_ln_kernel(y_ref, x_ref, w_ref, g_ref, b_ref, o_ref):
    for r in range(0, y_ref.shape[0], ROW_CHUNK):
        rows = slice(r, r + ROW_CHUNK)
        f = jnp.dot(y_ref[rows, :], w_ref[...], preferred_element_type=F32)
        o_ref[rows, :] = _layernorm(DN_ALPHA * x_ref[rows, :] + f, g_ref[...], b_ref[...])


def _outproj_ln(y, x2, w_out, ln_g, ln_b, *, tm=512):
    m, dm = x2.shape
    const2 = lambda i: (0, 0)
    return pl.pallas_call(
        _outproj_ln_kernel,
        grid=(m // tm,),
        in_specs=[pl.BlockSpec((tm, y.shape[1]), lambda i: (i, 0)),
                  pl.BlockSpec((tm, dm), lambda i: (i, 0)),
                  pl.BlockSpec(w_out.shape, const2, pipeline_mode=pl.Buffered(1)),
                  pl.BlockSpec((1, dm), const2),
                  pl.BlockSpec((1, dm), const2)],
        out_specs=pl.BlockSpec((tm, dm), lambda i: (i, 0)),
        out_shape=jax.ShapeDtypeStruct((m, dm), F32),
        compiler_params=pltpu.CompilerParams(dimension_semantics=("parallel",), vmem_limit_bytes=VMEM_LIMIT),
        name="outproj_ln",
    )(y, x2, w_out, ln_g, ln_b)


def _trunk(x, biases, w_ab, pool_w, pool_scale, w_out_ab, w_in_c, conv_w, w_out_c, ln_g, ln_b):
    b, s, dm = x.shape
    x2 = x.reshape(b * s, dm)
    *hds, ga = _qkvg_proj(x, w_ab)
    ob = _pool_branch(x2, w_ab, pool_w, pool_scale, seq=s)
    o_pats, stats = [], []
    for d, hd, bias in zip(DILATIONS, hds, biases):
        ld = s // d
        o, st = _banded_attention(hd.reshape(b * d, 3 * ATT_HEADS, ld, LANES), bias)
        o_pats.append(o.reshape(b, d, ATT_HEADS, ld, LANES))
        stats.append(st.reshape(b, d, ld, LANES))
    x1, x1b = _merge_outproj(o_pats, stats, ga, ob, x, w_out_ab, ln_g[0:1], ln_b[0:1])
    y = _conv_inproj(x1b.reshape(b * s, dm), w_in_c, conv_w, seq=s)
    out = _outproj_ln(y, x1.reshape(b * s, dm), w_out_c, ln_g[1:2], ln_b[1:2])
    return out.reshape(b, s, dm)


def kernel(x_prompt, x_sample, rel_bias, w_in_ab, pool_w, pool_scale, w_out_ab, w_in_c, conv_w, w_out_c, ln_g, ln_b):
    assert DEPTH == 2 and w_in_ab.shape[0] == 1 and w_in_c.shape[0] == 1
    params = dict(
        biases=[_band_bias(rel_bias, d) for d in DILATIONS],
        w_ab=w_in_ab[0].astype(BF16),
        pool_w=pool_w[0].astype(BF16),
        pool_scale=pool_scale[0].reshape(1, POOL_WIDTH),
        w_out_ab=w_out_ab[0].astype(BF16),
        w_in_c=w_in_c[0].astype(BF16),
        conv_w=conv_w[0],
        w_out_c=w_out_c[0].astype(BF16),
        ln_g=ln_g,
        ln_b=ln_b,
    )
    return (_trunk(x_prompt, **params), _trunk(x_sample, **params))
```
